```python
import math
import jax, jax.numpy as jnp
from jax import lax
import numpy as np

D_MODEL = 1024
BATCH = 2
SEQ = 8192
DEPTH = 4

N_A_LAYERS = DEPTH // 2
N_B_LAYERS = DEPTH - N_A_LAYERS
BLOCK = 128
EPS = 1e-6
ROPE_THETA = 10000.0

A_HEADS = 16
A_HEAD_DIM = D_MODEL // A_HEADS

B_HEAD_DIM = 128
B_HEADS = D_MODEL // B_HEAD_DIM
B_WINDOWS = (128, 512, 2048)
B_DILATIONS = (1, 4, 16)
B_GROUPS = len(B_WINDOWS)

MOE_GROUPS = 4
MOE_EXPERTS_PER_GROUP = 4
MOE_EXPERTS = MOE_GROUPS * MOE_EXPERTS_PER_GROUP
MOE_TOPK = 2
MOE_FF = D_MODEL // 4

kernel_name = "yoco_fox_dilated_hmoe_trunk"


def rms_norm(x, gain):
    xf = x.astype(jnp.float32)
    y = xf * lax.rsqrt(jnp.mean(xf * xf, axis=-1, keepdims=True) + EPS)
    return (y * gain.astype(jnp.float32)).astype(x.dtype)


def rope(x, positions):
    half = x.shape[-1] // 2
    inv_freq = ROPE_THETA ** (-jnp.arange(half, dtype=jnp.float32) / half)
    ang = positions.astype(jnp.float32)[..., None] * inv_freq
    cos = jnp.cos(ang)[:, :, None, :]
    sin = jnp.sin(ang)[:, :, None, :]
    xf = x.astype(jnp.float32)
    x1, x2 = xf[..., :half], xf[..., half:]
    out = jnp.concatenate([x1 * cos - x2 * sin, x2 * cos + x1 * sin], axis=-1)
    return out.astype(x.dtype)


def forgetting_attention(h, w_in, b_f, q_gain, k_gain, w_out):
    Bsz, S, D = h.shape
    proj = h @ w_in
    q = proj[..., :D].reshape(Bsz, S, A_HEADS, A_HEAD_DIM)
    k = proj[..., D:2 * D].reshape(Bsz, S, A_HEADS, A_HEAD_DIM)
    v = proj[..., 2 * D:3 * D].reshape(Bsz, S, A_HEADS, A_HEAD_DIM)
    f_logit = proj[..., 3 * D:]
    q = rms_norm(q, q_gain)
    k = rms_norm(k, k_gain)
    log_f = jax.nn.log_sigmoid(f_logit.astype(jnp.float32) + b_f.astype(jnp.float32))
    c = jnp.cumsum(log_f, axis=1)
    c_k = c.transpose(0, 2, 1)
    v32 = v.astype(jnp.float32)
    scale = A_HEAD_DIM ** -0.5
    nb = S // BLOCK
    q_blk = q.reshape(Bsz, nb, BLOCK, A_HEADS, A_HEAD_DIM).transpose(1, 0, 2, 3, 4)
    c_blk = c.reshape(Bsz, nb, BLOCK, A_HEADS).transpose(1, 0, 2, 3)
    starts = jnp.arange(nb, dtype=jnp.int32) * BLOCK
    k_pos = jnp.arange(S, dtype=jnp.int32)

    def one_block(args):
        qb, cb, start = args
        s = jnp.einsum('bqhd,bshd->bhqs', qb, k, preferred_element_type=jnp.float32) * scale
        s = s + cb.transpose(0, 2, 1)[..., None] - c_k[:, :, None, :]
        q_pos = start + jnp.arange(BLOCK, dtype=jnp.int32)
        causal = k_pos[None, :] <= q_pos[:, None]
        s = jnp.where(causal, s, -jnp.inf)
        p = jax.nn.softmax(s, axis=-1)
        return jnp.einsum('bhqs,bshd->bqhd', p, v32).astype(h.dtype)

    o = lax.map(one_block, (q_blk, c_blk, starts))
    o = o.transpose(1, 0, 2, 3, 4).reshape(Bsz, S, D)
    return o @ w_out


def shared_kv(h_stream, kv_norm, kv_w, k_gain, positions):
    Bsz, S, _ = h_stream.shape
    kv = rms_norm(h_stream, kv_norm) @ kv_w
    width = B_HEADS * B_HEAD_DIM
    k = kv[..., :width].reshape(Bsz, S, B_HEADS, B_HEAD_DIM)
    v = kv[..., width:].reshape(Bsz, S, B_HEADS, B_HEAD_DIM)
    k = rope(rms_norm(k, k_gain), positions)
    return k, v


def dilated_group(q, k, v, window, dilation):
    Bsz, S, H, Dh = q.shape
    band = window // dilation
    span = dilation * BLOCK
    s_pad = -(-S // span) * span
    L = s_pad // dilation
    nb = L // BLOCK

    def strided(t):
        X = t.shape[-1]
        t = jnp.pad(t, ((0, 0), (0, s_pad - S), (0, 0), (0, 0)))
        t = t.reshape(Bsz, L, dilation, H, X).transpose(0, 2, 1, 3, 4)
        return t.reshape(Bsz, dilation, nb, BLOCK, H, X)

    def with_prev(t):
        prev = jnp.pad(t, ((0, 0), (0, 0), (1, 0), (0, 0), (0, 0), (0, 0)))[:, :, :-1]
        return jnp.concatenate([prev, t], axis=3)

    qs = strided(q)
    kb = with_prev(strided(k))
    vb = with_prev(strided(v)).astype(jnp.float32)
    s = jnp.einsum('brnqhd,brnkhd->brnhqk', qs, kb,
                   preferred_element_type=jnp.float32) * (Dh ** -0.5)
    qi = jnp.arange(BLOCK)[:, None]
    kk = jnp.arange(2 * BLOCK)[None, :]
    dist = BLOCK + qi - kk
    in_band = (dist >= 0) & (dist <= band)
    before_start = (jnp.arange(nb)[:, None, None] == 0) & (kk[None] < BLOCK)
    valid = in_band[None] & ~before_start
    s = jnp.where(valid[None, None, :, None, :, :], s, -jnp.inf)
    m = jnp.max(s, axis=-1, keepdims=True)
    p = jnp.exp(s - m)
    l = jnp.sum(p, axis=-1, keepdims=True)
    o = jnp.einsum('brnhqk,brnkhd->brnqhd', p, vb) / l.transpose(0, 1, 2, 4, 3, 5)
    lse = (m + jnp.log(l))[..., 0].transpose(0, 1, 2, 4, 3)

    def unstride(t):
        X = t.shape[-1]
        t = t.reshape(Bsz, dilation, L, H, X).transpose(0, 2, 1, 3, 4)
        return t.reshape(Bsz, s_pad, H, X)[:, :S]

    return unstride(o), unstride(lse[..., None])[..., 0]


def dilated_attention(h, positions, k, v, w_q, q_gain, w_out):
    Bsz, S, D = h.shape
    q = (h @ w_q).reshape(Bsz, S, B_GROUPS, B_HEADS, B_HEAD_DIM)
    outs, lses = [], []
    for g in range(B_GROUPS):
        qg = rope(rms_norm(q[:, :, g], q_gain[g]), positions)
        o_g, lse_g = dilated_group(qg, k, v, B_WINDOWS[g], B_DILATIONS[g])
        outs.append(o_g)
        lses.append(lse_g)
    wgt = jax.nn.softmax(jnp.stack(lses, axis=0), axis=0)
    o = jnp.sum(wgt[..., None] * jnp.stack(outs, axis=0), axis=0)
    return o.reshape(Bsz, S, B_HEADS * B_HEAD_DIM).astype(h.dtype) @ w_out


def hier_moe(h, group_w, group_b, expert_w, expert_b, w_gate, w_up, w_down):
    Bsz, S, D = h.shape
    xt = h.reshape(Bsz * S, D)
    g_prob = jax.nn.softmax((xt @ group_w).astype(jnp.float32) + group_b.astype(jnp.float32), axis=-1)
    g_p, g_idx = lax.top_k(g_prob, 1)
    e_logits = ((xt @ expert_w).astype(jnp.float32) + expert_b.astype(jnp.float32))
    e_logits = e_logits.reshape(-1, MOE_GROUPS, MOE_EXPERTS_PER_GROUP)
    in_group = jnp.take_along_axis(e_logits, g_idx[:, :, None], axis=1)[:, 0]
    top_v, top_i = lax.top_k(in_group, MOE_TOPK)
    gate = g_p * jax.nn.softmax(top_v, axis=-1)
    eid = g_idx * MOE_EXPERTS_PER_GROUP + top_i
    comb = jnp.sum(jax.nn.one_hot(eid, MOE_EXPERTS, dtype=jnp.float32) * gate[..., None], axis=1)
    hid = jax.nn.silu(jnp.einsum('td,edf->tef', xt, w_gate)) * jnp.einsum('td,edf->tef', xt, w_up)
    y = jnp.einsum('tef,efd->td', hid * comb[..., None].astype(hid.dtype), w_down)
    return y.reshape(Bsz, S, D)


def setup_inputs(seed: int = 0) -> dict:
    key = jax.random.key(seed)
    ks = jax.random.split(key, 24)
    f32 = jnp.float32
    D = D_MODEL
    res_scale = (2.0 * DEPTH) ** -0.5
    bw = B_HEADS * B_HEAD_DIM

    def nrm(k, shape, scale):
        return jax.random.normal(k, shape, f32) * scale

    def gain(k, shape):
        return 1.0 + 0.02 * jax.random.normal(k, shape, f32)

    x = jax.random.normal(ks[0], (BATCH, SEQ, D), f32)
    positions = jnp.broadcast_to(jnp.arange(SEQ, dtype=jnp.int32), (BATCH, SEQ))
    return {
        "x": x,
        "positions": positions,
        "a_norm": gain(ks[1], (N_A_LAYERS, D)),
        "a_w_in": nrm(ks[2], (N_A_LAYERS, D, 3 * D + A_HEADS), D ** -0.5),
        "a_b_f": jax.random.uniform(ks[3], (N_A_LAYERS, A_HEADS), f32, 1.0, 5.0),
        "a_q_gain": gain(ks[4], (N_A_LAYERS, A_HEAD_DIM)),
        "a_k_gain": gain(ks[5], (N_A_LAYERS, A_HEAD_DIM)),
        "a_w_out": nrm(ks[6], (N_A_LAYERS, D, D), D ** -0.5 * res_scale),
        "kv_norm": gain(ks[7], (D,)),
        "kv_w": nrm(ks[8], (D, 2 * bw), D ** -0.5),
        "kv_k_gain": gain(ks[9], (B_HEAD_DIM,)),
        "b_norm": gain(ks[10], (N_B_LAYERS, D)),
        "b_w_q": nrm(ks[11], (N_B_LAYERS, D, B_GROUPS * bw), D ** -0.5),
        "b_q_gain": gain(ks[12], (N_B_LAYERS, B_GROUPS, B_HEAD_DIM)),
        "b_w_out": nrm(ks[13], (N_B_LAYERS, bw, D), bw ** -0.5 * res_scale),
        "ffn_norm": gain(ks[14], (DEPTH, D)),
        "moe_group_w": nrm(ks[15], (DEPTH, D, MOE_GROUPS), D ** -0.5),
        "moe_group_b": nrm(ks[16], (DEPTH, MOE_GROUPS), 0.01),
        "moe_expert_w": nrm(ks[17], (DEPTH, D, MOE_EXPERTS), D ** -0.5),
        "moe_expert_b": nrm(ks[18], (DEPTH, MOE_EXPERTS), 0.01),
        "moe_w_gate": nrm(ks[19], (DEPTH, MOE_EXPERTS, D, MOE_FF), D ** -0.5),
        "moe_w_up": nrm(ks[20], (DEPTH, MOE_EXPERTS, D, MOE_FF), D ** -0.5),
        "moe_w_down": nrm(ks[21], (DEPTH, MOE_EXPERTS, MOE_FF, D), MOE_FF ** -0.5 * res_scale),
    }


def reference(x, positions, a_norm, a_w_in, a_b_f, a_q_gain, a_k_gain, a_w_out,
              kv_norm, kv_w, kv_k_gain, b_norm, b_w_q, b_q_gain, b_w_out,
              ffn_norm, moe_group_w, moe_group_b, moe_expert_w, moe_expert_b,
              moe_w_gate, moe_w_up, moe_w_down):
    h = x
    k_sh = None
    v_sh = None
    for layer in range(DEPTH):
        if layer < N_A_LAYERS:
            i = layer
            h = h + forgetting_attention(rms_norm(h, a_norm[i]), a_w_in[i], a_b_f[i],
                                         a_q_gain[i], a_k_gain[i], a_w_out[i])
        else:
            if layer == N_A_LAYERS:
                k_sh, v_sh = shared_kv(h, kv_norm, kv_w, kv_k_gain, positions)
            j = layer - N_A_LAYERS
            h = h + dilated_attention(rms_norm(h, b_norm[j]), positions, k_sh, v_sh,
                                      b_w_q[j], b_q_gain[j], b_w_out[j])
        h = h + hier_moe(rms_norm(h, ffn_norm[layer]), moe_group_w[layer], moe_group_b[layer],
                         moe_expert_w[layer], moe_expert_b[layer],
                         moe_w_gate[layer], moe_w_up[layer], moe_w_down[layer])
    return h
```

```python
import functools
import math

import numpy as np
import jax
import jax.numpy as jnp
from jax import lax
from jax.experimental import pallas as pl
from jax.experimental.pallas import tpu as pltpu

F32 = jnp.float32
BF16 = jnp.bfloat16

D_MODEL = 1024
EPS = 1e-6
ROPE_THETA = 10000.0
A_HEADS = 16
A_HEAD_DIM = 64
B_HEADS = 8
B_HEAD_DIM = 128
B_WINDOWS = (128, 512, 2048)
B_DILATIONS = (1, 4, 16)
BAND_BLOCK = 128
MOE_GROUPS = 4
MOE_EXPERTS = 16
MOE_FF = 256

LANES = 128
MXU_DIM = 256
VMEM_LIMIT = 56 * 1024 * 1024

NEG_INF = float("-inf")

X_BASE = A_HEAD_DIM
ROUTE_E0 = MOE_GROUPS


def _params(sem):
    return pltpu.CompilerParams(dimension_semantics=sem, vmem_limit_bytes=VMEM_LIMIT)


def _rms_rows(x, gain_row):
    ms = jnp.mean(x * x, axis=-1, keepdims=True)
    return x * lax.rsqrt(ms + EPS) * gain_row


def _dot(a, b):
    return jnp.dot(a, b, preferred_element_type=F32)


def _dot_nt(a, b):
    return lax.dot_general(a, b, (((1,), (1,)), ((), ())), preferred_element_type=F32)


def _split3(x):
    p1 = x.astype(BF16)
    r1 = x - p1.astype(F32)
    p2 = r1.astype(BF16)
    p3 = (r1 - p2.astype(F32)).astype(BF16)
    return p1, p2, p3


def _seg_norm(x, ones_bd, seg):
    ss = _dot((x * x).astype(BF16), ones_bd)
    return x * lax.rsqrt(ss * (1.0 / seg) + EPS)


def _rope_table_kernel(pos_ref, inv_ref, sign_ref, cos_ref, sin_ref):
    ang = pos_ref[...].astype(F32) * inv_ref[...]
    cos_ref[...] = jnp.cos(ang)
    sin_ref[...] = jnp.sin(ang) * sign_ref[...]


def _rope_tables(positions):
    t = positions.size
    tm = 1024
    half = B_HEAD_DIM // 2
    inv = ROPE_THETA ** (-jnp.arange(half, dtype=F32) / half)
    inv = jnp.concatenate([inv, inv]).reshape(1, B_HEAD_DIM)
    sign = jnp.concatenate([-jnp.ones((half,), F32), jnp.ones((half,), F32)]).reshape(1, B_HEAD_DIM)
    row = pl.BlockSpec((1, B_HEAD_DIM), lambda i: (0, 0))
    tab = pl.BlockSpec((tm, B_HEAD_DIM), lambda i: (i, 0))
    return pl.pallas_call(
        _rope_table_kernel,
        grid=(t // tm,),
        in_specs=[pl.BlockSpec((tm, 1), lambda i: (i, 0)), row, row],
        out_specs=[tab, tab],
        out_shape=[jax.ShapeDtypeStruct((t, B_HEAD_DIM), F32)] * 2,
        compiler_params=_params(("parallel",)),
        name="rope_tables",
    )(positions.reshape(t, 1), inv, sign)


def _rope(x, cos, sin_signed):
    return x * cos + pltpu.roll(x, B_HEAD_DIM // 2, 1) * sin_signed


def _a_proj_kernel(tiles_per_seq, h_ref, g_ref, wqkv_ref, wf_ref, bf_ref, qg_ref, kg_ref,
                   ones_ref, tri_ref, selq_ref, selk_ref, q_ref, k_ref, v_ref, carry_ref):
    tm = h_ref.shape[0]

    @pl.when(pl.program_id(0) % tiles_per_seq == 0)
    def _():
        carry_ref[...] = jnp.zeros_like(carry_ref)

    xn = _rms_rows(h_ref[...], g_ref[...]).astype(BF16)

    fl = _dot(xn, wf_ref[...]) + bf_ref[...]
    lf = jnp.minimum(fl, 0.0) - jnp.log(1.0 + jnp.exp(-jnp.abs(fl)))
    tri = tri_ref[...]
    l1, l2, l3 = _split3(lf)
    c = _dot(tri, l1) + _dot(tri, l2) + _dot(tri, l3) + carry_ref[0:1, :]
    carry_ref[0:1, :] = c[tm - 1:tm, :]

    lane = lax.broadcasted_iota(jnp.int32, (tm, LANES), 1)
    c1 = c.astype(BF16).astype(F32)
    r1 = c - c1
    c2 = r1.astype(BF16).astype(F32)
    c3 = (r1 - c2).astype(BF16).astype(F32)
    e = jnp.where(lane < 16, c1,
                  jnp.where(lane < 32, pltpu.roll(c2, 16, 1),
                            jnp.where(lane < 48, pltpu.roll(c3, 32, 1),
                                      jnp.where(lane == 48, 1.0, 0.0)))).astype(BF16)
    v_extra = jnp.where(lane == X_BASE, 1.0, 0.0)
    low = lane < A_HEAD_DIM

    ones_bd = ones_ref[...]
    n_chunks = D_MODEL // MXU_DIM
    for ch in range(n_chunks):
        cs = slice(ch * MXU_DIM, (ch + 1) * MXU_DIM)
        qf = _dot(xn, wqkv_ref[:, cs])
        kf = _dot(xn, wqkv_ref[:, D_MODEL + ch * MXU_DIM:D_MODEL + (ch + 1) * MXU_DIM])
        vf = _dot(xn, wqkv_ref[:, 2 * D_MODEL + ch * MXU_DIM:2 * D_MODEL + (ch + 1) * MXU_DIM])
        qn = _seg_norm(qf, ones_bd, A_HEAD_DIM) * qg_ref[:, cs]
        kn = _seg_norm(kf, ones_bd, A_HEAD_DIM) * kg_ref[:, cs]
        heads_per_chunk = MXU_DIM // A_HEAD_DIM
        hs = slice(ch * heads_per_chunk * LANES, (ch + 1) * heads_per_chunk * LANES)
        exq = _dot(e, selq_ref[:, hs])
        exk = _dot(e, selk_ref[:, hs])
        for j in range(heads_per_chunk):
            pair = slice((j // 2) * LANES, (j // 2 + 1) * LANES)
            blk = slice(j * LANES, (j + 1) * LANES)
            out = slice((ch * heads_per_chunk + j) * LANES, (ch * heads_per_chunk + j + 1) * LANES)
            qp, kp, vp = qn[:, pair], kn[:, pair], vf[:, pair]
            if j % 2 == 1:
                qp = pltpu.roll(qp, A_HEAD_DIM, 1)
                kp = pltpu.roll(kp, A_HEAD_DIM, 1)
                vp = pltpu.roll(vp, A_HEAD_DIM, 1)
            q_ref[:, out] = jnp.where(low, qp, exq[:, blk]).astype(BF16)
            k_ref[:, out] = jnp.where(low, kp, exk[:, blk]).astype(BF16)
            v_ref[:, out] = jnp.where(low, vp, v_extra).astype(BF16)


def _a_sel_matrices():
    selq = np.zeros((LANES, A_HEADS * LANES), np.float32)
    selk = np.zeros((LANES, A_HEADS * LANES), np.float32)
    for h in range(A_HEADS):
        base = h * LANES + X_BASE
        for piece in range(3):
            selq[16 * piece + h, base + piece] = 1.0
            selq[48, base + 3 + piece] = 1.0
            selk[48, base + piece] = 1.0
            selk[16 * piece + h, base + 3 + piece] = -1.0
    return jnp.asarray(selq, BF16), jnp.asarray(selk, BF16)


def _block_diag_ones(seg):
    idx = np.arange(MXU_DIM) // seg
    return jnp.asarray((idx[:, None] == idx[None, :]).astype(np.float32), BF16)


def _a_proj(h2d, seq, norm_gain, w_in, b_f, q_gain, k_gain):
    t = h2d.shape[0]
    tm = 512
    wide = A_HEADS * LANES
    w_qkv = w_in[:, :3 * D_MODEL].astype(BF16)
    w_f = jnp.pad(w_in[:, 3 * D_MODEL:], ((0, 0), (0, LANES - A_HEADS))).astype(BF16)
    b_row = jnp.pad(b_f, (0, LANES - A_HEADS)).reshape(1, LANES)
    qg = (jnp.tile(q_gain, A_HEADS) * (A_HEAD_DIM ** -0.5)).reshape(1, D_MODEL)
    kg = jnp.tile(k_gain, A_HEADS).reshape(1, D_MODEL)
    tri = jnp.asarray(np.tril(np.ones((tm, tm), np.float32)), BF16)
    selq, selk = _a_sel_matrices()
    full = lambda shape: pl.BlockSpec(shape, lambda i: (0,) * len(shape))
    out_spec = pl.BlockSpec((tm, wide), lambda i: (i, 0))
    return pl.pallas_call(
        functools.partial(_a_proj_kernel, seq // tm),
        grid=(t // tm,),
        in_specs=[pl.BlockSpec((tm, D_MODEL), lambda i: (i, 0)),
                  full((1, D_MODEL)), full((D_MODEL, 3 * D_MODEL)), full((D_MODEL, LANES)),
                  full((1, LANES)), full((1, D_MODEL)), full((1, D_MODEL)),
                  full((MXU_DIM, MXU_DIM)), full((tm, tm)), full((LANES, wide)), full((LANES, wide))],
        out_specs=[out_spec] * 3,
        out_shape=[jax.ShapeDtypeStruct((t, wide), BF16)] * 3,
        scratch_shapes=[pltpu.VMEM((8, LANES), F32)],
        compiler_params=_params(("arbitrary",)),
        name="a_proj",
    )(h2d, norm_gain.reshape(1, D_MODEL), w_qkv, w_f, b_row, qg, kg,
      _block_diag_ones(A_HEAD_DIM), tri, selq, selk)


def _fox_kernel(q_ref, k_ref, v_ref, o_ref):
    tq = q_ref.shape[0]
    tk = tq
    qi = pl.program_id(2)
    qs = [q_ref[:, hh * LANES:(hh + 1) * LANES] for hh in range(2)]
    row = lax.broadcasted_iota(jnp.int32, (tq, tk), 0)
    col = lax.broadcasted_iota(jnp.int32, (tq, tk), 1)
    causal = col <= row

    def chunk(j, carry, masked):
        start = pl.multiple_of(j * tk, tk)
        new = []
        for hh in range(2):
            m, acc = carry[2 * hh], carry[2 * hh + 1]
            kc = k_ref[pl.ds(start, tk), hh * LANES:(hh + 1) * LANES]
            vc = v_ref[pl.ds(start, tk), hh * LANES:(hh + 1) * LANES]
            s = _dot_nt(qs[hh], kc)
            if masked:
                s = jnp.where(causal, s, NEG_INF)
            m_new = jnp.maximum(m, jnp.max(s, axis=-1, keepdims=True))
            alpha = jnp.exp(m - m_new)
            p = jnp.exp(s - m_new)
            acc = alpha * acc + _dot(p.astype(BF16), vc)
            new += [m_new, acc]
        return tuple(new)

    init = (jnp.full((tq, 1), NEG_INF, F32), jnp.zeros((tq, LANES), F32)) * 2
    carry = lax.fori_loop(0, qi, functools.partial(chunk, masked=False), init)
    carry = chunk(qi, carry, True)
    outs = [carry[2 * hh + 1] / carry[2 * hh + 1][:, X_BASE:X_BASE + 1] for hh in range(2)]
    lane = lax.broadcasted_iota(jnp.int32, (tq, LANES), 1)
    o_ref[...] = jnp.where(lane < A_HEAD_DIM, outs[0],
                           pltpu.roll(outs[1], A_HEAD_DIM, 1)).astype(o_ref.dtype)


def _fox_attention(q, k, v, batch, seq):
    tq = 512
    pairs = A_HEADS // 2
    q3, k3, v3 = (a.reshape(batch, seq, A_HEADS * LANES) for a in (q, k, v))
    kv_spec = pl.BlockSpec((None, seq, 2 * LANES), lambda b, p, i: (b, 0, p))
    return pl.pallas_call(
        _fox_kernel,
        grid=(batch, pairs, seq // tq),
        in_specs=[pl.BlockSpec((None, tq, 2 * LANES), lambda b, p, i: (b, i, p)), kv_spec, kv_spec],
        out_specs=pl.BlockSpec((None, tq, LANES), lambda b, p, i: (b, i, p)),
        out_shape=jax.ShapeDtypeStruct((batch, seq, D_MODEL), BF16),
        compiler_params=_params(("parallel", "parallel", "arbitrary")),
        name="fox_attention",
    )(q3, k3, v3)


def _out_proj_kernel(h_ref, o_ref, w_ref, out_ref):
    out_ref[...] = h_ref[...] + _dot(o_ref[...], w_ref[...])


def _out_proj(h2d, o2d, w_out):
    t = h2d.shape[0]
    tm = 1024
    tile = pl.BlockSpec((tm, D_MODEL), lambda i: (i, 0))
    return pl.pallas_call(
        _out_proj_kernel,
        grid=(t // tm,),
        in_specs=[tile, tile, pl.BlockSpec((D_MODEL, D_MODEL), lambda i: (0, 0))],
        out_specs=tile,
        out_shape=jax.ShapeDtypeStruct((t, D_MODEL), F32),
        compiler_params=_params(("parallel",)),
        name="out_proj",
    )(h2d, o2d, w_out.astype(BF16))


def _b_proj_kernel(n_rope_cols, h_ref, g_ref, w_ref, gain_ref, ones_ref, cos_ref, sin_ref, out_ref):
    xn = _rms_rows(h_ref[...], g_ref[...]).astype(BF16)
    cos, sin = cos_ref[...], sin_ref[...]
    ones_bd = ones_ref[...]
    for ch in range(w_ref.shape[1] // MXU_DIM):
        cs = slice(ch * MXU_DIM, (ch + 1) * MXU_DIM)
        y = _dot(xn, w_ref[:, cs])
        if ch * MXU_DIM < n_rope_cols:
            y = _seg_norm(y, ones_bd, B_HEAD_DIM) * gain_ref[:, cs]
            for j in range(MXU_DIM // B_HEAD_DIM):
                blk = slice(j * B_HEAD_DIM, (j + 1) * B_HEAD_DIM)
                out_ref[:, ch * MXU_DIM + j * B_HEAD_DIM:ch * MXU_DIM + (j + 1) * B_HEAD_DIM] = (
                    _rope(y[:, blk], cos, sin).astype(out_ref.dtype))
        else:
            out_ref[:, cs] = y.astype(out_ref.dtype)


def _b_proj(h2d, norm_gain, w, head_gain_row, n_rope_cols, cos, sin):
    t = h2d.shape[0]
    n = w.shape[1]
    tm = 512
    full = lambda shape: pl.BlockSpec(shape, lambda i: (0,) * len(shape))
    return pl.pallas_call(
        functools.partial(_b_proj_kernel, n_rope_cols),
        grid=(t // tm,),
        in_specs=[pl.BlockSpec((tm, D_MODEL), lambda i: (i, 0)), full((1, D_MODEL)),
                  full((D_MODEL, n)), full((1, n)), full((MXU_DIM, MXU_DIM)),
                  pl.BlockSpec((tm, B_HEAD_DIM), lambda i: (i, 0)),
                  pl.BlockSpec((tm, B_HEAD_DIM), lambda i: (i, 0))],
        out_specs=pl.BlockSpec((tm, n), lambda i: (i, 0)),
        out_shape=jax.ShapeDtypeStruct((t, n), BF16),
        compiler_params=_params(("parallel",)),
        name="b_proj",
    )(h2d, norm_gain.reshape(1, D_MODEL), w.astype(BF16), head_gain_row,
      _block_diag_ones(B_HEAD_DIM), cos, sin)


def _dilated_kernel(q_ref, kc_ref, kp_ref, vc_ref, vp_ref, bp_ref, bc_ref, o_ref, lse_ref):
    tl = q_ref.shape[0]
    blk = BAND_BLOCK
    first_pen = jnp.where(pl.program_id(2) == 0, NEG_INF, 0.0)
    bias_prev, bias_cur = bp_ref[...], bc_ref[...]
    lane = lax.broadcasted_iota(jnp.int32, (blk, LANES), 1)
    for i in range(tl // blk):
        rows = slice(i * blk, (i + 1) * blk)
        lse_tile = jnp.zeros((blk, LANES), F32)
        for h in range(B_HEADS):
            hs = slice(h * B_HEAD_DIM, (h + 1) * B_HEAD_DIM)
            q = q_ref[rows, hs]
            if i == 0:
                k_prev, v_prev = kp_ref[:, hs], vp_ref[:, hs]
            else:
                prev_rows = slice((i - 1) * blk, i * blk)
                k_prev, v_prev = kc_ref[prev_rows, hs], vc_ref[prev_rows, hs]
            s_prev = _dot_nt(q, k_prev) + bias_prev
            if i == 0:
                s_prev = s_prev + first_pen
            s_cur = _dot_nt(q, kc_ref[rows, hs]) + bias_cur
            m = jnp.maximum(jnp.max(s_prev, axis=-1, keepdims=True),
                            jnp.max(s_cur, axis=-1, keepdims=True))
            p_prev = jnp.exp(s_prev - m)
            p_cur = jnp.exp(s_cur - m)
            l = jnp.sum(p_prev, axis=-1, keepdims=True) + jnp.sum(p_cur, axis=-1, keepdims=True)
            o = (_dot(p_prev.astype(BF16), v_prev) + _dot(p_cur.astype(BF16), vc_ref[rows, hs])) / l
            o_ref[rows, hs] = o.astype(o_ref.dtype)
            lse_tile = jnp.where(lane == h, m + jnp.log(l), lse_tile)
        lse_ref[rows, :] = lse_tile


def _band_biases():
    qi = np.arange(BAND_BLOCK)[:, None]
    kk = np.arange(BAND_BLOCK)[None, :]
    prev = np.where(kk >= qi, 0.0, -np.inf).astype(np.float32)
    cur = np.where(kk <= qi, 0.0, -np.inf).astype(np.float32)
    return jnp.asarray(prev), jnp.asarray(cur)


def _dilated_group(q_all, kv, group, batch, seq):
    d = B_DILATIONS[group]
    assert B_WINDOWS[group] // d == BAND_BLOCK and seq % (d * BAND_BLOCK) == 0
    length = seq // d
    tl = 256
    width = B_HEADS * B_HEAD_DIM
    sub = tl // BAND_BLOCK
    n_groups = len(B_DILATIONS)
    qv = q_all.reshape(batch, length, d * n_groups * width)
    kvv = kv.reshape(batch, length, d * 2 * width)

    def cur(part):
        return pl.BlockSpec((None, tl, width), lambda b, r, n: (b, n, r * 2 + part))

    def prev(part):
        return pl.BlockSpec((None, BAND_BLOCK, width),
                            lambda b, r, n: (b, jnp.maximum(n * sub - 1, 0), r * 2 + part))

    bias = pl.BlockSpec((BAND_BLOCK, BAND_BLOCK), lambda b, r, n: (0, 0))
    bias_prev, bias_cur = _band_biases()
    o, lse = pl.pallas_call(
        _dilated_kernel,
        grid=(batch, d, length // tl),
        in_specs=[pl.BlockSpec((None, tl, width), lambda b, r, n: (b, n, r * n_groups + group)),
                  cur(0), prev(0), cur(1), prev(1), bias, bias],
        out_specs=[pl.BlockSpec((None, tl, width), lambda b, r, n: (b, n, r)),
                   pl.BlockSpec((None, tl, LANES), lambda b, r, n: (b, n, r))],
        out_shape=[jax.ShapeDtypeStruct((batch, length, d * width), BF16),
                   jax.ShapeDtypeStruct((batch, length, d * LANES), F32)],
        compiler_params=_params(("parallel", "parallel", "arbitrary")),
        name=f"dilated_g{group}",
    )(qv, kvv, kvv, kvv, kvv, bias_prev, bias_cur)
    return o.reshape(batch * seq, width), lse.reshape(batch * seq, LANES)


def _b_out_kernel(h_ref, o0_ref, o1_ref, o2_ref, l0_ref, l1_ref, l2_ref, w_ref, out_ref, merged_ref):
    lses = [l0_ref[...], l1_ref[...], l2_ref[...]]
    top = jnp.maximum(jnp.maximum(lses[0], lses[1]), lses[2])
    es = [jnp.exp(x - top) for x in lses]
    den = es[0] + es[1] + es[2]
    ws = [x / den for x in es]
    o_refs = (o0_ref, o1_ref, o2_ref)
    for h in range(B_HEADS):
        hs = slice(h * B_HEAD_DIM, (h + 1) * B_HEAD_DIM)
        acc = ws[0][:, h:h + 1] * o_refs[0][:, hs].astype(F32)
        for g in (1, 2):
            acc = acc + ws[g][:, h:h + 1] * o_refs[g][:, hs].astype(F32)
        merged_ref[:, hs] = acc.astype(BF16)
    out_ref[...] = h_ref[...] + _dot(merged_ref[...], w_ref[...])


def _b_out(h2d, outs, lses, w_out):
    t = h2d.shape[0]
    tm = 512
    tile = pl.BlockSpec((tm, D_MODEL), lambda i: (i, 0))
    ltile = pl.BlockSpec((tm, LANES), lambda i: (i, 0))
    return pl.pallas_call(
        _b_out_kernel,
        grid=(t // tm,),
        in_specs=[tile, tile, tile, tile, ltile, ltile, ltile,
                  pl.BlockSpec((D_MODEL, D_MODEL), lambda i: (0, 0))],
        out_specs=tile,
        out_shape=jax.ShapeDtypeStruct((t, D_MODEL), F32),
        scratch_shapes=[pltpu.VMEM((tm, D_MODEL), BF16)],
        compiler_params=_params(("parallel",)),
        name="b_out",
    )(h2d, *outs, *lses, w_out.astype(BF16))


def _route(logits):
    tm = logits.shape[0]
    lane = lax.broadcasted_iota(jnp.int32, (tm, LANES), 1)
    lanef = lane.astype(F32)
    far = float(LANES)
    is_g = lane < MOE_GROUPS
    g_max = jnp.max(jnp.where(is_g, logits, NEG_INF), axis=-1, keepdims=True)
    g_sum = jnp.sum(jnp.where(is_g, jnp.exp(logits - g_max), 0.0), axis=-1, keepdims=True)
    g_idx = jnp.min(jnp.where(is_g, jnp.where(logits == g_max, lanef, far), far), axis=-1, keepdims=True)
    per_group = MOE_EXPERTS // MOE_GROUPS
    e_lo = ROUTE_E0 + g_idx * per_group
    in_group = jnp.where(lanef >= e_lo, jnp.where(lanef < e_lo + per_group, 1.0, 0.0), 0.0)
    cand1 = jnp.where(in_group > 0.0, logits, NEG_INF)
    v1 = jnp.max(cand1, axis=-1, keepdims=True)
    i1 = jnp.min(jnp.where(cand1 == v1, lanef, far), axis=-1, keepdims=True)
    cand2 = jnp.where(lanef == i1, NEG_INF, cand1)
    v2 = jnp.max(cand2, axis=-1, keepdims=True)
    i2 = jnp.min(jnp.where(cand2 == v2, lanef, far), axis=-1, keepdims=True)
    e21 = jnp.exp(v2 - v1)
    w1 = 1.0 / (1.0 + e21)
    w2 = e21 / (1.0 + e21)
    return (jnp.where(lanef == i1, w1, 0.0) + jnp.where(lanef == i2, w2, 0.0)) / g_sum


def _moe_kernel(h_ref, g_ref, wr1_ref, wr2_ref, wr3_ref, br_ref, wgu_ref, wd_ref, out_ref,
                xn_ref, comb_ref, acc_ref):
    e = pl.program_id(1)

    @pl.when(e == 0)
    def _():
        xn = _rms_rows(h_ref[...], g_ref[...])
        x1, x2, x3 = _split3(xn)
        w1, w2, w3 = wr1_ref[...], wr2_ref[...], wr3_ref[...]
        logits = (_dot(x1, w1) + (_dot(x1, w2) + _dot(x2, w1))
                  + (_dot(x2, w2) + _dot(x1, w3) + _dot(x3, w1))) + br_ref[...]
        comb_ref[...] = _route(logits)
        xn_ref[...] = x1
        acc_ref[...] = jnp.zeros_like(acc_ref)

    tm = h_ref.shape[0]
    lane = lax.broadcasted_iota(jnp.int32, (tm, LANES), 1)
    gate_w = jnp.sum(jnp.where(lane == e + ROUTE_E0, comb_ref[...], 0.0), axis=-1, keepdims=True)
    gu = _dot(xn_ref[...], wgu_ref[...])
    g, u = gu[:, :MOE_FF], gu[:, MOE_FF:]
    hid = (g / (1.0 + jnp.exp(-g))) * u * gate_w
    acc_ref[...] += _dot(hid.astype(BF16), wd_ref[...])

    @pl.when(e == MOE_EXPERTS - 1)
    def _():
        out_ref[...] = h_ref[...] + acc_ref[...]


def _moe(h2d, norm_gain, group_w, group_b, expert_w, expert_b, w_gate, w_up, w_down):
    t = h2d.shape[0]
    tm = 1024
    pad = LANES - MOE_GROUPS - MOE_EXPERTS
    w_r = jnp.pad(jnp.concatenate([group_w, expert_w], axis=1), ((0, 0), (0, pad)))
    r1 = w_r.astype(BF16)
    rem = w_r - r1.astype(F32)
    r2 = rem.astype(BF16)
    r3 = (rem - r2.astype(F32)).astype(BF16)
    b_r = jnp.pad(jnp.concatenate([group_b, expert_b]), (0, pad)).reshape(1, LANES)
    w_gu = jnp.concatenate([w_gate, w_up], axis=2).astype(BF16)
    w_d = w_down.astype(BF16)
    tile = pl.BlockSpec((tm, D_MODEL), lambda i, e: (i, 0))
    full = lambda shape: pl.BlockSpec(shape, lambda i, e: (0,) * len(shape))
    return pl.pallas_call(
        _moe_kernel,
        grid=(t // tm, MOE_EXPERTS),
        in_specs=[tile, full((1, D_MODEL)), full((D_MODEL, LANES)), full((D_MODEL, LANES)),
                  full((D_MODEL, LANES)), full((1, LANES)),
                  pl.BlockSpec((None, D_MODEL, 2 * MOE_FF), lambda i, e: (e, 0, 0)),
                  pl.BlockSpec((None, MOE_FF, D_MODEL), lambda i, e: (e, 0, 0))],
        out_specs=tile,
        out_shape=jax.ShapeDtypeStruct((t, D_MODEL), F32),
        scratch_shapes=[pltpu.VMEM((tm, D_MODEL), BF16), pltpu.VMEM((tm, LANES), F32),
                        pltpu.VMEM((tm, D_MODEL), F32)],
        compiler_params=_params(("parallel", "arbitrary")),
        name="moe",
    )(h2d, norm_gain.reshape(1, D_MODEL), r1, r2, r3, b_r, w_gu, w_d)


def kernel(x, positions, a_norm, a_w_in, a_b_f, a_q_gain, a_k_gain, a_w_out, kv_norm, kv_w, kv_k_gain, b_norm, b_w_q, b_q_gain, b_w_out, ffn_norm, moe_group_w, moe_group_b, moe_expert_w, moe_expert_b, moe_w_gate, moe_w_up, moe_w_down):
    batch, seq, _ = x.shape
    n_a = a_norm.shape[0]
    n_b = b_norm.shape[0]
    h = x.reshape(batch * seq, D_MODEL)
    b_width = B_HEADS * B_HEAD_DIM
    kv_sh = cos = sin = None

    for layer in range(n_a + n_b):
        if layer < n_a:
            i = layer
            q, k, v = _a_proj(h, seq, a_norm[i], a_w_in[i], a_b_f[i], a_q_gain[i], a_k_gain[i])
            o = _fox_attention(q, k, v, batch, seq)
            h = _out_proj(h, o.reshape(batch * seq, D_MODEL), a_w_out[i])
        else:
            j = layer - n_a
            if j == 0:
                cos, sin = _rope_tables(positions)
                k_gain_row = jnp.concatenate([jnp.tile(kv_k_gain, B_HEADS), jnp.ones((b_width,), F32)])
                kv_sh = _b_proj(h, kv_norm, kv_w, k_gain_row.reshape(1, 2 * b_width), b_width, cos, sin)
            q_gain_row = (jnp.tile(b_q_gain[j], (1, B_HEADS)) * (B_HEAD_DIM ** -0.5)).reshape(1, 3 * b_width)
            q_all = _b_proj(h, b_norm[j], b_w_q[j], q_gain_row, 3 * b_width, cos, sin)
            outs, lses = [], []
            for g in range(len(B_DILATIONS)):
                o_g, lse_g = _dilated_group(q_all, kv_sh, g, batch, seq)
                outs.append(o_g)
                lses.append(lse_g)
            h = _b_out(h, outs, lses, b_w_out[j])
        h = _moe(h, ffn_norm[layer], moe_group_w[layer], moe_group_b[layer], moe_expert_w[layer],
                 moe_expert_b[layer], moe_w_gate[layer], moe_w_up[layer], moe_w_down[layer])
    return h.reshape(batch, seq, D_MODEL)
```

```python
import functools
import math

import numpy as np
import jax
import jax.numpy as jnp
from jax import lax
from jax.experimental import pallas as pl
from jax.experimental.pallas import tpu as pltpu

F32 = jnp.float32
BF16 = jnp.bfloat16

D_MODEL = 1024
EPS = 1e-6
ROPE_THETA = 10000.0
A_HEADS = 16
A_HEAD_DIM = 64
B_HEADS = 8
B_HEAD_DIM = 128
B_WINDOWS = (128, 512, 2048)
B_DILATIONS = (1, 4, 16)
BAND_BLOCK = 128
MOE_GROUPS = 4
MOE_EXPERTS = 16
MOE_FF = 256

LANES = 128
MXU_DIM = 256
VMEM_LIMIT = 56 * 1024 * 1024

NEG_INF = float("-inf")
LOG2E = math.log2(math.e)
FOX_BLOCK = 512
MAX_SAFE_SHIFT = 40.0

X_BASE = A_HEAD_DIM
ROUTE_E0 = MOE_GROUPS


def _params(sem):
    return pltpu.CompilerParams(dimension_semantics=sem, vmem_limit_bytes=VMEM_LIMIT)


def _rms_rows(x, gain_row):
    ms = jnp.mean(x * x, axis=-1, keepdims=True)
    return x * lax.rsqrt(ms + EPS) * gain_row


def _dot(a, b):
    return jnp.dot(a, b, preferred_element_type=F32)


def _dot_nt(a, b):
    return lax.dot_general(a, b, (((1,), (1,)), ((), ())), preferred_element_type=F32)


def _split3(x):
    p1 = x.astype(BF16)
    r1 = x - p1.astype(F32)
    p2 = r1.astype(BF16)
    p3 = (r1 - p2.astype(F32)).astype(BF16)
    return p1, p2, p3


def _seg_norm(x, ones_bd, seg):
    ss = _dot((x * x).astype(BF16), ones_bd)
    return x * lax.rsqrt(ss * (1.0 / seg) + EPS)


def _rope_table_kernel(pos_ref, inv_ref, sign_ref, cos_ref, sin_ref):
    ang = pos_ref[...].astype(F32) * inv_ref[...]
    cos_ref[...] = jnp.cos(ang)
    sin_ref[...] = jnp.sin(ang) * sign_ref[...]


def _rope_tables(positions):
    t = positions.size
    tm = 1024
    half = B_HEAD_DIM // 2
    inv = ROPE_THETA ** (-jnp.arange(half, dtype=F32) / half)
    inv = jnp.concatenate([inv, inv]).reshape(1, B_HEAD_DIM)
    sign = jnp.concatenate([-jnp.ones((half,), F32), jnp.ones((half,), F32)]).reshape(1, B_HEAD_DIM)
    row = pl.BlockSpec((1, B_HEAD_DIM), lambda i: (0, 0))
    tab = pl.BlockSpec((tm, B_HEAD_DIM), lambda i: (i, 0))
    return pl.pallas_call(
        _rope_table_kernel,
        grid=(t // tm,),
        in_specs=[pl.BlockSpec((tm, 1), lambda i: (i, 0)), row, row],
        out_specs=[tab, tab],
        out_shape=[jax.ShapeDtypeStruct((t, B_HEAD_DIM), F32)] * 2,
        compiler_params=_params(("parallel",)),
        name="rope_tables",
    )(positions.reshape(t, 1), inv, sign)


def _rope(x, cos, sin_signed):
    return x * cos + pltpu.roll(x, B_HEAD_DIM // 2, 1) * sin_signed


def _a_proj_kernel(tiles_per_seq, h_ref, g_ref, wqkv_ref, wf_ref, bf_ref, qg_ref, kg_ref,
                   ones_ref, tri_ref, selq_ref, selk_ref, shift_ref, q_ref, k_ref, vt_ref, carry_ref):
    tm = h_ref.shape[0]

    @pl.when(pl.program_id(0) % tiles_per_seq == 0)
    def _():
        carry_ref[...] = jnp.zeros_like(carry_ref)

    xn = _rms_rows(h_ref[...], g_ref[...]).astype(BF16)

    fl = _dot(xn, wf_ref[...]) + bf_ref[...]
    lf = jnp.minimum(fl, 0.0) - jnp.log(1.0 + jnp.exp(-jnp.abs(fl)))
    tri = tri_ref[...]
    l1, l2, l3 = _split3(lf)
    c = _dot(tri, l1) + _dot(tri, l2) + _dot(tri, l3) + carry_ref[0:1, :]
    carry_ref[0:1, :] = c[tm - 1:tm, :]

    lane = lax.broadcasted_iota(jnp.int32, (tm, LANES), 1)
    ck = c * LOG2E
    cq = ck - shift_ref[...]
    pieces = [p.astype(F32) for p in _split3(cq) + _split3(ck)]
    e = jnp.where(lane == 6 * A_HEADS, 1.0, 0.0)
    for n in reversed(range(6)):
        piece = pieces[n] if n == 0 else pltpu.roll(pieces[n], n * A_HEADS, 1)
        e = jnp.where((lane >= n * A_HEADS) & (lane < (n + 1) * A_HEADS), piece, e)
    e = e.astype(BF16)
    v_extra = jnp.where(lane == X_BASE, 1.0, 0.0)
    low = lane < A_HEAD_DIM
    tkv = vt_ref.shape[-1]

    ones_bd = ones_ref[...]
    n_chunks = D_MODEL // MXU_DIM
    for ch in range(n_chunks):
        cs = slice(ch * MXU_DIM, (ch + 1) * MXU_DIM)
        qf = _dot(xn, wqkv_ref[:, cs])
        kf = _dot(xn, wqkv_ref[:, D_MODEL + ch * MXU_DIM:D_MODEL + (ch + 1) * MXU_DIM])
        vf = _dot(xn, wqkv_ref[:, 2 * D_MODEL + ch * MXU_DIM:2 * D_MODEL + (ch + 1) * MXU_DIM])
        qn = _seg_norm(qf, ones_bd, A_HEAD_DIM) * qg_ref[:, cs]
        kn = _seg_norm(kf, ones_bd, A_HEAD_DIM) * kg_ref[:, cs]
        heads_per_chunk = MXU_DIM // A_HEAD_DIM
        hs = slice(ch * heads_per_chunk * LANES, (ch + 1) * heads_per_chunk * LANES)
        exq = _dot(e, selq_ref[:, hs])
        exk = _dot(e, selk_ref[:, hs])
        for j in range(heads_per_chunk):
            pair = slice((j // 2) * LANES, (j // 2 + 1) * LANES)
            blk = slice(j * LANES, (j + 1) * LANES)
            out = slice((ch * heads_per_chunk + j) * LANES, (ch * heads_per_chunk + j + 1) * LANES)
            qp, kp, vp = qn[:, pair], kn[:, pair], vf[:, pair]
            if j % 2 == 1:
                qp = pltpu.roll(qp, A_HEAD_DIM, 1)
                kp = pltpu.roll(kp, A_HEAD_DIM, 1)
                vp = pltpu.roll(vp, A_HEAD_DIM, 1)
            q_ref[:, out] = jnp.where(low, qp, exq[:, blk]).astype(BF16)
            k_ref[:, out] = jnp.where(low, kp, exk[:, blk]).astype(BF16)
            v_aug = jnp.where(low, vp, v_extra)
            for cc in range(tm // tkv):
                vt_ref[cc, ch * heads_per_chunk + j] = (
                    v_aug[cc * tkv:(cc + 1) * tkv, :].T.astype(BF16))


def _a_sel_matrices():
    selq = np.zeros((LANES, A_HEADS * LANES), np.float32)
    selk = np.zeros((LANES, A_HEADS * LANES), np.float32)
    ones_lane = 6 * A_HEADS
    for h in range(A_HEADS):
        base = h * LANES + X_BASE
        for piece in range(3):
            selq[A_HEADS * piece + h, base + piece] = 1.0
            selq[ones_lane, base + 3 + piece] = 1.0
            selk[ones_lane, base + piece] = 1.0
            selk[A_HEADS * (3 + piece) + h, base + 3 + piece] = -1.0
    return jnp.asarray(selq, BF16), jnp.asarray(selk, BF16)


def _block_diag_ones(seg):
    idx = np.arange(MXU_DIM) // seg
    return jnp.asarray((idx[:, None] == idx[None, :]).astype(np.float32), BF16)


def _a_proj(h2d, seq, norm_gain, w_in, b_f, q_gain, k_gain, shift):
    t = h2d.shape[0]
    tm = 512
    wide = A_HEADS * LANES
    w_qkv = w_in[:, :3 * D_MODEL].astype(BF16)
    w_f = jnp.pad(w_in[:, 3 * D_MODEL:], ((0, 0), (0, LANES - A_HEADS))).astype(BF16)
    b_row = jnp.pad(b_f, (0, LANES - A_HEADS)).reshape(1, LANES)
    qg = (jnp.tile(q_gain, A_HEADS) * (A_HEAD_DIM ** -0.5 * LOG2E)).reshape(1, D_MODEL)
    kg = jnp.tile(k_gain, A_HEADS).reshape(1, D_MODEL)
    tri = jnp.asarray(np.tril(np.ones((tm, tm), np.float32)), BF16)
    selq, selk = _a_sel_matrices()
    shift_row = jnp.full((1, LANES), LOG2E, F32) * shift
    full = lambda shape: pl.BlockSpec(shape, lambda i: (0,) * len(shape))
    out_spec = pl.BlockSpec((tm, wide), lambda i: (i, 0))
    per_tile = tm // FOX_BLOCK
    return pl.pallas_call(
        functools.partial(_a_proj_kernel, seq // tm),
        grid=(t // tm,),
        in_specs=[pl.BlockSpec((tm, D_MODEL), lambda i: (i, 0)),
                  full((1, D_MODEL)), full((D_MODEL, 3 * D_MODEL)), full((D_MODEL, LANES)),
                  full((1, LANES)), full((1, D_MODEL)), full((1, D_MODEL)),
                  full((MXU_DIM, MXU_DIM)), full((tm, tm)), full((LANES, wide)), full((LANES, wide)),
                  full((1, LANES))],
        out_specs=[out_spec, out_spec,
                   pl.BlockSpec((per_tile, A_HEADS, LANES, FOX_BLOCK), lambda i: (i, 0, 0, 0))],
        out_shape=[jax.ShapeDtypeStruct((t, wide), BF16), jax.ShapeDtypeStruct((t, wide), BF16),
                   jax.ShapeDtypeStruct((t // FOX_BLOCK, A_HEADS, LANES, FOX_BLOCK), BF16)],
        scratch_shapes=[pltpu.VMEM((8, LANES), F32)],
        compiler_params=_params(("arbitrary",)),
        name="a_proj",
    )(h2d, norm_gain.reshape(1, D_MODEL), w_qkv, w_f, b_row, qg, kg,
      _block_diag_ones(A_HEAD_DIM), tri, selq, selk, shift_row)


def _fox_pair_output(accs_t, o_ref, rows):
    outs = [(a / a[X_BASE:X_BASE + 1, :]).T for a in accs_t]
    lane = lax.broadcasted_iota(jnp.int32, outs[0].shape, 1)
    o_ref[rows, :] = jnp.where(lane < A_HEAD_DIM, outs[0],
                               pltpu.roll(outs[1], A_HEAD_DIM, 1)).astype(o_ref.dtype)


def _fox_shifted_kernel(q_ref, k_ref, vt_ref, o_ref, acc_ref):
    blk = FOX_BLOCK
    row = lax.broadcasted_iota(jnp.int32, (blk, blk), 0)
    col = lax.broadcasted_iota(jnp.int32, (blk, blk), 1)
    key_visible = row <= col

    def q_tile(i, _):
        rows = pl.ds(pl.multiple_of(i * blk, blk), blk)
        qs = [q_ref[rows, hh * LANES:(hh + 1) * LANES] for hh in range(2)]

        def block(j, masked):
            keys = pl.ds(pl.multiple_of(j * blk, blk), blk)
            ss = [_dot_nt(k_ref[keys, hh * LANES:(hh + 1) * LANES], qs[hh]) for hh in range(2)]
            for hh in range(2):
                p = jnp.exp2(ss[hh])
                if masked:
                    p = jnp.where(key_visible, p, 0.0)
                acc_ref[hh] += _dot(vt_ref[j, hh], p.astype(BF16))

        acc_ref[...] = jnp.zeros_like(acc_ref)

        @pl.loop(0, i)
        def _(j):
            block(j, False)

        block(i, True)
        _fox_pair_output((acc_ref[0], acc_ref[1]), o_ref, rows)
        return 0

    lax.fori_loop(0, q_ref.shape[0] // blk, q_tile, 0)


def _fox_online_kernel(q_ref, k_ref, vt_ref, o_ref):
    blk = FOX_BLOCK
    qi = pl.program_id(2)
    qs = [q_ref[:, hh * LANES:(hh + 1) * LANES] for hh in range(2)]
    row = lax.broadcasted_iota(jnp.int32, (blk, blk), 0)
    col = lax.broadcasted_iota(jnp.int32, (blk, blk), 1)
    key_visible = row <= col

    def block(j, carry, masked):
        keys = pl.ds(pl.multiple_of(j * blk, blk), blk)
        new = []
        for hh in range(2):
            m, acc = carry[2 * hh], carry[2 * hh + 1]
            s = _dot_nt(k_ref[keys, hh * LANES:(hh + 1) * LANES], qs[hh])
            if masked:
                s = jnp.where(key_visible, s, NEG_INF)
            m_new = jnp.maximum(m, jnp.max(s, axis=0, keepdims=True))
            p = jnp.exp2(s - m_new)
            acc = jnp.exp2(m - m_new) * acc + _dot(vt_ref[j, hh], p.astype(BF16))
            new += [m_new, acc]
        return tuple(new)

    init = (jnp.full((1, blk), NEG_INF, F32), jnp.zeros((LANES, blk), F32)) * 2
    carry = lax.fori_loop(0, qi, functools.partial(block, masked=False), init)
    carry = block(qi, carry, True)
    _fox_pair_output((carry[1], carry[3]), o_ref, slice(None))


def _fox_attention(q, k, vt, batch, seq, shift):
    pairs = A_HEADS // 2
    blk = FOX_BLOCK
    q3, k3 = (a.reshape(batch, seq, A_HEADS * LANES) for a in (q, k))
    vt5 = vt.reshape(batch, seq // blk, A_HEADS, LANES, blk)
    out_shape = jax.ShapeDtypeStruct((batch, seq, D_MODEL), BF16)

    def shifted(q3, k3, vt5):
        pair = pl.BlockSpec((None, seq, 2 * LANES), lambda b, p: (b, 0, p))
        return pl.pallas_call(
            _fox_shifted_kernel,
            grid=(batch, pairs),
            in_specs=[pair, pair,
                      pl.BlockSpec((None, seq // blk, 2, LANES, blk), lambda b, p: (b, 0, p, 0, 0))],
            out_specs=pl.BlockSpec((None, seq, LANES), lambda b, p: (b, 0, p)),
            out_shape=out_shape,
            scratch_shapes=[pltpu.VMEM((2, LANES, blk), F32)],
            compiler_params=_params(("parallel", "parallel")),
            name="fox_shifted",
        )(q3, k3, vt5)

    def online(q3, k3, vt5):
        return pl.pallas_call(
            _fox_online_kernel,
            grid=(batch, pairs, seq // blk),
            in_specs=[pl.BlockSpec((None, blk, 2 * LANES), lambda b, p, i: (b, i, p)),
                      pl.BlockSpec((None, seq, 2 * LANES), lambda b, p, i: (b, 0, p)),
                      pl.BlockSpec((None, seq // blk, 2, LANES, blk), lambda b, p, i: (b, 0, p, 0, 0))],
            out_specs=pl.BlockSpec((None, blk, LANES), lambda b, p, i: (b, i, p)),
            out_shape=out_shape,
            compiler_params=_params(("parallel", "parallel", "arbitrary")),
            name="fox_online",
        )(q3, k3, vt5)

    return lax.cond(shift <= MAX_SAFE_SHIFT, shifted, online, q3, k3, vt5)


def _out_proj_kernel(h_ref, o_ref, w_ref, out_ref):
    out_ref[...] = h_ref[...] + _dot(o_ref[...], w_ref[...])


def _out_proj(h2d, o2d, w_out):
    t = h2d.shape[0]
    tm = 1024
    tile = pl.BlockSpec((tm, D_MODEL), lambda i: (i, 0))
    return pl.pallas_call(
        _out_proj_kernel,
        grid=(t // tm,),
        in_specs=[tile, tile, pl.BlockSpec((D_MODEL, D_MODEL), lambda i: (0, 0))],
        out_specs=tile,
        out_shape=jax.ShapeDtypeStruct((t, D_MODEL), F32),
        compiler_params=_params(("parallel",)),
        name="out_proj",
    )(h2d, o2d, w_out.astype(BF16))


def _b_proj_kernel(n_rope_cols, h_ref, g_ref, w_ref, gain_ref, ones_ref, cos_ref, sin_ref, out_ref):
    xn = _rms_rows(h_ref[...], g_ref[...]).astype(BF16)
    cos, sin = cos_ref[...], sin_ref[...]
    ones_bd = ones_ref[...]
    for ch in range(w_ref.shape[1] // MXU_DIM):
        cs = slice(ch * MXU_DIM, (ch + 1) * MXU_DIM)
        y = _dot(xn, w_ref[:, cs])
        if ch * MXU_DIM < n_rope_cols:
            y = _seg_norm(y, ones_bd, B_HEAD_DIM) * gain_ref[:, cs]
            for j in range(MXU_DIM // B_HEAD_DIM):
                blk = slice(j * B_HEAD_DIM, (j + 1) * B_HEAD_DIM)
                out_ref[:, ch * MXU_DIM + j * B_HEAD_DIM:ch * MXU_DIM + (j + 1) * B_HEAD_DIM] = (
                    _rope(y[:, blk], cos, sin).astype(out_ref.dtype))
        else:
            out_ref[:, cs] = y.astype(out_ref.dtype)


def _b_proj(h2d, norm_gain, w, head_gain_row, n_rope_cols, cos, sin):
    t = h2d.shape[0]
    n = w.shape[1]
    tm = 512
    full = lambda shape: pl.BlockSpec(shape, lambda i: (0,) * len(shape))
    return pl.pallas_call(
        functools.partial(_b_proj_kernel, n_rope_cols),
        grid=(t // tm,),
        in_specs=[pl.BlockSpec((tm, D_MODEL), lambda i: (i, 0)), full((1, D_MODEL)),
                  full((D_MODEL, n)), full((1, n)), full((MXU_DIM, MXU_DIM)),
                  pl.BlockSpec((tm, B_HEAD_DIM), lambda i: (i, 0)),
                  pl.BlockSpec((tm, B_HEAD_DIM), lambda i: (i, 0))],
        out_specs=pl.BlockSpec((tm, n), lambda i: (i, 0)),
        out_shape=jax.ShapeDtypeStruct((t, n), BF16),
        compiler_params=_params(("parallel",)),
        name="b_proj",
    )(h2d, norm_gain.reshape(1, D_MODEL), w.astype(BF16), head_gain_row,
      _block_diag_ones(B_HEAD_DIM), cos, sin)


def _dilated_kernel(q_ref, kc_ref, kp_ref, vc_ref, vp_ref, bp_ref, bc_ref, o_ref, lse_ref):
    tl = q_ref.shape[0]
    blk = BAND_BLOCK
    first_pen = jnp.where(pl.program_id(2) == 0, NEG_INF, 0.0)
    bias_prev, bias_cur = bp_ref[...], bc_ref[...]
    lane = lax.broadcasted_iota(jnp.int32, (blk, LANES), 1)
    for i in range(tl // blk):
        rows = slice(i * blk, (i + 1) * blk)
        lse_tile = jnp.zeros((blk, LANES), F32)
        for h in range(B_HEADS):
            hs = slice(h * B_HEAD_DIM, (h + 1) * B_HEAD_DIM)
            q = q_ref[rows, hs]
            if i == 0:
                k_prev, v_prev = kp_ref[:, hs], vp_ref[:, hs]
            else:
                prev_rows = slice((i - 1) * blk, i * blk)
                k_prev, v_prev = kc_ref[prev_rows, hs], vc_ref[prev_rows, hs]
            s_prev = _dot_nt(q, k_prev) + bias_prev
            if i == 0:
                s_prev = s_prev + first_pen
            s_cur = _dot_nt(q, kc_ref[rows, hs]) + bias_cur
            m = jnp.maximum(jnp.max(s_prev, axis=-1, keepdims=True),
                            jnp.max(s_cur, axis=-1, keepdims=True))
            p_prev = jnp.exp(s_prev - m)
            p_cur = jnp.exp(s_cur - m)
            l = jnp.sum(p_prev, axis=-1, keepdims=True) + jnp.sum(p_cur, axis=-1, keepdims=True)
            o = (_dot(p_prev.astype(BF16), v_prev) + _dot(p_cur.astype(BF16), vc_ref[rows, hs])) / l
            o_ref[rows, hs] = o.astype(o_ref.dtype)
            lse_tile = jnp.where(lane == h, m + jnp.log(l), lse_tile)
        lse_ref[rows, :] = lse_tile


def _band_biases():
    qi = np.arange(BAND_BLOCK)[:, None]
    kk = np.arange(BAND_BLOCK)[None, :]
    prev = np.where(kk >= qi, 0.0, -np.inf).astype(np.float32)
    cur = np.where(kk <= qi, 0.0, -np.inf).astype(np.float32)
    return jnp.asarray(prev), jnp.asarray(cur)


def _dilated_group(q_all, kv, group, batch, seq):
    d = B_DILATIONS[group]
    assert B_WINDOWS[group] // d == BAND_BLOCK and seq % (d * BAND_BLOCK) == 0
    length = seq // d
    tl = 256
    width = B_HEADS * B_HEAD_DIM
    sub = tl // BAND_BLOCK
    n_groups = len(B_DILATIONS)
    qv = q_all.reshape(batch, length, d * n_groups * width)
    kvv = kv.reshape(batch, length, d * 2 * width)

    def cur(part):
        return pl.BlockSpec((None, tl, width), lambda b, r, n: (b, n, r * 2 + part))

    def prev(part):
        return pl.BlockSpec((None, BAND_BLOCK, width),
                            lambda b, r, n: (b, jnp.maximum(n * sub - 1, 0), r * 2 + part))

    bias = pl.BlockSpec((BAND_BLOCK, BAND_BLOCK), lambda b, r, n: (0, 0))
    bias_prev, bias_cur = _band_biases()
    o, lse = pl.pallas_call(
        _dilated_kernel,
        grid=(batch, d, length // tl),
        in_specs=[pl.BlockSpec((None, tl, width), lambda b, r, n: (b, n, r * n_groups + group)),
                  cur(0), prev(0), cur(1), prev(1), bias, bias],
        out_specs=[pl.BlockSpec((None, tl, width), lambda b, r, n: (b, n, r)),
                   pl.BlockSpec((None, tl, LANES), lambda b, r, n: (b, n, r))],
        out_shape=[jax.ShapeDtypeStruct((batch, length, d * width), BF16),
                   jax.ShapeDtypeStruct((batch, length, d * LANES), F32)],
        compiler_params=_params(("parallel", "parallel", "arbitrary")),
        name=f"dilated_g{group}",
    )(qv, kvv, kvv, kvv, kvv, bias_prev, bias_cur)
    return o.reshape(batch * seq, width), lse.reshape(batch * seq, LANES)


def _b_out_kernel(h_ref, o0_ref, o1_ref, o2_ref, l0_ref, l1_ref, l2_ref, w_ref, out_ref, merged_ref):
    lses = [l0_ref[...], l1_ref[...], l2_ref[...]]
    top = jnp.maximum(jnp.maximum(lses[0], lses[1]), lses[2])
    es = [jnp.exp(x - top) for x in lses]
    den = es[0] + es[1] + es[2]
    ws = [x / den for x in es]
    o_refs = (o0_ref, o1_ref, o2_ref)
    for h in range(B_HEADS):
        hs = slice(h * B_HEAD_DIM, (h + 1) * B_HEAD_DIM)
        acc = ws[0][:, h:h + 1] * o_refs[0][:, hs].astype(F32)
        for g in (1, 2):
            acc = acc + ws[g][:, h:h + 1] * o_refs[g][:, hs].astype(F32)
        merged_ref[:, hs] = acc.astype(BF16)
    out_ref[...] = h_ref[...] + _dot(merged_ref[...], w_ref[...])


def _b_out(h2d, outs, lses, w_out):
    t = h2d.shape[0]
    tm = 512
    tile = pl.BlockSpec((tm, D_MODEL), lambda i: (i, 0))
    ltile = pl.BlockSpec((tm, LANES), lambda i: (i, 0))
    return pl.pallas_call(
        _b_out_kernel,
        grid=(t // tm,),
        in_specs=[tile, tile, tile, tile, ltile, ltile, ltile,
                  pl.BlockSpec((D_MODEL, D_MODEL), lambda i: (0, 0))],
        out_specs=tile,
        out_shape=jax.ShapeDtypeStruct((t, D_MODEL), F32),
        scratch_shapes=[pltpu.VMEM((tm, D_MODEL), BF16)],
        compiler_params=_params(("parallel",)),
        name="b_out",
    )(h2d, *outs, *lses, w_out.astype(BF16))


def _route(logits):
    tm = logits.shape[0]
    lane = lax.broadcasted_iota(jnp.int32, (tm, LANES), 1)
    lanef = lane.astype(F32)
    far = float(LANES)
    is_g = lane < MOE_GROUPS
    g_max = jnp.max(jnp.where(is_g, logits, NEG_INF), axis=-1, keepdims=True)
    g_sum = jnp.sum(jnp.where(is_g, jnp.exp(logits - g_max), 0.0), axis=-1, keepdims=True)
    g_idx = jnp.min(jnp.where(is_g, jnp.where(logits == g_max, lanef, far), far), axis=-1, keepdims=True)
    per_group = MOE_EXPERTS // MOE_GROUPS
    e_lo = ROUTE_E0 + g_idx * per_group
    in_group = jnp.where(lanef >= e_lo, jnp.where(lanef < e_lo + per_group, 1.0, 0.0), 0.0)
    cand1 = jnp.where(in_group > 0.0, logits, NEG_INF)
    v1 = jnp.max(cand1, axis=-1, keepdims=True)
    i1 = jnp.min(jnp.where(cand1 == v1, lanef, far), axis=-1, keepdims=True)
    cand2 = jnp.where(lanef == i1, NEG_INF, cand1)
    v2 = jnp.max(cand2, axis=-1, keepdims=True)
    i2 = jnp.min(jnp.where(cand2 == v2, lanef, far), axis=-1, keepdims=True)
    e21 = jnp.exp(v2 - v1)
    w1 = 1.0 / (1.0 + e21)
    w2 = e21 / (1.0 + e21)
    return (jnp.where(lanef == i1, w1, 0.0) + jnp.where(lanef == i2, w2, 0.0)) / g_sum


def _moe_kernel(h_ref, g_ref, wr1_ref, wr2_ref, wr3_ref, br_ref, wgu_ref, wd_ref, out_ref,
                xn_ref, comb_ref, acc_ref):
    e = pl.program_id(1)

    @pl.when(e == 0)
    def _():
        xn = _rms_rows(h_ref[...], g_ref[...])
        x1, x2, x3 = _split3(xn)
        w1, w2, w3 = wr1_ref[...], wr2_ref[...], wr3_ref[...]
        logits = (_dot(x1, w1) + (_dot(x1, w2) + _dot(x2, w1))
                  + (_dot(x2, w2) + _dot(x1, w3) + _dot(x3, w1))) + br_ref[...]
        comb_ref[...] = _route(logits)
        xn_ref[...] = x1
        acc_ref[...] = jnp.zeros_like(acc_ref)

    tm = h_ref.shape[0]
    lane = lax.broadcasted_iota(jnp.int32, (tm, LANES), 1)
    gate_w = jnp.sum(jnp.where(lane == e + ROUTE_E0, comb_ref[...], 0.0), axis=-1, keepdims=True)
    gu = _dot(xn_ref[...], wgu_ref[...])
    g, u = gu[:, :MOE_FF], gu[:, MOE_FF:]
    hid = (g / (1.0 + jnp.exp(-g))) * u * gate_w
    acc_ref[...] += _dot(hid.astype(BF16), wd_ref[...])

    @pl.when(e == MOE_EXPERTS - 1)
    def _():
        out_ref[...] = h_ref[...] + acc_ref[...]


def _moe(h2d, norm_gain, group_w, group_b, expert_w, expert_b, w_gate, w_up, w_down):
    t = h2d.shape[0]
    tm = 1024
    pad = LANES - MOE_GROUPS - MOE_EXPERTS
    w_r = jnp.pad(jnp.concatenate([group_w, expert_w], axis=1), ((0, 0), (0, pad)))
    r1 = w_r.astype(BF16)
    rem = w_r - r1.astype(F32)
    r2 = rem.astype(BF16)
    r3 = (rem - r2.astype(F32)).astype(BF16)
    b_r = jnp.pad(jnp.concatenate([group_b, expert_b]), (0, pad)).reshape(1, LANES)
    w_gu = jnp.concatenate([w_gate, w_up], axis=2).astype(BF16)
    w_d = w_down.astype(BF16)
    tile = pl.BlockSpec((tm, D_MODEL), lambda i, e: (i, 0))
    full = lambda shape: pl.BlockSpec(shape, lambda i, e: (0,) * len(shape))
    return pl.pallas_call(
        _moe_kernel,
        grid=(t // tm, MOE_EXPERTS),
        in_specs=[tile, full((1, D_MODEL)), full((D_MODEL, LANES)), full((D_MODEL, LANES)),
                  full((D_MODEL, LANES)), full((1, LANES)),
                  pl.BlockSpec((None, D_MODEL, 2 * MOE_FF), lambda i, e: (e, 0, 0)),
                  pl.BlockSpec((None, MOE_FF, D_MODEL), lambda i, e: (e, 0, 0))],
        out_specs=tile,
        out_shape=jax.ShapeDtypeStruct((t, D_MODEL), F32),
        scratch_shapes=[pltpu.VMEM((tm, D_MODEL), BF16), pltpu.VMEM((tm, LANES), F32),
                        pltpu.VMEM((tm, D_MODEL), F32)],
        compiler_params=_params(("parallel", "arbitrary")),
        name="moe",
    )(h2d, norm_gain.reshape(1, D_MODEL), r1, r2, r3, b_r, w_gu, w_d)


def kernel(x, positions, a_norm, a_w_in, a_b_f, a_q_gain, a_k_gain, a_w_out, kv_norm, kv_w, kv_k_gain, b_norm, b_w_q, b_q_gain, b_w_out, ffn_norm, moe_group_w, moe_group_b, moe_expert_w, moe_expert_b, moe_w_gate, moe_w_up, moe_w_down):
    batch, seq, _ = x.shape
    n_a = a_norm.shape[0]
    n_b = b_norm.shape[0]
    h = x.reshape(batch * seq, D_MODEL)
    b_width = B_HEADS * B_HEAD_DIM
    kv_sh = cos = sin = None

    for layer in range(n_a + n_b):
        if layer < n_a:
            i = layer
            shift = (A_HEAD_DIM ** 0.5) * jnp.max(jnp.abs(a_q_gain[i])) * jnp.max(jnp.abs(a_k_gain[i]))
            q, k, vt = _a_proj(h, seq, a_norm[i], a_w_in[i], a_b_f[i], a_q_gain[i], a_k_gain[i], shift)
            o = _fox_attention(q, k, vt, batch, seq, shift)
            h = _out_proj(h, o.reshape(batch * seq, D_MODEL), a_w_out[i])
        else:
            j = layer - n_a
            if j == 0:
                cos, sin = _rope_tables(positions)
                k_gain_row = jnp.concatenate([jnp.tile(kv_k_gain, B_HEADS), jnp.ones((b_width,), F32)])
                kv_sh = _b_proj(h, kv_norm, kv_w, k_gain_row.reshape(1, 2 * b_width), b_width, cos, sin)
            q_gain_row = (jnp.tile(b_q_gain[j], (1, B_HEADS)) * (B_HEAD_DIM ** -0.5)).reshape(1, 3 * b_width)
            q_all = _b_proj(h, b_norm[j], b_w_q[j], q_gain_row, 3 * b_width, cos, sin)
            outs, lses = [], []
            for g in range(len(B_DILATIONS)):
                o_g, lse_g = _dilated_group(q_all, kv_sh, g, batch, seq)
                outs.append(o_g)
                lses.append(lse_g)
            h = _b_out(h, outs, lses, b_w_out[j])
        h = _moe(h, ffn_norm[layer], moe_group_w[layer], moe_group_b[layer], moe_expert_w[layer],
                 moe_expert_b[layer], moe_w_gate[layer], moe_w_up[layer], moe_w_down[layer])
    return h.reshape(batch, seq, D_MODEL)
```

```python
import functools
import math

import numpy as np
import jax
import jax.numpy as jnp
from jax import lax
from jax.experimental import pallas as pl
from jax.experimental.pallas import tpu as pltpu

F32 = jnp.float32
BF16 = jnp.bfloat16

D_MODEL = 1024
EPS = 1e-6
ROPE_THETA = 10000.0
A_HEADS = 16
A_HEAD_DIM = 64
B_HEADS = 8
B_HEAD_DIM = 128
B_WINDOWS = (128, 512, 2048)
B_DILATIONS = (1, 4, 16)
BAND_BLOCK = 128
MOE_GROUPS = 4
MOE_EXPERTS = 16
MOE_FF = 256

LANES = 128
MXU_DIM = 256
VMEM_LIMIT = 56 * 1024 * 1024

NEG_INF = float("-inf")
LOG2E = math.log2(math.e)
FOX_BLOCK = 512
MAX_SAFE_SHIFT = 40.0

X_BASE = A_HEAD_DIM
ROUTE_E0 = MOE_GROUPS


def _params(sem):
    return pltpu.CompilerParams(dimension_semantics=sem, vmem_limit_bytes=VMEM_LIMIT)


def _rms_rows(x, gain_row):
    ms = jnp.mean(x * x, axis=-1, keepdims=True)
    return x * lax.rsqrt(ms + EPS) * gain_row


def _dot(a, b):
    return jnp.dot(a, b, preferred_element_type=F32)


def _dot_nt(a, b):
    return lax.dot_general(a, b, (((1,), (1,)), ((), ())), preferred_element_type=F32)


def _split3(x):
    p1 = x.astype(BF16)
    r1 = x - p1.astype(F32)
    p2 = r1.astype(BF16)
    p3 = (r1 - p2.astype(F32)).astype(BF16)
    return p1, p2, p3


def _seg_norm(x, ones_bd, seg):
    ss = _dot((x * x).astype(BF16), ones_bd)
    return x * lax.rsqrt(ss * (1.0 / seg) + EPS)


def _rope_table_kernel(pos_ref, inv_ref, sign_ref, cos_ref, sin_ref):
    ang = pos_ref[...].astype(F32) * inv_ref[...]
    cos_ref[...] = jnp.cos(ang)
    sin_ref[...] = jnp.sin(ang) * sign_ref[...]


def _rope_tables(positions):
    t = positions.size
    tm = 1024
    half = B_HEAD_DIM // 2
    inv = ROPE_THETA ** (-jnp.arange(half, dtype=F32) / half)
    inv = jnp.concatenate([inv, inv]).reshape(1, B_HEAD_DIM)
    sign = jnp.concatenate([-jnp.ones((half,), F32), jnp.ones((half,), F32)]).reshape(1, B_HEAD_DIM)
    row = pl.BlockSpec((1, B_HEAD_DIM), lambda i: (0, 0))
    tab = pl.BlockSpec((tm, B_HEAD_DIM), lambda i: (i, 0))
    return pl.pallas_call(
        _rope_table_kernel,
        grid=(t // tm,),
        in_specs=[pl.BlockSpec((tm, 1), lambda i: (i, 0)), row, row],
        out_specs=[tab, tab],
        out_shape=[jax.ShapeDtypeStruct((t, B_HEAD_DIM), F32)] * 2,
        compiler_params=_params(("parallel",)),
        name="rope_tables",
    )(positions.reshape(t, 1), inv, sign)


def _rope(x, cos, sin_signed):
    return x * cos + pltpu.roll(x, B_HEAD_DIM // 2, 1) * sin_signed


def _a_proj_kernel(tiles_per_seq, h_ref, g_ref, wqkv_ref, wf_ref, bf_ref, qg_ref, kg_ref,
                   ones_ref, tri_ref, selq_ref, selk_ref, shift_ref, q_ref, k_ref, vt_ref, carry_ref):
    tm = h_ref.shape[0]

    @pl.when(pl.program_id(0) % tiles_per_seq == 0)
    def _():
        carry_ref[...] = jnp.zeros_like(carry_ref)

    xn = _rms_rows(h_ref[...], g_ref[...]).astype(BF16)

    fl = _dot(xn, wf_ref[...]) + bf_ref[...]
    lf = jnp.minimum(fl, 0.0) - jnp.log(1.0 + jnp.exp(-jnp.abs(fl)))
    tri = tri_ref[...]
    l1, l2, l3 = _split3(lf)
    c = _dot(tri, l1) + _dot(tri, l2) + _dot(tri, l3) + carry_ref[0:1, :]
    carry_ref[0:1, :] = c[tm - 1:tm, :]

    lane = lax.broadcasted_iota(jnp.int32, (tm, LANES), 1)
    ck = c * LOG2E
    cq = ck - shift_ref[...]
    pieces = [p.astype(F32) for p in _split3(cq) + _split3(ck)]
    e = jnp.where(lane == 6 * A_HEADS, 1.0, 0.0)
    for n in reversed(range(6)):
        piece = pieces[n] if n == 0 else pltpu.roll(pieces[n], n * A_HEADS, 1)
        e = jnp.where((lane >= n * A_HEADS) & (lane < (n + 1) * A_HEADS), piece, e)
    e = e.astype(BF16)
    v_extra = jnp.where(lane == X_BASE, 1.0, 0.0)
    low = lane < A_HEAD_DIM
    tkv = vt_ref.shape[-1]

    ones_bd = ones_ref[...]
    n_chunks = D_MODEL // MXU_DIM
    for ch in range(n_chunks):
        cs = slice(ch * MXU_DIM, (ch + 1) * MXU_DIM)
        qf = _dot(xn, wqkv_ref[:, cs])
        kf = _dot(xn, wqkv_ref[:, D_MODEL + ch * MXU_DIM:D_MODEL + (ch + 1) * MXU_DIM])
        vf = _dot(xn, wqkv_ref[:, 2 * D_MODEL + ch * MXU_DIM:2 * D_MODEL + (ch + 1) * MXU_DIM])
        qn = _seg_norm(qf, ones_bd, A_HEAD_DIM) * qg_ref[:, cs]
        kn = _seg_norm(kf, ones_bd, A_HEAD_DIM) * kg_ref[:, cs]
        heads_per_chunk = MXU_DIM // A_HEAD_DIM
        hs = slice(ch * heads_per_chunk * LANES, (ch + 1) * heads_per_chunk * LANES)
        exq = _dot(e, selq_ref[:, hs])
        exk = _dot(e, selk_ref[:, hs])
        for j in range(heads_per_chunk):
            pair = slice((j // 2) * LANES, (j // 2 + 1) * LANES)
            blk = slice(j * LANES, (j + 1) * LANES)
            out = slice((ch * heads_per_chunk + j) * LANES, (ch * heads_per_chunk + j + 1) * LANES)
            qp, kp, vp = qn[:, pair], kn[:, pair], vf[:, pair]
            if j % 2 == 1:
                qp = pltpu.roll(qp, A_HEAD_DIM, 1)
                kp = pltpu.roll(kp, A_HEAD_DIM, 1)
                vp = pltpu.roll(vp, A_HEAD_DIM, 1)
            q_ref[:, out] = jnp.where(low, qp, exq[:, blk]).astype(BF16)
            k_ref[:, out] = jnp.where(low, kp, exk[:, blk]).astype(BF16)
            v_aug = jnp.where(low, vp, v_extra)
            for cc in range(tm // tkv):
                vt_ref[cc, ch * heads_per_chunk + j] = (
                    v_aug[cc * tkv:(cc + 1) * tkv, :].T.astype(BF16))


def _a_sel_matrices():
    selq = np.zeros((LANES, A_HEADS * LANES), np.float32)
    selk = np.zeros((LANES, A_HEADS * LANES), np.float32)
    ones_lane = 6 * A_HEADS
    for h in range(A_HEADS):
        base = h * LANES + X_BASE
        for piece in range(3):
            selq[A_HEADS * piece + h, base + piece] = 1.0
            selq[ones_lane, base + 3 + piece] = 1.0
            selk[ones_lane, base + piece] = 1.0
            selk[A_HEADS * (3 + piece) + h, base + 3 + piece] = -1.0
    return jnp.asarray(selq, BF16), jnp.asarray(selk, BF16)


def _block_diag_ones(seg):
    idx = np.arange(MXU_DIM) // seg
    return jnp.asarray((idx[:, None] == idx[None, :]).astype(np.float32), BF16)


def _a_proj(h2d, seq, norm_gain, w_in, b_f, q_gain, k_gain, shift):
    t = h2d.shape[0]
    tm = 512
    wide = A_HEADS * LANES
    w_qkv = w_in[:, :3 * D_MODEL].astype(BF16)
    w_f = jnp.pad(w_in[:, 3 * D_MODEL:], ((0, 0), (0, LANES - A_HEADS))).astype(BF16)
    b_row = jnp.pad(b_f, (0, LANES - A_HEADS)).reshape(1, LANES)
    qg = (jnp.tile(q_gain, A_HEADS) * (A_HEAD_DIM ** -0.5 * LOG2E)).reshape(1, D_MODEL)
    kg = jnp.tile(k_gain, A_HEADS).reshape(1, D_MODEL)
    tri = jnp.asarray(np.tril(np.ones((tm, tm), np.float32)), BF16)
    selq, selk = _a_sel_matrices()
    shift_row = jnp.full((1, LANES), LOG2E, F32) * shift
    full = lambda shape: pl.BlockSpec(shape, lambda i: (0,) * len(shape))
    out_spec = pl.BlockSpec((tm, wide), lambda i: (i, 0))
    per_tile = tm // FOX_BLOCK
    return pl.pallas_call(
        functools.partial(_a_proj_kernel, seq // tm),
        grid=(t // tm,),
        in_specs=[pl.BlockSpec((tm, D_MODEL), lambda i: (i, 0)),
                  full((1, D_MODEL)), full((D_MODEL, 3 * D_MODEL)), full((D_MODEL, LANES)),
                  full((1, LANES)), full((1, D_MODEL)), full((1, D_MODEL)),
                  full((MXU_DIM, MXU_DIM)), full((tm, tm)), full((LANES, wide)), full((LANES, wide)),
                  full((1, LANES))],
        out_specs=[out_spec, out_spec,
                   pl.BlockSpec((per_tile, A_HEADS, LANES, FOX_BLOCK), lambda i: (i, 0, 0, 0))],
        out_shape=[jax.ShapeDtypeStruct((t, wide), BF16), jax.ShapeDtypeStruct((t, wide), BF16),
                   jax.ShapeDtypeStruct((t // FOX_BLOCK, A_HEADS, LANES, FOX_BLOCK), BF16)],
        scratch_shapes=[pltpu.VMEM((8, LANES), F32)],
        compiler_params=_params(("arbitrary",)),
        name="a_proj",
    )(h2d, norm_gain.reshape(1, D_MODEL), w_qkv, w_f, b_row, qg, kg,
      _block_diag_ones(A_HEAD_DIM), tri, selq, selk, shift_row)


def _fox_pair_output(accs_t, o_ref, rows):
    outs = [(a / a[X_BASE:X_BASE + 1, :]).T for a in accs_t]
    lane = lax.broadcasted_iota(jnp.int32, outs[0].shape, 1)
    o_ref[rows, :] = jnp.where(lane < A_HEAD_DIM, outs[0],
                               pltpu.roll(outs[1], A_HEAD_DIM, 1)).astype(o_ref.dtype)


def _fox_shifted_kernel(q_ref, k_ref, vt_ref, o_ref, acc_ref):
    blk = FOX_BLOCK
    row = lax.broadcasted_iota(jnp.int32, (blk, blk), 0)
    col = lax.broadcasted_iota(jnp.int32, (blk, blk), 1)
    key_visible = row <= col

    def q_tile(i, _):
        rows = pl.ds(pl.multiple_of(i * blk, blk), blk)
        qs = [q_ref[rows, hh * LANES:(hh + 1) * LANES] for hh in range(2)]

        def block(j, masked):
            keys = pl.ds(pl.multiple_of(j * blk, blk), blk)
            ss = [_dot_nt(k_ref[keys, hh * LANES:(hh + 1) * LANES], qs[hh]) for hh in range(2)]
            for hh in range(2):
                p = jnp.exp2(ss[hh])
                if masked:
                    p = jnp.where(key_visible, p, 0.0)
                acc_ref[hh] += _dot(vt_ref[j, hh], p.astype(BF16))

        acc_ref[...] = jnp.zeros_like(acc_ref)

        @pl.loop(0, i)
        def _(j):
            block(j, False)

        block(i, True)
        _fox_pair_output((acc_ref[0], acc_ref[1]), o_ref, rows)
        return 0

    lax.fori_loop(0, q_ref.shape[0] // blk, q_tile, 0)


def _fox_online_kernel(q_ref, k_ref, vt_ref, o_ref):
    blk = FOX_BLOCK
    qi = pl.program_id(2)
    qs = [q_ref[:, hh * LANES:(hh + 1) * LANES] for hh in range(2)]
    row = lax.broadcasted_iota(jnp.int32, (blk, blk), 0)
    col = lax.broadcasted_iota(jnp.int32, (blk, blk), 1)
    key_visible = row <= col

    def block(j, carry, masked):
        keys = pl.ds(pl.multiple_of(j * blk, blk), blk)
        new = []
        for hh in range(2):
            m, acc = carry[2 * hh], carry[2 * hh + 1]
            s = _dot_nt(k_ref[keys, hh * LANES:(hh + 1) * LANES], qs[hh])
            if masked:
                s = jnp.where(key_visible, s, NEG_INF)
            m_new = jnp.maximum(m, jnp.max(s, axis=0, keepdims=True))
            p = jnp.exp2(s - m_new)
            acc = jnp.exp2(m - m_new) * acc + _dot(vt_ref[j, hh], p.astype(BF16))
            new += [m_new, acc]
        return tuple(new)

    init = (jnp.full((1, blk), NEG_INF, F32), jnp.zeros((LANES, blk), F32)) * 2
    carry = lax.fori_loop(0, qi, functools.partial(block, masked=False), init)
    carry = block(qi, carry, True)
    _fox_pair_output((carry[1], carry[3]), o_ref, slice(None))


def _fox_attention(q, k, vt, batch, seq, fixed_shift):
    pairs = A_HEADS // 2
    blk = FOX_BLOCK
    q3, k3 = (a.reshape(batch, seq, A_HEADS * LANES) for a in (q, k))
    vt5 = vt.reshape(batch, seq // blk, A_HEADS, LANES, blk)
    out_shape = jax.ShapeDtypeStruct((batch, seq, D_MODEL), BF16)

    def shifted(q3, k3, vt5):
        pair = pl.BlockSpec((None, seq, 2 * LANES), lambda b, p: (b, 0, p))
        return pl.pallas_call(
            _fox_shifted_kernel,
            grid=(batch, pairs),
            in_specs=[pair, pair,
                      pl.BlockSpec((None, seq // blk, 2, LANES, blk), lambda b, p: (b, 0, p, 0, 0))],
            out_specs=pl.BlockSpec((None, seq, LANES), lambda b, p: (b, 0, p)),
            out_shape=out_shape,
            scratch_shapes=[pltpu.VMEM((2, LANES, blk), F32)],
            compiler_params=_params(("parallel", "parallel")),
            name="fox_shifted",
        )(q3, k3, vt5)

    def online(q3, k3, vt5):
        return pl.pallas_call(
            _fox_online_kernel,
            grid=(batch, pairs, seq // blk),
            in_specs=[pl.BlockSpec((None, blk, 2 * LANES), lambda b, p, i: (b, i, p)),
                      pl.BlockSpec((None, seq, 2 * LANES), lambda b, p, i: (b, 0, p)),
                      pl.BlockSpec((None, seq // blk, 2, LANES, blk), lambda b, p, i: (b, 0, p, 0, 0))],
            out_specs=pl.BlockSpec((None, blk, LANES), lambda b, p, i: (b, i, p)),
            out_shape=out_shape,
            compiler_params=_params(("parallel", "parallel", "arbitrary")),
            name="fox_online",
        )(q3, k3, vt5)

    return shifted(q3, k3, vt5) if fixed_shift else online(q3, k3, vt5)


def _out_proj_kernel(h_ref, o_ref, w_ref, out_ref):
    out_ref[...] = h_ref[...] + _dot(o_ref[...], w_ref[...])


def _out_proj(h2d, o2d, w_out):
    t = h2d.shape[0]
    tm = 1024
    tile = pl.BlockSpec((tm, D_MODEL), lambda i: (i, 0))
    return pl.pallas_call(
        _out_proj_kernel,
        grid=(t // tm,),
        in_specs=[tile, tile, pl.BlockSpec((D_MODEL, D_MODEL), lambda i: (0, 0))],
        out_specs=tile,
        out_shape=jax.ShapeDtypeStruct((t, D_MODEL), F32),
        compiler_params=_params(("parallel",)),
        name="out_proj",
    )(h2d, o2d, w_out.astype(BF16))


def _b_proj_kernel(n_rope_cols, h_ref, g_ref, w_ref, gain_ref, ones_ref, cos_ref, sin_ref, out_ref):
    xn = _rms_rows(h_ref[...], g_ref[...]).astype(BF16)
    cos, sin = cos_ref[...], sin_ref[...]
    ones_bd = ones_ref[...]
    for ch in range(w_ref.shape[1] // MXU_DIM):
        cs = slice(ch * MXU_DIM, (ch + 1) * MXU_DIM)
        y = _dot(xn, w_ref[:, cs])
        if ch * MXU_DIM < n_rope_cols:
            y = _seg_norm(y, ones_bd, B_HEAD_DIM) * gain_ref[:, cs]
            for j in range(MXU_DIM // B_HEAD_DIM):
                blk = slice(j * B_HEAD_DIM, (j + 1) * B_HEAD_DIM)
                out_ref[:, ch * MXU_DIM + j * B_HEAD_DIM:ch * MXU_DIM + (j + 1) * B_HEAD_DIM] = (
                    _rope(y[:, blk], cos, sin).astype(out_ref.dtype))
        else:
            out_ref[:, cs] = y.astype(out_ref.dtype)


def _b_proj(h2d, norm_gain, w, head_gain_row, n_rope_cols, cos, sin):
    t = h2d.shape[0]
    n = w.shape[1]
    tm = 512
    full = lambda shape: pl.BlockSpec(shape, lambda i: (0,) * len(shape))
    return pl.pallas_call(
        functools.partial(_b_proj_kernel, n_rope_cols),
        grid=(t // tm,),
        in_specs=[pl.BlockSpec((tm, D_MODEL), lambda i: (i, 0)), full((1, D_MODEL)),
                  full((D_MODEL, n)), full((1, n)), full((MXU_DIM, MXU_DIM)),
                  pl.BlockSpec((tm, B_HEAD_DIM), lambda i: (i, 0)),
                  pl.BlockSpec((tm, B_HEAD_DIM), lambda i: (i, 0))],
        out_specs=pl.BlockSpec((tm, n), lambda i: (i, 0)),
        out_shape=jax.ShapeDtypeStruct((t, n), BF16),
        compiler_params=_params(("parallel",)),
        name="b_proj",
    )(h2d, norm_gain.reshape(1, D_MODEL), w.astype(BF16), head_gain_row,
      _block_diag_ones(B_HEAD_DIM), cos, sin)


def _dilated_kernel(q_ref, kc_ref, kp_ref, vc_ref, vp_ref, bp_ref, bc_ref, o_ref, lse_ref):
    tl = q_ref.shape[0]
    blk = BAND_BLOCK
    first_pen = jnp.where(pl.program_id(2) == 0, NEG_INF, 0.0)
    bias_prev, bias_cur = bp_ref[...], bc_ref[...]
    lane = lax.broadcasted_iota(jnp.int32, (blk, LANES), 1)
    for i in range(tl // blk):
        rows = slice(i * blk, (i + 1) * blk)
        lse_tile = jnp.zeros((blk, LANES), F32)
        for h in range(B_HEADS):
            hs = slice(h * B_HEAD_DIM, (h + 1) * B_HEAD_DIM)
            q = q_ref[rows, hs]
            if i == 0:
                k_prev, v_prev = kp_ref[:, hs], vp_ref[:, hs]
            else:
                prev_rows = slice((i - 1) * blk, i * blk)
                k_prev, v_prev = kc_ref[prev_rows, hs], vc_ref[prev_rows, hs]
            s_prev = _dot_nt(q, k_prev) + bias_prev
            if i == 0:
                s_prev = s_prev + first_pen
            s_cur = _dot_nt(q, kc_ref[rows, hs]) + bias_cur
            m = jnp.maximum(jnp.max(s_prev, axis=-1, keepdims=True),
                            jnp.max(s_cur, axis=-1, keepdims=True))
            p_prev = jnp.exp(s_prev - m)
            p_cur = jnp.exp(s_cur - m)
            l = jnp.sum(p_prev, axis=-1, keepdims=True) + jnp.sum(p_cur, axis=-1, keepdims=True)
            o = (_dot(p_prev.astype(BF16), v_prev) + _dot(p_cur.astype(BF16), vc_ref[rows, hs])) / l
            o_ref[rows, hs] = o.astype(o_ref.dtype)
            lse_tile = jnp.where(lane == h, m + jnp.log(l), lse_tile)
        lse_ref[rows, :] = lse_tile


def _band_biases():
    qi = np.arange(BAND_BLOCK)[:, None]
    kk = np.arange(BAND_BLOCK)[None, :]
    prev = np.where(kk >= qi, 0.0, -np.inf).astype(np.float32)
    cur = np.where(kk <= qi, 0.0, -np.inf).astype(np.float32)
    return jnp.asarray(prev), jnp.asarray(cur)


def _dilated_group(q_all, kv, group, batch, seq):
    d = B_DILATIONS[group]
    assert B_WINDOWS[group] // d == BAND_BLOCK and seq % (d * BAND_BLOCK) == 0
    length = seq // d
    tl = 256
    width = B_HEADS * B_HEAD_DIM
    sub = tl // BAND_BLOCK
    n_groups = len(B_DILATIONS)
    qv = q_all.reshape(batch, length, d * n_groups * width)
    kvv = kv.reshape(batch, length, d * 2 * width)

    def cur(part):
        return pl.BlockSpec((None, tl, width), lambda b, r, n: (b, n, r * 2 + part))

    def prev(part):
        return pl.BlockSpec((None, BAND_BLOCK, width),
                            lambda b, r, n: (b, jnp.maximum(n * sub - 1, 0), r * 2 + part))

    bias = pl.BlockSpec((BAND_BLOCK, BAND_BLOCK), lambda b, r, n: (0, 0))
    bias_prev, bias_cur = _band_biases()
    o, lse = pl.pallas_call(
        _dilated_kernel,
        grid=(batch, d, length // tl),
        in_specs=[pl.BlockSpec((None, tl, width), lambda b, r, n: (b, n, r * n_groups + group)),
                  cur(0), prev(0), cur(1), prev(1), bias, bias],
        out_specs=[pl.BlockSpec((None, tl, width), lambda b, r, n: (b, n, r)),
                   pl.BlockSpec((None, tl, LANES), lambda b, r, n: (b, n, r))],
        out_shape=[jax.ShapeDtypeStruct((batch, length, d * width), BF16),
                   jax.ShapeDtypeStruct((batch, length, d * LANES), F32)],
        compiler_params=_params(("parallel", "parallel", "arbitrary")),
        name=f"dilated_g{group}",
    )(qv, kvv, kvv, kvv, kvv, bias_prev, bias_cur)
    return o.reshape(batch * seq, width), lse.reshape(batch * seq, LANES)


def _b_out_kernel(h_ref, o0_ref, o1_ref, o2_ref, l0_ref, l1_ref, l2_ref, w_ref, out_ref, merged_ref):
    lses = [l0_ref[...], l1_ref[...], l2_ref[...]]
    top = jnp.maximum(jnp.maximum(lses[0], lses[1]), lses[2])
    es = [jnp.exp(x - top) for x in lses]
    den = es[0] + es[1] + es[2]
    ws = [x / den for x in es]
    o_refs = (o0_ref, o1_ref, o2_ref)
    for h in range(B_HEADS):
        hs = slice(h * B_HEAD_DIM, (h + 1) * B_HEAD_DIM)
        acc = ws[0][:, h:h + 1] * o_refs[0][:, hs].astype(F32)
        for g in (1, 2):
            acc = acc + ws[g][:, h:h + 1] * o_refs[g][:, hs].astype(F32)
        merged_ref[:, hs] = acc.astype(BF16)
    out_ref[...] = h_ref[...] + _dot(merged_ref[...], w_ref[...])


def _b_out(h2d, outs, lses, w_out):
    t = h2d.shape[0]
    tm = 512
    tile = pl.BlockSpec((tm, D_MODEL), lambda i: (i, 0))
    ltile = pl.BlockSpec((tm, LANES), lambda i: (i, 0))
    return pl.pallas_call(
        _b_out_kernel,
        grid=(t // tm,),
        in_specs=[tile, tile, tile, tile, ltile, ltile, ltile,
                  pl.BlockSpec((D_MODEL, D_MODEL), lambda i: (0, 0))],
        out_specs=tile,
        out_shape=jax.ShapeDtypeStruct((t, D_MODEL), F32),
        scratch_shapes=[pltpu.VMEM((tm, D_MODEL), BF16)],
        compiler_params=_params(("parallel",)),
        name="b_out",
    )(h2d, *outs, *lses, w_out.astype(BF16))


def _b_proj_res_kernel(plan, n_out, h_ref, g_ref, w_ref, gain_ref, ones_ref, cos_ref, sin_ref, *rest):
    out_refs, stage_ref = rest[:n_out], rest[n_out]
    tm = h_ref.shape[0]
    xn = _rms_rows(h_ref[...], g_ref[...]).astype(BF16)
    cos, sin = cos_ref[...], sin_ref[...]
    ones_bd = ones_ref[...]
    for ch, (roped, dests) in enumerate(plan):
        cs = slice(ch * MXU_DIM, (ch + 1) * MXU_DIM)
        y = _dot(xn, w_ref[:, cs])
        if roped:
            y = _seg_norm(y, ones_bd, B_HEAD_DIM) * gain_ref[:, cs]
        for j in range(MXU_DIM // LANES):
            blk = slice(j * LANES, (j + 1) * LANES)
            stage_ref[j] = _rope(y[:, blk], cos, sin) if roped else y[:, blk]
        for oi, d, c0 in dests:
            for r in range(d):
                for j in range(MXU_DIM // LANES):
                    out_refs[oi][r, :, c0 + j * LANES:c0 + (j + 1) * LANES] = (
                        stage_ref[j, pl.ds(r, tm // d, stride=d), :].astype(BF16))


def _b_proj_res(h2d, batch, seq, norm_gain, w, head_gain_row, plan, outs, cos, sin):
    n = w.shape[1]
    tm = 512
    per_seq = seq // tm
    full = lambda shape: pl.BlockSpec(shape, lambda b, i: (0,) * len(shape))
    rows = lambda width: pl.BlockSpec((tm, width), lambda b, i: (b * per_seq + i, 0))
    return pl.pallas_call(
        functools.partial(_b_proj_res_kernel, plan, len(outs)),
        grid=(batch, per_seq),
        in_specs=[rows(D_MODEL), full((1, D_MODEL)), full((D_MODEL, n)), full((1, n)),
                  full((MXU_DIM, MXU_DIM)), rows(B_HEAD_DIM), rows(B_HEAD_DIM)],
        out_specs=[pl.BlockSpec((None, d, tm // d, width), lambda b, i: (b, 0, i, 0)) for d, width in outs],
        out_shape=[jax.ShapeDtypeStruct((batch, d, seq // d, width), BF16) for d, width in outs],
        scratch_shapes=[pltpu.VMEM((MXU_DIM // LANES, tm, LANES), F32)],
        compiler_params=_params(("parallel", "parallel")),
        name="b_proj_res",
    )(h2d, norm_gain.reshape(1, D_MODEL), w.astype(BF16), head_gain_row,
      _block_diag_ones(B_HEAD_DIM), cos, sin)


def _dilated_shifted_kernel(q_ref, kc_ref, kp_ref, vc_ref, vp_ref, bias_ref, bias0_ref,
                            o_ref, l_ref, kbuf, vbuf):
    tl = q_ref.shape[0]
    blk = BAND_BLOCK
    kbuf[0:blk, :] = kp_ref[...]
    kbuf[blk:, :] = kc_ref[...]
    vbuf[0:blk, :] = vp_ref[...]
    vbuf[blk:, :] = vc_ref[...]
    bias = bias_ref[...]
    bias_first = jnp.where(pl.program_id(2) == 0, bias0_ref[...], bias)
    lane = lax.broadcasted_iota(jnp.int32, (blk, LANES), 1)
    for i in range(tl // blk):
        rows = slice(i * blk, (i + 1) * blk)
        keys = slice(i * blk, (i + 2) * blk)
        l_tile = jnp.zeros((blk, LANES), F32)
        for h in range(B_HEADS):
            hs = slice(h * B_HEAD_DIM, (h + 1) * B_HEAD_DIM)
            s = _dot_nt(q_ref[rows, hs], kbuf[keys, hs]) + (bias_first if i == 0 else bias)
            p = jnp.exp2(s)
            o_ref[rows, hs] = _dot(p.astype(BF16), vbuf[keys, hs]).astype(o_ref.dtype)
            l_tile = jnp.where(lane == h, jnp.sum(p, axis=-1, keepdims=True), l_tile)
        l_ref[rows, :] = l_tile


def _band_shift_biases(shift_log2):
    qi = np.arange(BAND_BLOCK)[:, None]
    kk = np.arange(2 * BAND_BLOCK)[None, :]
    band = (kk >= qi) & (kk <= qi + BAND_BLOCK)
    band0 = band & (kk >= BAND_BLOCK)
    neg = jnp.full((BAND_BLOCK, 2 * BAND_BLOCK), NEG_INF, F32)
    return jnp.where(band, -shift_log2, neg), jnp.where(band0, -shift_log2, neg)


def _dilated_shifted(q_g, kv_d, d, batch, seq, biases):
    length = seq // d
    tl = 512
    width = B_HEADS * B_HEAD_DIM
    sub = tl // BAND_BLOCK

    def cur(part):
        return pl.BlockSpec((None, None, tl, width), lambda b, r, n: (b, r, n, part))

    def prev(part):
        return pl.BlockSpec((None, None, BAND_BLOCK, width),
                            lambda b, r, n: (b, r, jnp.maximum(n * sub - 1, 0), part))

    bias = pl.BlockSpec((BAND_BLOCK, 2 * BAND_BLOCK), lambda b, r, n: (0, 0))
    return pl.pallas_call(
        _dilated_shifted_kernel,
        grid=(batch, d, length // tl),
        in_specs=[cur(0), cur(0), prev(0), cur(1), prev(1), bias, bias],
        out_specs=[cur(0), pl.BlockSpec((None, None, tl, LANES), lambda b, r, n: (b, r, n, 0))],
        out_shape=[jax.ShapeDtypeStruct((batch, d, length, width), BF16),
                   jax.ShapeDtypeStruct((batch, d, length, LANES), F32)],
        scratch_shapes=[pltpu.VMEM((tl + BAND_BLOCK, width), BF16)] * 2,
        compiler_params=_params(("parallel", "parallel", "arbitrary")),
        name=f"dilated_shifted_d{d}",
    )(q_g, kv_d, kv_d, kv_d, kv_d, *biases)


def _b_out_sum_kernel(dils, h_ref, *rest):
    n = len(dils)
    a_refs, l_refs = rest[:n], rest[n:2 * n]
    w_ref, out_ref, acc_ref, den_ref, merged_ref = rest[2 * n:]
    tm = h_ref.shape[0]
    for g, d in enumerate(dils):
        for r in range(d):
            idx = pl.ds(r, tm // d, stride=d)
            planes = [(den_ref, (idx, slice(None)), l_refs[g][r])]
            planes += [(acc_ref, (h, idx, slice(None)),
                        a_refs[g][r, :, h * B_HEAD_DIM:(h + 1) * B_HEAD_DIM].astype(F32))
                       for h in range(B_HEADS)]
            for ref, at, val in planes:
                ref[at] = val if g == 0 else ref[at] + val
    den = den_ref[...]
    for h in range(B_HEADS):
        hs = slice(h * B_HEAD_DIM, (h + 1) * B_HEAD_DIM)
        merged_ref[:, hs] = (acc_ref[h] / den[:, h:h + 1]).astype(BF16)
    out_ref[...] = h_ref[...] + _dot(merged_ref[...], w_ref[...])


def _b_out_sum(h2d, batch, seq, accs, dens, w_out):
    tm = 512
    per_seq = seq // tm
    dils = B_DILATIONS
    tile = pl.BlockSpec((tm, D_MODEL), lambda b, i: (b * per_seq + i, 0))
    res = lambda d, width: pl.BlockSpec((None, d, tm // d, width), lambda b, i: (b, 0, i, 0))
    return pl.pallas_call(
        functools.partial(_b_out_sum_kernel, dils),
        grid=(batch, per_seq),
        in_specs=([tile] + [res(d, D_MODEL) for d in dils] + [res(d, LANES) for d in dils]
                  + [pl.BlockSpec((D_MODEL, D_MODEL), lambda b, i: (0, 0))]),
        out_specs=tile,
        out_shape=jax.ShapeDtypeStruct(h2d.shape, F32),
        scratch_shapes=[pltpu.VMEM((B_HEADS, tm, B_HEAD_DIM), F32), pltpu.VMEM((tm, LANES), F32),
                        pltpu.VMEM((tm, D_MODEL), BF16)],
        compiler_params=_params(("parallel", "parallel")),
        name="b_out_sum",
    )(h2d, *accs, *dens, w_out.astype(BF16))


def _route(logits):
    tm = logits.shape[0]
    lane = lax.broadcasted_iota(jnp.int32, (tm, LANES), 1)
    lanef = lane.astype(F32)
    far = float(LANES)
    is_g = lane < MOE_GROUPS
    g_max = jnp.max(jnp.where(is_g, logits, NEG_INF), axis=-1, keepdims=True)
    g_sum = jnp.sum(jnp.where(is_g, jnp.exp(logits - g_max), 0.0), axis=-1, keepdims=True)
    g_idx = jnp.min(jnp.where(is_g, jnp.where(logits == g_max, lanef, far), far), axis=-1, keepdims=True)
    per_group = MOE_EXPERTS // MOE_GROUPS
    e_lo = ROUTE_E0 + g_idx * per_group
    in_group = jnp.where(lanef >= e_lo, jnp.where(lanef < e_lo + per_group, 1.0, 0.0), 0.0)
    cand1 = jnp.where(in_group > 0.0, logits, NEG_INF)
    v1 = jnp.max(cand1, axis=-1, keepdims=True)
    i1 = jnp.min(jnp.where(cand1 == v1, lanef, far), axis=-1, keepdims=True)
    cand2 = jnp.where(lanef == i1, NEG_INF, cand1)
    v2 = jnp.max(cand2, axis=-1, keepdims=True)
    i2 = jnp.min(jnp.where(cand2 == v2, lanef, far), axis=-1, keepdims=True)
    e21 = jnp.exp(v2 - v1)
    w1 = 1.0 / (1.0 + e21)
    w2 = e21 / (1.0 + e21)
    return (jnp.where(lanef == i1, w1, 0.0) + jnp.where(lanef == i2, w2, 0.0)) / g_sum


def _moe_kernel(h_ref, g_ref, wr1_ref, wr2_ref, wr3_ref, br_ref, wgu_ref, wd_ref, out_ref,
                xn_ref, comb_ref, acc_ref):
    e = pl.program_id(1)

    @pl.when(e == 0)
    def _():
        xn = _rms_rows(h_ref[...], g_ref[...])
        x1, x2, x3 = _split3(xn)
        w1, w2, w3 = wr1_ref[...], wr2_ref[...], wr3_ref[...]
        logits = (_dot(x1, w1) + (_dot(x1, w2) + _dot(x2, w1))
                  + (_dot(x2, w2) + _dot(x1, w3) + _dot(x3, w1))) + br_ref[...]
        comb_ref[...] = _route(logits)
        xn_ref[...] = x1
        acc_ref[...] = jnp.zeros_like(acc_ref)

    tm = h_ref.shape[0]
    lane = lax.broadcasted_iota(jnp.int32, (tm, LANES), 1)
    gate_w = jnp.sum(jnp.where(lane == e + ROUTE_E0, comb_ref[...], 0.0), axis=-1, keepdims=True)
    gu = _dot(xn_ref[...], wgu_ref[...])
    g, u = gu[:, :MOE_FF], gu[:, MOE_FF:]
    hid = (g / (1.0 + jnp.exp(-g))) * u * gate_w
    acc_ref[...] += _dot(hid.astype(BF16), wd_ref[...])

    @pl.when(e == MOE_EXPERTS - 1)
    def _():
        out_ref[...] = h_ref[...] + acc_ref[...]


def _moe(h2d, norm_gain, group_w, group_b, expert_w, expert_b, w_gate, w_up, w_down):
    t = h2d.shape[0]
    tm = 1024
    pad = LANES - MOE_GROUPS - MOE_EXPERTS
    w_r = jnp.pad(jnp.concatenate([group_w, expert_w], axis=1), ((0, 0), (0, pad)))
    r1 = w_r.astype(BF16)
    rem = w_r - r1.astype(F32)
    r2 = rem.astype(BF16)
    r3 = (rem - r2.astype(F32)).astype(BF16)
    b_r = jnp.pad(jnp.concatenate([group_b, expert_b]), (0, pad)).reshape(1, LANES)
    w_gu = jnp.concatenate([w_gate, w_up], axis=2).astype(BF16)
    w_d = w_down.astype(BF16)
    tile = pl.BlockSpec((tm, D_MODEL), lambda i, e: (i, 0))
    full = lambda shape: pl.BlockSpec(shape, lambda i, e: (0,) * len(shape))
    return pl.pallas_call(
        _moe_kernel,
        grid=(t // tm, MOE_EXPERTS),
        in_specs=[tile, full((1, D_MODEL)), full((D_MODEL, LANES)), full((D_MODEL, LANES)),
                  full((D_MODEL, LANES)), full((1, LANES)),
                  pl.BlockSpec((None, D_MODEL, 2 * MOE_FF), lambda i, e: (e, 0, 0)),
                  pl.BlockSpec((None, MOE_FF, D_MODEL), lambda i, e: (e, 0, 0))],
        out_specs=tile,
        out_shape=jax.ShapeDtypeStruct((t, D_MODEL), F32),
        scratch_shapes=[pltpu.VMEM((tm, D_MODEL), BF16), pltpu.VMEM((tm, LANES), F32),
                        pltpu.VMEM((tm, D_MODEL), F32)],
        compiler_params=_params(("parallel", "arbitrary")),
        name="moe",
    )(h2d, norm_gain.reshape(1, D_MODEL), r1, r2, r3, b_r, w_gu, w_d)


def kernel(x, positions, a_norm, a_w_in, a_b_f, a_q_gain, a_k_gain, a_w_out, kv_norm, kv_w, kv_k_gain, b_norm, b_w_q, b_q_gain, b_w_out, ffn_norm, moe_group_w, moe_group_b, moe_expert_w, moe_expert_b, moe_w_gate, moe_w_up, moe_w_down):
    batch, seq, _ = x.shape
    n_a = a_norm.shape[0]
    n_b = b_norm.shape[0]
    b_width = B_HEADS * B_HEAD_DIM
    n_groups = len(B_DILATIONS)
    chunks = b_width // MXU_DIM

    a_shifts = [(A_HEAD_DIM ** 0.5) * jnp.max(jnp.abs(a_q_gain[i])) * jnp.max(jnp.abs(a_k_gain[i]))
                for i in range(n_a)]
    b_shifts = [(B_HEAD_DIM ** 0.5) * jnp.max(jnp.abs(b_q_gain[j])) * jnp.max(jnp.abs(kv_k_gain))
                for j in range(n_b)]

    def trunk(fixed_shift, x):
        h = x.reshape(batch * seq, D_MODEL)
        kv_sh = cos = sin = None
        for layer in range(n_a + n_b):
            if layer < n_a:
                i = layer
                q, k, vt = _a_proj(h, seq, a_norm[i], a_w_in[i], a_b_f[i], a_q_gain[i], a_k_gain[i],
                                   a_shifts[i])
                o = _fox_attention(q, k, vt, batch, seq, fixed_shift)
                h = _out_proj(h, o.reshape(batch * seq, D_MODEL), a_w_out[i])
            else:
                j = layer - n_a
                k_gain_row = jnp.concatenate(
                    [jnp.tile(kv_k_gain, B_HEADS), jnp.ones((b_width,), F32)]).reshape(1, 2 * b_width)
                q_gain_row = (jnp.tile(b_q_gain[j], (1, B_HEADS)) * (B_HEAD_DIM ** -0.5)).reshape(
                    1, n_groups * b_width)
                if j == 0:
                    cos, sin = _rope_tables(positions)
                if fixed_shift:
                    if j == 0:
                        kv_plan = tuple((ch < chunks, tuple((g, d, ch * MXU_DIM) for g, d in enumerate(B_DILATIONS)))
                                        for ch in range(2 * chunks))
                        kv_sh = _b_proj_res(h, batch, seq, kv_norm, kv_w, k_gain_row, kv_plan,
                                            [(d, 2 * b_width) for d in B_DILATIONS], cos, sin)
                    q_plan = tuple((True, ((ch // chunks, B_DILATIONS[ch // chunks], (ch % chunks) * MXU_DIM),))
                                   for ch in range(n_groups * chunks))
                    q_res = _b_proj_res(h, batch, seq, b_norm[j], b_w_q[j], q_gain_row * LOG2E, q_plan,
                                        [(d, b_width) for d in B_DILATIONS], cos, sin)
                    biases = _band_shift_biases(b_shifts[j] * LOG2E)
                    accs, dens = zip(*[_dilated_shifted(q_res[g], kv_sh[g], d, batch, seq, biases)
                                       for g, d in enumerate(B_DILATIONS)])
                    h = _b_out_sum(h, batch, seq, accs, dens, b_w_out[j])
                else:
                    if j == 0:
                        kv_sh = _b_proj(h, kv_norm, kv_w, k_gain_row, b_width, cos, sin)
                    q_all = _b_proj(h, b_norm[j], b_w_q[j], q_gain_row, n_groups * b_width, cos, sin)
                    outs, lses = zip(*[_dilated_group(q_all, kv_sh, g, batch, seq) for g in range(n_groups)])
                    h = _b_out(h, outs, lses, b_w_out[j])
            h = _moe(h, ffn_norm[layer], moe_group_w[layer], moe_group_b[layer], moe_expert_w[layer],
                     moe_expert_b[layer], moe_w_gate[layer], moe_w_up[layer], moe_w_down[layer])
        return h.reshape(batch, seq, D_MODEL)

    largest = functools.reduce(jnp.maximum, a_shifts + b_shifts)
    return lax.cond(largest <= MAX_SAFE_SHIFT, functools.partial(trunk, True),
                    functools.partial(trunk, False), x)
```

```python
import functools
import math

import numpy as np
import jax
import jax.numpy as jnp
from jax import lax
from jax.experimental import pallas as pl
from jax.experimental.pallas import tpu as pltpu

F32 = jnp.float32
BF16 = jnp.bfloat16

D_MODEL = 1024
EPS = 1e-6
ROPE_THETA = 10000.0
A_HEADS = 16
A_HEAD_DIM = 64
B_HEADS = 8
B_HEAD_DIM = 128
B_WINDOWS = (128, 512, 2048)
B_DILATIONS = (1, 4, 16)
BAND_BLOCK = 128
MOE_GROUPS = 4
MOE_EXPERTS = 16
MOE_FF = 256

LANES = 128
MXU_DIM = 256
VMEM_LIMIT = 56 * 1024 * 1024

NEG_INF = float("-inf")
LOG2E = math.log2(math.e)
FOX_BLOCK = 512
MAX_SAFE_SHIFT = 40.0

X_BASE = A_HEAD_DIM
ROUTE_E0 = MOE_GROUPS


def _params(sem):
    return pltpu.CompilerParams(dimension_semantics=sem, vmem_limit_bytes=VMEM_LIMIT)


def _rms_rows(x, gain_row):
    ms = jnp.mean(x * x, axis=-1, keepdims=True)
    return x * lax.rsqrt(ms + EPS) * gain_row


def _dot(a, b):
    return jnp.dot(a, b, preferred_element_type=F32)


def _dot_nt(a, b):
    return lax.dot_general(a, b, (((1,), (1,)), ((), ())), preferred_element_type=F32)


def _split3(x):
    p1 = x.astype(BF16)
    r1 = x - p1.astype(F32)
    p2 = r1.astype(BF16)
    p3 = (r1 - p2.astype(F32)).astype(BF16)
    return p1, p2, p3


def _seg_norm(x, ones_bd, seg):
    ss = _dot((x * x).astype(BF16), ones_bd)
    return x * lax.rsqrt(ss * (1.0 / seg) + EPS)


def _rope_table_kernel(pos_ref, inv_ref, sign_ref, cos_ref, sin_ref):
    ang = pos_ref[...].astype(F32) * inv_ref[...]
    cos_ref[...] = jnp.cos(ang)
    sin_ref[...] = jnp.sin(ang) * sign_ref[...]


def _rope_tables(positions):
    t = positions.size
    tm = 1024
    half = B_HEAD_DIM // 2
    inv = ROPE_THETA ** (-jnp.arange(half, dtype=F32) / half)
    inv = jnp.concatenate([inv, inv]).reshape(1, B_HEAD_DIM)
    sign = jnp.concatenate([-jnp.ones((half,), F32), jnp.ones((half,), F32)]).reshape(1, B_HEAD_DIM)
    row = pl.BlockSpec((1, B_HEAD_DIM), lambda i: (0, 0))
    tab = pl.BlockSpec((tm, B_HEAD_DIM), lambda i: (i, 0))
    return pl.pallas_call(
        _rope_table_kernel,
        grid=(t // tm,),
        in_specs=[pl.BlockSpec((tm, 1), lambda i: (i, 0)), row, row],
        out_specs=[tab, tab],
        out_shape=[jax.ShapeDtypeStruct((t, B_HEAD_DIM), F32)] * 2,
        compiler_params=_params(("parallel",)),
        name="rope_tables",
    )(positions.reshape(t, 1), inv, sign)


def _rope(x, cos, sin_signed):
    return x * cos + pltpu.roll(x, B_HEAD_DIM // 2, 1) * sin_signed


def _a_proj_kernel(tiles_per_seq, h_ref, g_ref, wqkv_ref, wf_ref, bf_ref, qg_ref, kg_ref,
                   ones_ref, tri_ref, selq_ref, selk_ref, shift_ref, q_ref, k_ref, vt_ref, carry_ref):
    tm = h_ref.shape[0]

    @pl.when(pl.program_id(0) % tiles_per_seq == 0)
    def _():
        carry_ref[...] = jnp.zeros_like(carry_ref)

    xn = _rms_rows(h_ref[...], g_ref[...]).astype(BF16)

    fl = _dot(xn, wf_ref[...]) + bf_ref[...]
    lf = jnp.minimum(fl, 0.0) - jnp.log(1.0 + jnp.exp(-jnp.abs(fl)))
    tri = tri_ref[...]
    l1, l2, l3 = _split3(lf)
    c = _dot(tri, l1) + _dot(tri, l2) + _dot(tri, l3) + carry_ref[0:1, :]
    carry_ref[0:1, :] = c[tm - 1:tm, :]

    lane = lax.broadcasted_iota(jnp.int32, (tm, LANES), 1)
    ck = c * LOG2E
    cq = ck - shift_ref[...]
    pieces = [p.astype(F32) for p in _split3(cq) + _split3(ck)]
    e = jnp.where(lane == 6 * A_HEADS, 1.0, 0.0)
    for n in reversed(range(6)):
        piece = pieces[n] if n == 0 else pltpu.roll(pieces[n], n * A_HEADS, 1)
        e = jnp.where((lane >= n * A_HEADS) & (lane < (n + 1) * A_HEADS), piece, e)
    e = e.astype(BF16)
    v_extra = jnp.where(lane == X_BASE, 1.0, 0.0)
    low = lane < A_HEAD_DIM
    tkv = vt_ref.shape[-1]

    ones_bd = ones_ref[...]
    n_chunks = D_MODEL // MXU_DIM
    for ch in range(n_chunks):
        cs = slice(ch * MXU_DIM, (ch + 1) * MXU_DIM)
        qf = _dot(xn, wqkv_ref[:, cs])
        kf = _dot(xn, wqkv_ref[:, D_MODEL + ch * MXU_DIM:D_MODEL + (ch + 1) * MXU_DIM])
        vf = _dot(xn, wqkv_ref[:, 2 * D_MODEL + ch * MXU_DIM:2 * D_MODEL + (ch + 1) * MXU_DIM])
        qn = _seg_norm(qf, ones_bd, A_HEAD_DIM) * qg_ref[:, cs]
        kn = _seg_norm(kf, ones_bd, A_HEAD_DIM) * kg_ref[:, cs]
        heads_per_chunk = MXU_DIM // A_HEAD_DIM
        hs = slice(ch * heads_per_chunk * LANES, (ch + 1) * heads_per_chunk * LANES)
        exq = _dot(e, selq_ref[:, hs])
        exk = _dot(e, selk_ref[:, hs])
        for j in range(heads_per_chunk):
            pair = slice((j // 2) * LANES, (j // 2 + 1) * LANES)
            blk = slice(j * LANES, (j + 1) * LANES)
            out = slice((ch * heads_per_chunk + j) * LANES, (ch * heads_per_chunk + j + 1) * LANES)
            qp, kp, vp = qn[:, pair], kn[:, pair], vf[:, pair]
            if j % 2 == 1:
                qp = pltpu.roll(qp, A_HEAD_DIM, 1)
                kp = pltpu.roll(kp, A_HEAD_DIM, 1)
                vp = pltpu.roll(vp, A_HEAD_DIM, 1)
            q_ref[:, out] = jnp.where(low, qp, exq[:, blk]).astype(BF16)
            k_ref[:, out] = jnp.where(low, kp, exk[:, blk]).astype(BF16)
            v_aug = jnp.where(low, vp, v_extra)
            for cc in range(tm // tkv):
                vt_ref[cc, ch * heads_per_chunk + j] = (
                    v_aug[cc * tkv:(cc + 1) * tkv, :].T.astype(BF16))


def _a_sel_matrices():
    selq = np.zeros((LANES, A_HEADS * LANES), np.float32)
    selk = np.zeros((LANES, A_HEADS * LANES), np.float32)
    ones_lane = 6 * A_HEADS
    for h in range(A_HEADS):
        base = h * LANES + X_BASE
        for piece in range(3):
            selq[A_HEADS * piece + h, base + piece] = 1.0
            selq[ones_lane, base + 3 + piece] = 1.0
            selk[ones_lane, base + piece] = 1.0
            selk[A_HEADS * (3 + piece) + h, base + 3 + piece] = -1.0
    return jnp.asarray(selq, BF16), jnp.asarray(selk, BF16)


def _block_diag_ones(seg):
    idx = np.arange(MXU_DIM) // seg
    return jnp.asarray((idx[:, None] == idx[None, :]).astype(np.float32), BF16)


def _a_proj(h2d, seq, norm_gain, w_in, b_f, q_gain, k_gain, shift):
    t = h2d.shape[0]
    tm = 512
    wide = A_HEADS * LANES
    w_qkv = w_in[:, :3 * D_MODEL].astype(BF16)
    w_f = jnp.pad(w_in[:, 3 * D_MODEL:], ((0, 0), (0, LANES - A_HEADS))).astype(BF16)
    b_row = jnp.pad(b_f, (0, LANES - A_HEADS)).reshape(1, LANES)
    qg = (jnp.tile(q_gain, A_HEADS) * (A_HEAD_DIM ** -0.5 * LOG2E)).reshape(1, D_MODEL)
    kg = jnp.tile(k_gain, A_HEADS).reshape(1, D_MODEL)
    tri = jnp.asarray(np.tril(np.ones((tm, tm), np.float32)), BF16)
    selq, selk = _a_sel_matrices()
    shift_row = jnp.full((1, LANES), LOG2E, F32) * shift
    full = lambda shape: pl.BlockSpec(shape, lambda i: (0,) * len(shape))
    out_spec = pl.BlockSpec((tm, wide), lambda i: (i, 0))
    per_tile = tm // FOX_BLOCK
    return pl.pallas_call(
        functools.partial(_a_proj_kernel, seq // tm),
        grid=(t // tm,),
        in_specs=[pl.BlockSpec((tm, D_MODEL), lambda i: (i, 0)),
                  full((1, D_MODEL)), full((D_MODEL, 3 * D_MODEL)), full((D_MODEL, LANES)),
                  full((1, LANES)), full((1, D_MODEL)), full((1, D_MODEL)),
                  full((MXU_DIM, MXU_DIM)), full((tm, tm)), full((LANES, wide)), full((LANES, wide)),
                  full((1, LANES))],
        out_specs=[out_spec, out_spec,
                   pl.BlockSpec((per_tile, A_HEADS, LANES, FOX_BLOCK), lambda i: (i, 0, 0, 0))],
        out_shape=[jax.ShapeDtypeStruct((t, wide), BF16), jax.ShapeDtypeStruct((t, wide), BF16),
                   jax.ShapeDtypeStruct((t // FOX_BLOCK, A_HEADS, LANES, FOX_BLOCK), BF16)],
        scratch_shapes=[pltpu.VMEM((8, LANES), F32)],
        compiler_params=_params(("arbitrary",)),
        name="a_proj",
    )(h2d, norm_gain.reshape(1, D_MODEL), w_qkv, w_f, b_row, qg, kg,
      _block_diag_ones(A_HEAD_DIM), tri, selq, selk, shift_row)


def _fox_pair_output(accs_t, o_ref, rows):
    outs = [(a / a[X_BASE:X_BASE + 1, :]).T for a in accs_t]
    lane = lax.broadcasted_iota(jnp.int32, outs[0].shape, 1)
    o_ref[rows, :] = jnp.where(lane < A_HEAD_DIM, outs[0],
                               pltpu.roll(outs[1], A_HEAD_DIM, 1)).astype(o_ref.dtype)


def _fox_shifted_kernel(q_ref, k_ref, vt_ref, o_ref, acc_ref):
    blk = FOX_BLOCK
    row = lax.broadcasted_iota(jnp.int32, (blk, blk), 0)
    col = lax.broadcasted_iota(jnp.int32, (blk, blk), 1)
    key_visible = row <= col

    def q_tile(i, _):
        rows = pl.ds(pl.multiple_of(i * blk, blk), blk)
        qs = [q_ref[rows, hh * LANES:(hh + 1) * LANES] for hh in range(2)]

        def block(j, masked):
            keys = pl.ds(pl.multiple_of(j * blk, blk), blk)
            ss = [_dot_nt(k_ref[keys, hh * LANES:(hh + 1) * LANES], qs[hh]) for hh in range(2)]
            for hh in range(2):
                p = jnp.exp2(ss[hh])
                if masked:
                    p = jnp.where(key_visible, p, 0.0)
                acc_ref[hh] += _dot(vt_ref[j, hh], p.astype(BF16))

        acc_ref[...] = jnp.zeros_like(acc_ref)

        @pl.loop(0, i // 2)
        def _(t):
            block(2 * t, False)
            block(2 * t + 1, False)

        @pl.when(i % 2 == 1)
        def _():
            block(i - 1, False)

        block(i, True)
        _fox_pair_output((acc_ref[0], acc_ref[1]), o_ref, rows)
        return 0

    lax.fori_loop(0, q_ref.shape[0] // blk, q_tile, 0)


def _fox_online_kernel(q_ref, k_ref, vt_ref, o_ref):
    blk = FOX_BLOCK
    qi = pl.program_id(2)
    qs = [q_ref[:, hh * LANES:(hh + 1) * LANES] for hh in range(2)]
    row = lax.broadcasted_iota(jnp.int32, (blk, blk), 0)
    col = lax.broadcasted_iota(jnp.int32, (blk, blk), 1)
    key_visible = row <= col

    def block(j, carry, masked):
        keys = pl.ds(pl.multiple_of(j * blk, blk), blk)
        new = []
        for hh in range(2):
            m, acc = carry[2 * hh], carry[2 * hh + 1]
            s = _dot_nt(k_ref[keys, hh * LANES:(hh + 1) * LANES], qs[hh])
            if masked:
                s = jnp.where(key_visible, s, NEG_INF)
            m_new = jnp.maximum(m, jnp.max(s, axis=0, keepdims=True))
            p = jnp.exp2(s - m_new)
            acc = jnp.exp2(m - m_new) * acc + _dot(vt_ref[j, hh], p.astype(BF16))
            new += [m_new, acc]
        return tuple(new)

    init = (jnp.full((1, blk), NEG_INF, F32), jnp.zeros((LANES, blk), F32)) * 2
    carry = lax.fori_loop(0, qi, functools.partial(block, masked=False), init)
    carry = block(qi, carry, True)
    _fox_pair_output((carry[1], carry[3]), o_ref, slice(None))


def _fox_attention(q, k, vt, batch, seq, fixed_shift):
    pairs = A_HEADS // 2
    blk = FOX_BLOCK
    q3, k3 = (a.reshape(batch, seq, A_HEADS * LANES) for a in (q, k))
    vt5 = vt.reshape(batch, seq // blk, A_HEADS, LANES, blk)
    out_shape = jax.ShapeDtypeStruct((batch, seq, D_MODEL), BF16)

    def shifted(q3, k3, vt5):
        pair = pl.BlockSpec((None, seq, 2 * LANES), lambda b, p: (b, 0, p))
        return pl.pallas_call(
            _fox_shifted_kernel,
            grid=(batch, pairs),
            in_specs=[pair, pair,
                      pl.BlockSpec((None, seq // blk, 2, LANES, blk), lambda b, p: (b, 0, p, 0, 0))],
            out_specs=pl.BlockSpec((None, seq, LANES), lambda b, p: (b, 0, p)),
            out_shape=out_shape,
            scratch_shapes=[pltpu.VMEM((2, LANES, blk), F32)],
            compiler_params=_params(("parallel", "parallel")),
            name="fox_shifted",
        )(q3, k3, vt5)

    def online(q3, k3, vt5):
        return pl.pallas_call(
            _fox_online_kernel,
            grid=(batch, pairs, seq // blk),
            in_specs=[pl.BlockSpec((None, blk, 2 * LANES), lambda b, p, i: (b, i, p)),
                      pl.BlockSpec((None, seq, 2 * LANES), lambda b, p, i: (b, 0, p)),
                      pl.BlockSpec((None, seq // blk, 2, LANES, blk), lambda b, p, i: (b, 0, p, 0, 0))],
            out_specs=pl.BlockSpec((None, blk, LANES), lambda b, p, i: (b, i, p)),
            out_shape=out_shape,
            compiler_params=_params(("parallel", "parallel", "arbitrary")),
            name="fox_online",
        )(q3, k3, vt5)

    return shifted(q3, k3, vt5) if fixed_shift else online(q3, k3, vt5)


def _out_proj_kernel(h_ref, o_ref, w_ref, out_ref):
    out_ref[...] = h_ref[...] + _dot(o_ref[...], w_ref[...])


def _out_proj(h2d, o2d, w_out):
    t = h2d.shape[0]
    tm = 1024
    tile = pl.BlockSpec((tm, D_MODEL), lambda i: (i, 0))
    return pl.pallas_call(
        _out_proj_kernel,
        grid=(t // tm,),
        in_specs=[tile, tile, pl.BlockSpec((D_MODEL, D_MODEL), lambda i: (0, 0))],
        out_specs=tile,
        out_shape=jax.ShapeDtypeStruct((t, D_MODEL), F32),
        compiler_params=_params(("parallel",)),
        name="out_proj",
    )(h2d, o2d, w_out.astype(BF16))


def _b_proj_kernel(n_rope_cols, h_ref, g_ref, w_ref, gain_ref, ones_ref, cos_ref, sin_ref, out_ref):
    xn = _rms_rows(h_ref[...], g_ref[...]).astype(BF16)
    cos, sin = cos_ref[...], sin_ref[...]
    ones_bd = ones_ref[...]
    for ch in range(w_ref.shape[1] // MXU_DIM):
        cs = slice(ch * MXU_DIM, (ch + 1) * MXU_DIM)
        y = _dot(xn, w_ref[:, cs])
        if ch * MXU_DIM < n_rope_cols:
            y = _seg_norm(y, ones_bd, B_HEAD_DIM) * gain_ref[:, cs]
            for j in range(MXU_DIM // B_HEAD_DIM):
                blk = slice(j * B_HEAD_DIM, (j + 1) * B_HEAD_DIM)
                out_ref[:, ch * MXU_DIM + j * B_HEAD_DIM:ch * MXU_DIM + (j + 1) * B_HEAD_DIM] = (
                    _rope(y[:, blk], cos, sin).astype(out_ref.dtype))
        else:
            out_ref[:, cs] = y.astype(out_ref.dtype)


def _b_proj(h2d, norm_gain, w, head_gain_row, n_rope_cols, cos, sin):
    t = h2d.shape[0]
    n = w.shape[1]
    tm = 512
    full = lambda shape: pl.BlockSpec(shape, lambda i: (0,) * len(shape))
    return pl.pallas_call(
        functools.partial(_b_proj_kernel, n_rope_cols),
        grid=(t // tm,),
        in_specs=[pl.BlockSpec((tm, D_MODEL), lambda i: (i, 0)), full((1, D_MODEL)),
                  full((D_MODEL, n)), full((1, n)), full((MXU_DIM, MXU_DIM)),
                  pl.BlockSpec((tm, B_HEAD_DIM), lambda i: (i, 0)),
                  pl.BlockSpec((tm, B_HEAD_DIM), lambda i: (i, 0))],
        out_specs=pl.BlockSpec((tm, n), lambda i: (i, 0)),
        out_shape=jax.ShapeDtypeStruct((t, n), BF16),
        compiler_params=_params(("parallel",)),
        name="b_proj",
    )(h2d, norm_gain.reshape(1, D_MODEL), w.astype(BF16), head_gain_row,
      _block_diag_ones(B_HEAD_DIM), cos, sin)


def _dilated_kernel(q_ref, kc_ref, kp_ref, vc_ref, vp_ref, bp_ref, bc_ref, o_ref, lse_ref):
    tl = q_ref.shape[0]
    blk = BAND_BLOCK
    first_pen = jnp.where(pl.program_id(2) == 0, NEG_INF, 0.0)
    bias_prev, bias_cur = bp_ref[...], bc_ref[...]
    lane = lax.broadcasted_iota(jnp.int32, (blk, LANES), 1)
    for i in range(tl // blk):
        rows = slice(i * blk, (i + 1) * blk)
        lse_tile = jnp.zeros((blk, LANES), F32)
        for h in range(B_HEADS):
            hs = slice(h * B_HEAD_DIM, (h + 1) * B_HEAD_DIM)
            q = q_ref[rows, hs]
            if i == 0:
                k_prev, v_prev = kp_ref[:, hs], vp_ref[:, hs]
            else:
                prev_rows = slice((i - 1) * blk, i * blk)
                k_prev, v_prev = kc_ref[prev_rows, hs], vc_ref[prev_rows, hs]
            s_prev = _dot_nt(q, k_prev) + bias_prev
            if i == 0:
                s_prev = s_prev + first_pen
            s_cur = _dot_nt(q, kc_ref[rows, hs]) + bias_cur
            m = jnp.maximum(jnp.max(s_prev, axis=-1, keepdims=True),
                            jnp.max(s_cur, axis=-1, keepdims=True))
            p_prev = jnp.exp(s_prev - m)
            p_cur = jnp.exp(s_cur - m)
            l = jnp.sum(p_prev, axis=-1, keepdims=True) + jnp.sum(p_cur, axis=-1, keepdims=True)
            o = (_dot(p_prev.astype(BF16), v_prev) + _dot(p_cur.astype(BF16), vc_ref[rows, hs])) / l
            o_ref[rows, hs] = o.astype(o_ref.dtype)
            lse_tile = jnp.where(lane == h, m + jnp.log(l), lse_tile)
        lse_ref[rows, :] = lse_tile


def _band_biases():
    qi = np.arange(BAND_BLOCK)[:, None]
    kk = np.arange(BAND_BLOCK)[None, :]
    prev = np.where(kk >= qi, 0.0, -np.inf).astype(np.float32)
    cur = np.where(kk <= qi, 0.0, -np.inf).astype(np.float32)
    return jnp.asarray(prev), jnp.asarray(cur)


def _dilated_group(q_all, kv, group, batch, seq):
    d = B_DILATIONS[group]
    assert B_WINDOWS[group] // d == BAND_BLOCK and seq % (d * BAND_BLOCK) == 0
    length = seq // d
    tl = 256
    width = B_HEADS * B_HEAD_DIM
    sub = tl // BAND_BLOCK
    n_groups = len(B_DILATIONS)
    qv = q_all.reshape(batch, length, d * n_groups * width)
    kvv = kv.reshape(batch, length, d * 2 * width)

    def cur(part):
        return pl.BlockSpec((None, tl, width), lambda b, r, n: (b, n, r * 2 + part))

    def prev(part):
        return pl.BlockSpec((None, BAND_BLOCK, width),
                            lambda b, r, n: (b, jnp.maximum(n * sub - 1, 0), r * 2 + part))

    bias = pl.BlockSpec((BAND_BLOCK, BAND_BLOCK), lambda b, r, n: (0, 0))
    bias_prev, bias_cur = _band_biases()
    o, lse = pl.pallas_call(
        _dilated_kernel,
        grid=(batch, d, length // tl),
        in_specs=[pl.BlockSpec((None, tl, width), lambda b, r, n: (b, n, r * n_groups + group)),
                  cur(0), prev(0), cur(1), prev(1), bias, bias],
        out_specs=[pl.BlockSpec((None, tl, width), lambda b, r, n: (b, n, r)),
                   pl.BlockSpec((None, tl, LANES), lambda b, r, n: (b, n, r))],
        out_shape=[jax.ShapeDtypeStruct((batch, length, d * width), BF16),
                   jax.ShapeDtypeStruct((batch, length, d * LANES), F32)],
        compiler_params=_params(("parallel", "parallel", "arbitrary")),
        name=f"dilated_g{group}",
    )(qv, kvv, kvv, kvv, kvv, bias_prev, bias_cur)
    return o.reshape(batch * seq, width), lse.reshape(batch * seq, LANES)


def _b_out_kernel(h_ref, o0_ref, o1_ref, o2_ref, l0_ref, l1_ref, l2_ref, w_ref, out_ref, merged_ref):
    lses = [l0_ref[...], l1_ref[...], l2_ref[...]]
    top = jnp.maximum(jnp.maximum(lses[0], lses[1]), lses[2])
    es = [jnp.exp(x - top) for x in lses]
    den = es[0] + es[1] + es[2]
    ws = [x / den for x in es]
    o_refs = (o0_ref, o1_ref, o2_ref)
    for h in range(B_HEADS):
        hs = slice(h * B_HEAD_DIM, (h + 1) * B_HEAD_DIM)
        acc = ws[0][:, h:h + 1] * o_refs[0][:, hs].astype(F32)
        for g in (1, 2):
            acc = acc + ws[g][:, h:h + 1] * o_refs[g][:, hs].astype(F32)
        merged_ref[:, hs] = acc.astype(BF16)
    out_ref[...] = h_ref[...] + _dot(merged_ref[...], w_ref[...])


def _b_out(h2d, outs, lses, w_out):
    t = h2d.shape[0]
    tm = 512
    tile = pl.BlockSpec((tm, D_MODEL), lambda i: (i, 0))
    ltile = pl.BlockSpec((tm, LANES), lambda i: (i, 0))
    return pl.pallas_call(
        _b_out_kernel,
        grid=(t // tm,),
        in_specs=[tile, tile, tile, tile, ltile, ltile, ltile,
                  pl.BlockSpec((D_MODEL, D_MODEL), lambda i: (0, 0))],
        out_specs=tile,
        out_shape=jax.ShapeDtypeStruct((t, D_MODEL), F32),
        scratch_shapes=[pltpu.VMEM((tm, D_MODEL), BF16)],
        compiler_params=_params(("parallel",)),
        name="b_out",
    )(h2d, *outs, *lses, w_out.astype(BF16))


def _b_proj_res_kernel(plan, n_out, h_ref, g_ref, w_ref, gain_ref, cos_ref, sin_ref, *rest):
    out_refs, stage_ref = rest[:n_out], rest[n_out]
    tm = h_ref.shape[0]
    xn = _rms_rows(h_ref[...], g_ref[...]).astype(BF16)
    cos, sin = cos_ref[...], sin_ref[...]
    for ch, (roped, dests) in enumerate(plan):
        y = _dot(xn, w_ref[:, ch * MXU_DIM:(ch + 1) * MXU_DIM])
        slot = ch % 2
        for j in range(MXU_DIM // LANES):
            blk = y[:, j * LANES:(j + 1) * LANES]
            if roped:
                cols = slice(ch * MXU_DIM + j * LANES, ch * MXU_DIM + (j + 1) * LANES)
                blk = _rope(_rms_rows(blk, gain_ref[:, cols]), cos, sin)
            stage_ref[slot, j] = blk
        for oi, d, c0 in dests:
            for r in range(d):
                for j in range(MXU_DIM // LANES):
                    out_refs[oi][r, :, c0 + j * LANES:c0 + (j + 1) * LANES] = (
                        stage_ref[slot, j, pl.ds(r, tm // d, stride=d), :].astype(BF16))


def _b_proj_res(h2d, batch, seq, norm_gain, w, head_gain_row, plan, outs, cos, sin):
    n = w.shape[1]
    tm = 512
    per_seq = seq // tm
    full = lambda shape: pl.BlockSpec(shape, lambda b, i: (0,) * len(shape))
    rows = lambda width: pl.BlockSpec((tm, width), lambda b, i: (b * per_seq + i, 0))
    return pl.pallas_call(
        functools.partial(_b_proj_res_kernel, plan, len(outs)),
        grid=(batch, per_seq),
        in_specs=[rows(D_MODEL), full((1, D_MODEL)), full((D_MODEL, n)), full((1, n)),
                  rows(B_HEAD_DIM), rows(B_HEAD_DIM)],
        out_specs=[pl.BlockSpec((None, d, tm // d, width), lambda b, i: (b, 0, i, 0)) for d, width in outs],
        out_shape=[jax.ShapeDtypeStruct((batch, d, seq // d, width), BF16) for d, width in outs],
        scratch_shapes=[pltpu.VMEM((2, MXU_DIM // LANES, tm, LANES), F32)],
        compiler_params=_params(("parallel", "parallel")),
        name="b_proj_res",
    )(h2d, norm_gain.reshape(1, D_MODEL), w.astype(BF16), head_gain_row, cos, sin)


def _dilated_shifted_kernel(q_ref, kc_ref, kp_ref, vc_ref, vp_ref, bias_ref, bias0_ref,
                            o_ref, l_ref, kbuf, vbuf):
    tl = q_ref.shape[0]
    blk = BAND_BLOCK
    kbuf[0:blk, :] = kp_ref[...]
    kbuf[blk:, :] = kc_ref[...]
    vbuf[0:blk, :] = vp_ref[...]
    vbuf[blk:, :] = vc_ref[...]
    bias = bias_ref[...]
    bias_first = jnp.where(pl.program_id(2) == 0, bias0_ref[...], bias)
    lane = lax.broadcasted_iota(jnp.int32, (blk, LANES), 1)
    for i in range(tl // blk):
        rows = slice(i * blk, (i + 1) * blk)
        keys = slice(i * blk, (i + 2) * blk)
        l_tile = jnp.zeros((blk, LANES), F32)
        for h in range(B_HEADS):
            hs = slice(h * B_HEAD_DIM, (h + 1) * B_HEAD_DIM)
            s = _dot_nt(q_ref[rows, hs], kbuf[keys, hs]) + (bias_first if i == 0 else bias)
            p = jnp.exp2(s)
            o_ref[rows, hs] = _dot(p.astype(BF16), vbuf[keys, hs]).astype(o_ref.dtype)
            l_tile = jnp.where(lane == h, jnp.sum(p, axis=-1, keepdims=True), l_tile)
        l_ref[rows, :] = l_tile


def _band_shift_biases(shift_log2):
    qi = np.arange(BAND_BLOCK)[:, None]
    kk = np.arange(2 * BAND_BLOCK)[None, :]
    band = (kk >= qi) & (kk <= qi + BAND_BLOCK)
    band0 = band & (kk >= BAND_BLOCK)
    neg = jnp.full((BAND_BLOCK, 2 * BAND_BLOCK), NEG_INF, F32)
    return jnp.where(band, -shift_log2, neg), jnp.where(band0, -shift_log2, neg)


def _dilated_shifted(q_g, kv_d, d, batch, seq, biases):
    length = seq // d
    tl = 512
    width = B_HEADS * B_HEAD_DIM
    sub = tl // BAND_BLOCK

    def cur(part):
        return pl.BlockSpec((None, None, tl, width), lambda b, r, n: (b, r, n, part))

    def prev(part):
        return pl.BlockSpec((None, None, BAND_BLOCK, width),
                            lambda b, r, n: (b, r, jnp.maximum(n * sub - 1, 0), part))

    bias = pl.BlockSpec((BAND_BLOCK, 2 * BAND_BLOCK), lambda b, r, n: (0, 0))
    return pl.pallas_call(
        _dilated_shifted_kernel,
        grid=(batch, d, length // tl),
        in_specs=[cur(0), cur(0), prev(0), cur(1), prev(1), bias, bias],
        out_specs=[cur(0), pl.BlockSpec((None, None, tl, LANES), lambda b, r, n: (b, r, n, 0))],
        out_shape=[jax.ShapeDtypeStruct((batch, d, length, width), BF16),
                   jax.ShapeDtypeStruct((batch, d, length, LANES), F32)],
        scratch_shapes=[pltpu.VMEM((tl + BAND_BLOCK, width), BF16)] * 2,
        compiler_params=_params(("parallel", "parallel", "arbitrary")),
        name=f"dilated_shifted_d{d}",
    )(q_g, kv_d, kv_d, kv_d, kv_d, *biases)


def _b_out_sum_kernel(dils, h_ref, *rest):
    n = len(dils)
    a_refs, l_refs = rest[:n], rest[n:2 * n]
    w_ref, out_ref, acc_ref, den_ref, merged_ref = rest[2 * n:]
    tm = h_ref.shape[0]
    for g, d in enumerate(dils):
        for r in range(d):
            idx = pl.ds(r, tm // d, stride=d)
            planes = [(den_ref, (idx, slice(None)), l_refs[g][r])]
            planes += [(acc_ref, (h, idx, slice(None)),
                        a_refs[g][r, :, h * B_HEAD_DIM:(h + 1) * B_HEAD_DIM].astype(F32))
                       for h in range(B_HEADS)]
            for ref, at, val in planes:
                ref[at] = val if g == 0 else ref[at] + val
    den = den_ref[...]
    for h in range(B_HEADS):
        hs = slice(h * B_HEAD_DIM, (h + 1) * B_HEAD_DIM)
        merged_ref[:, hs] = (acc_ref[h] / den[:, h:h + 1]).astype(BF16)
    out_ref[...] = h_ref[...] + _dot(merged_ref[...], w_ref[...])


def _b_out_sum(h2d, batch, seq, accs, dens, w_out):
    tm = 512
    per_seq = seq // tm
    dils = B_DILATIONS
    tile = pl.BlockSpec((tm, D_MODEL), lambda b, i: (b * per_seq + i, 0))
    res = lambda d, width: pl.BlockSpec((None, d, tm // d, width), lambda b, i: (b, 0, i, 0))
    return pl.pallas_call(
        functools.partial(_b_out_sum_kernel, dils),
        grid=(batch, per_seq),
        in_specs=([tile] + [res(d, D_MODEL) for d in dils] + [res(d, LANES) for d in dils]
                  + [pl.BlockSpec((D_MODEL, D_MODEL), lambda b, i: (0, 0))]),
        out_specs=tile,
        out_shape=jax.ShapeDtypeStruct(h2d.shape, F32),
        scratch_shapes=[pltpu.VMEM((B_HEADS, tm, B_HEAD_DIM), F32), pltpu.VMEM((tm, LANES), F32),
                        pltpu.VMEM((tm, D_MODEL), BF16)],
        compiler_params=_params(("parallel", "parallel")),
        name="b_out_sum",
    )(h2d, *accs, *dens, w_out.astype(BF16))


def _route(logits):
    tm = logits.shape[0]
    lane = lax.broadcasted_iota(jnp.int32, (tm, LANES), 1)
    lanef = lane.astype(F32)
    far = float(LANES)
    is_g = lane < MOE_GROUPS
    g_max = jnp.max(jnp.where(is_g, logits, NEG_INF), axis=-1, keepdims=True)
    g_sum = jnp.sum(jnp.where(is_g, jnp.exp(logits - g_max), 0.0), axis=-1, keepdims=True)
    g_idx = jnp.min(jnp.where(is_g, jnp.where(logits == g_max, lanef, far), far), axis=-1, keepdims=True)
    per_group = MOE_EXPERTS // MOE_GROUPS
    e_lo = ROUTE_E0 + g_idx * per_group
    in_group = jnp.where(lanef >= e_lo, jnp.where(lanef < e_lo + per_group, 1.0, 0.0), 0.0)
    cand1 = jnp.where(in_group > 0.0, logits, NEG_INF)
    v1 = jnp.max(cand1, axis=-1, keepdims=True)
    i1 = jnp.min(jnp.where(cand1 == v1, lanef, far), axis=-1, keepdims=True)
    cand2 = jnp.where(lanef == i1, NEG_INF, cand1)
    v2 = jnp.max(cand2, axis=-1, keepdims=True)
    i2 = jnp.min(jnp.where(cand2 == v2, lanef, far), axis=-1, keepdims=True)
    e21 = jnp.exp(v2 - v1)
    w1 = 1.0 / (1.0 + e21)
    w2 = e21 / (1.0 + e21)
    return (jnp.where(lanef == i1, w1, 0.0) + jnp.where(lanef == i2, w2, 0.0)) / g_sum


def _moe_kernel(h_ref, g_ref, wr12_ref, wr1_ref, br_ref, wgu_ref, wd_ref, out_ref,
                xn_ref, comb_ref, hid_ref):
    grp = pl.program_id(1)
    per_group = MOE_EXPERTS // MOE_GROUPS

    @pl.when(grp == 0)
    def _():
        xn = _rms_rows(h_ref[...], g_ref[...])
        x1 = xn.astype(BF16)
        x2 = (xn - x1.astype(F32)).astype(BF16)
        first = _dot(x1, wr12_ref[...])
        logits = first[:, :LANES] + (first[:, LANES:] + _dot(x2, wr1_ref[...])) + br_ref[...]
        comb_ref[...] = _route(logits)
        xn_ref[...] = x1

    tm = h_ref.shape[0]
    lane = lax.broadcasted_iota(jnp.int32, (tm, LANES), 1)
    comb = comb_ref[...]
    xn = xn_ref[...]
    for e in range(per_group):
        gate_w = jnp.sum(jnp.where(lane == grp * per_group + (ROUTE_E0 + e), comb, 0.0),
                         axis=-1, keepdims=True)
        gu = _dot(xn, wgu_ref[e])
        g, u = gu[:, :MOE_FF], gu[:, MOE_FF:]
        hid_ref[:, e * MOE_FF:(e + 1) * MOE_FF] = ((g / (1.0 + jnp.exp(-g))) * u * gate_w).astype(BF16)
    y = _dot(hid_ref[...], wd_ref[...])

    @pl.when(grp == 0)
    def _():
        out_ref[...] = h_ref[...] + y

    @pl.when(grp != 0)
    def _():
        out_ref[...] += y


def _moe_weights(group_w, group_b, expert_w, expert_b, w_gate, w_up, w_down):
    pad = LANES - MOE_GROUPS - MOE_EXPERTS
    w_r = jnp.pad(jnp.concatenate([group_w, expert_w], axis=-1), ((0, 0), (0, 0), (0, pad)))
    r1 = w_r.astype(BF16)
    r2 = (w_r - r1.astype(F32)).astype(BF16)
    b_r = jnp.pad(jnp.concatenate([group_b, expert_b], axis=-1), ((0, 0), (0, pad)))[:, None, :]
    layers = w_gate.shape[0]
    per_group = MOE_EXPERTS // MOE_GROUPS
    w_gu = jnp.concatenate([w_gate, w_up], axis=-1).astype(BF16).reshape(
        layers, MOE_GROUPS, per_group, D_MODEL, 2 * MOE_FF)
    w_d = w_down.astype(BF16).reshape(layers, MOE_GROUPS, per_group * MOE_FF, D_MODEL)
    return jnp.concatenate([r1, r2], axis=-1), r1, b_r, w_gu, w_d


def _moe(h2d, norm_gain, layer, weights):
    wr12, wr1, b_r, w_gu, w_d = weights
    t = h2d.shape[0]
    tm = 1024
    per_group = MOE_EXPERTS // MOE_GROUPS
    tile = pl.BlockSpec((tm, D_MODEL), lambda i, g: (i, 0))
    of_layer = lambda shape: pl.BlockSpec((None,) + shape, lambda i, g: (layer,) + (0,) * len(shape))
    return pl.pallas_call(
        _moe_kernel,
        grid=(t // tm, MOE_GROUPS),
        in_specs=[tile, pl.BlockSpec((1, D_MODEL), lambda i, g: (0, 0)),
                  of_layer((D_MODEL, 2 * LANES)), of_layer((D_MODEL, LANES)), of_layer((1, LANES)),
                  pl.BlockSpec((None, None, per_group, D_MODEL, 2 * MOE_FF), lambda i, g: (layer, g, 0, 0, 0)),
                  pl.BlockSpec((None, None, per_group * MOE_FF, D_MODEL), lambda i, g: (layer, g, 0, 0))],
        out_specs=tile,
        out_shape=jax.ShapeDtypeStruct((t, D_MODEL), F32),
        scratch_shapes=[pltpu.VMEM((tm, D_MODEL), BF16), pltpu.VMEM((tm, LANES), F32),
                        pltpu.VMEM((tm, per_group * MOE_FF), BF16)],
        compiler_params=_params(("parallel", "arbitrary")),
        name="moe",
    )(h2d, norm_gain.reshape(1, D_MODEL), wr12, wr1, b_r, w_gu, w_d)


def kernel(x, positions, a_norm, a_w_in, a_b_f, a_q_gain, a_k_gain, a_w_out, kv_norm, kv_w, kv_k_gain, b_norm, b_w_q, b_q_gain, b_w_out, ffn_norm, moe_group_w, moe_group_b, moe_expert_w, moe_expert_b, moe_w_gate, moe_w_up, moe_w_down):
    batch, seq, _ = x.shape
    n_a = a_norm.shape[0]
    n_b = b_norm.shape[0]
    b_width = B_HEADS * B_HEAD_DIM
    n_groups = len(B_DILATIONS)
    chunks = b_width // MXU_DIM

    a_shifts = [(A_HEAD_DIM ** 0.5) * jnp.max(jnp.abs(a_q_gain[i])) * jnp.max(jnp.abs(a_k_gain[i]))
                for i in range(n_a)]
    b_shifts = [(B_HEAD_DIM ** 0.5) * jnp.max(jnp.abs(b_q_gain[j])) * jnp.max(jnp.abs(kv_k_gain))
                for j in range(n_b)]

    moe_weights = _moe_weights(moe_group_w, moe_group_b, moe_expert_w, moe_expert_b,
                               moe_w_gate, moe_w_up, moe_w_down)

    def trunk(fixed_shift, x):
        h = x.reshape(batch * seq, D_MODEL)
        kv_sh = cos = sin = None
        for layer in range(n_a + n_b):
            if layer < n_a:
                i = layer
                q, k, vt = _a_proj(h, seq, a_norm[i], a_w_in[i], a_b_f[i], a_q_gain[i], a_k_gain[i],
                                   a_shifts[i])
                o = _fox_attention(q, k, vt, batch, seq, fixed_shift)
                h = _out_proj(h, o.reshape(batch * seq, D_MODEL), a_w_out[i])
            else:
                j = layer - n_a
                k_gain_row = jnp.concatenate(
                    [jnp.tile(kv_k_gain, B_HEADS), jnp.ones((b_width,), F32)]).reshape(1, 2 * b_width)
                q_gain_row = (jnp.tile(b_q_gain[j], (1, B_HEADS)) * (B_HEAD_DIM ** -0.5)).reshape(
                    1, n_groups * b_width)
                if j == 0:
                    cos, sin = _rope_tables(positions)
                if fixed_shift:
                    if j == 0:
                        kv_plan = tuple((ch < chunks, tuple((g, d, ch * MXU_DIM) for g, d in enumerate(B_DILATIONS)))
                                        for ch in range(2 * chunks))
                        kv_sh = _b_proj_res(h, batch, seq, kv_norm, kv_w, k_gain_row, kv_plan,
                                            [(d, 2 * b_width) for d in B_DILATIONS], cos, sin)
                    q_plan = tuple((True, ((ch // chunks, B_DILATIONS[ch // chunks], (ch % chunks) * MXU_DIM),))
                                   for ch in range(n_groups * chunks))
                    q_res = _b_proj_res(h, batch, seq, b_norm[j], b_w_q[j], q_gain_row * LOG2E, q_plan,
                                        [(d, b_width) for d in B_DILATIONS], cos, sin)
                    biases = _band_shift_biases(b_shifts[j] * LOG2E)
                    accs, dens = zip(*[_dilated_shifted(q_res[g], kv_sh[g], d, batch, seq, biases)
                                       for g, d in enumerate(B_DILATIONS)])
                    h = _b_out_sum(h, batch, seq, accs, dens, b_w_out[j])
                else:
                    if j == 0:
                        kv_sh = _b_proj(h, kv_norm, kv_w, k_gain_row, b_width, cos, sin)
                    q_all = _b_proj(h, b_norm[j], b_w_q[j], q_gain_row, n_groups * b_width, cos, sin)
                    outs, lses = zip(*[_dilated_group(q_all, kv_sh, g, batch, seq) for g in range(n_groups)])
                    h = _b_out(h, outs, lses, b_w_out[j])
            h = _moe(h, ffn_norm[layer], layer, moe_weights)
        return h.reshape(batch, seq, D_MODEL)

    largest = functools.reduce(jnp.maximum, a_shifts + b_shifts)
    return lax.cond(largest <= MAX_SAFE_SHIFT, functools.partial(trunk, True),
                    functools.partial(trunk, False), x)
```

```python
import functools
import math

import numpy as np
import jax
import jax.numpy as jnp
from jax import lax
from jax.experimental import pallas as pl
from jax.experimental.pallas import tpu as pltpu

F32 = jnp.float32
BF16 = jnp.bfloat16

D_MODEL = 1024
EPS = 1e-6
ROPE_THETA = 10000.0
A_HEADS = 16
A_HEAD_DIM = 64
B_HEADS = 8
B_HEAD_DIM = 128
B_WINDOWS = (128, 512, 2048)
B_DILATIONS = (1, 4, 16)
BAND_BLOCK = 128
MOE_GROUPS = 4
MOE_EXPERTS = 16
MOE_FF = 256
MOE_CHUNK = 256

LANES = 128
MXU_DIM = 256
VMEM_LIMIT = 56 * 1024 * 1024

NEG_INF = float("-inf")
LOG2E = math.log2(math.e)
FOX_BLOCK = 512
MAX_SAFE_SHIFT = 40.0

X_BASE = A_HEAD_DIM
ROUTE_E0 = MOE_GROUPS


def _params(sem):
    return pltpu.CompilerParams(dimension_semantics=sem, vmem_limit_bytes=VMEM_LIMIT)


def _rms_rows(x, gain_row):
    ms = jnp.mean(x * x, axis=-1, keepdims=True)
    return x * lax.rsqrt(ms + EPS) * gain_row


def _dot(a, b):
    return jnp.dot(a, b, preferred_element_type=F32)


def _dot_nt(a, b):
    return lax.dot_general(a, b, (((1,), (1,)), ((), ())), preferred_element_type=F32)


def _split3(x):
    p1 = x.astype(BF16)
    r1 = x - p1.astype(F32)
    p2 = r1.astype(BF16)
    p3 = (r1 - p2.astype(F32)).astype(BF16)
    return p1, p2, p3


def _seg_norm(x, ones_bd, seg):
    ss = _dot((x * x).astype(BF16), ones_bd)
    return x * lax.rsqrt(ss * (1.0 / seg) + EPS)


def _rope_table_kernel(pos_ref, inv_ref, sign_ref, cos_ref, sin_ref):
    ang = pos_ref[...].astype(F32) * inv_ref[...]
    cos_ref[...] = jnp.cos(ang)
    sin_ref[...] = jnp.sin(ang) * sign_ref[...]


def _rope_tables(positions):
    t = positions.size
    tm = 1024
    half = B_HEAD_DIM // 2
    inv = ROPE_THETA ** (-jnp.arange(half, dtype=F32) / half)
    inv = jnp.concatenate([inv, inv]).reshape(1, B_HEAD_DIM)
    sign = jnp.concatenate([-jnp.ones((half,), F32), jnp.ones((half,), F32)]).reshape(1, B_HEAD_DIM)
    row = pl.BlockSpec((1, B_HEAD_DIM), lambda i: (0, 0))
    tab = pl.BlockSpec((tm, B_HEAD_DIM), lambda i: (i, 0))
    return pl.pallas_call(
        _rope_table_kernel,
        grid=(t // tm,),
        in_specs=[pl.BlockSpec((tm, 1), lambda i: (i, 0)), row, row],
        out_specs=[tab, tab],
        out_shape=[jax.ShapeDtypeStruct((t, B_HEAD_DIM), F32)] * 2,
        compiler_params=_params(("parallel",)),
        name="rope_tables",
    )(positions.reshape(t, 1), inv, sign)


def _rope(x, cos, sin_signed):
    return x * cos + pltpu.roll(x, B_HEAD_DIM // 2, 1) * sin_signed


def _a_proj_kernel(tiles_per_seq, h_ref, g_ref, wqkv_ref, wf_ref, bf_ref, qg_ref, kg_ref,
                   ones_ref, tri_ref, selq_ref, selk_ref, shift_ref, q_ref, k_ref, vt_ref, carry_ref):
    tm = h_ref.shape[0]

    @pl.when(pl.program_id(0) % tiles_per_seq == 0)
    def _():
        carry_ref[...] = jnp.zeros_like(carry_ref)

    xn = _rms_rows(h_ref[...], g_ref[...]).astype(BF16)

    fl = _dot(xn, wf_ref[...]) + bf_ref[...]
    lf = jnp.minimum(fl, 0.0) - jnp.log(1.0 + jnp.exp(-jnp.abs(fl)))
    tri = tri_ref[...]
    l1, l2, l3 = _split3(lf)
    c = _dot(tri, l1) + _dot(tri, l2) + _dot(tri, l3) + carry_ref[0:1, :]
    carry_ref[0:1, :] = c[tm - 1:tm, :]

    lane = lax.broadcasted_iota(jnp.int32, (tm, LANES), 1)
    ck = c * LOG2E
    cq = ck - shift_ref[...]
    pieces = [p.astype(F32) for p in _split3(cq) + _split3(ck)]
    e = jnp.where(lane == 6 * A_HEADS, 1.0, 0.0)
    for n in reversed(range(6)):
        piece = pieces[n] if n == 0 else pltpu.roll(pieces[n], n * A_HEADS, 1)
        e = jnp.where((lane >= n * A_HEADS) & (lane < (n + 1) * A_HEADS), piece, e)
    e = e.astype(BF16)
    v_extra = jnp.where(lane == X_BASE, 1.0, 0.0)
    low = lane < A_HEAD_DIM
    tkv = vt_ref.shape[-1]

    ones_bd = ones_ref[...]
    n_chunks = D_MODEL // MXU_DIM
    for ch in range(n_chunks):
        cs = slice(ch * MXU_DIM, (ch + 1) * MXU_DIM)
        qf = _dot(xn, wqkv_ref[:, cs])
        kf = _dot(xn, wqkv_ref[:, D_MODEL + ch * MXU_DIM:D_MODEL + (ch + 1) * MXU_DIM])
        vf = _dot(xn, wqkv_ref[:, 2 * D_MODEL + ch * MXU_DIM:2 * D_MODEL + (ch + 1) * MXU_DIM])
        qn = _seg_norm(qf, ones_bd, A_HEAD_DIM) * qg_ref[:, cs]
        kn = _seg_norm(kf, ones_bd, A_HEAD_DIM) * kg_ref[:, cs]
        heads_per_chunk = MXU_DIM // A_HEAD_DIM
        hs = slice(ch * heads_per_chunk * LANES, (ch + 1) * heads_per_chunk * LANES)
        exq = _dot(e, selq_ref[:, hs])
        exk = _dot(e, selk_ref[:, hs])
        for j in range(heads_per_chunk):
            pair = slice((j // 2) * LANES, (j // 2 + 1) * LANES)
            blk = slice(j * LANES, (j + 1) * LANES)
            out = slice((ch * heads_per_chunk + j) * LANES, (ch * heads_per_chunk + j + 1) * LANES)
            qp, kp, vp = qn[:, pair], kn[:, pair], vf[:, pair]
            if j % 2 == 1:
                qp = pltpu.roll(qp, A_HEAD_DIM, 1)
                kp = pltpu.roll(kp, A_HEAD_DIM, 1)
                vp = pltpu.roll(vp, A_HEAD_DIM, 1)
            q_ref[:, out] = jnp.where(low, qp, exq[:, blk]).astype(BF16)
            k_ref[:, out] = jnp.where(low, kp, exk[:, blk]).astype(BF16)
            v_aug = jnp.where(low, vp, v_extra)
            for cc in range(tm // tkv):
                vt_ref[cc, ch * heads_per_chunk + j] = (
                    v_aug[cc * tkv:(cc + 1) * tkv, :].T.astype(BF16))


def _a_sel_matrices():
    selq = np.zeros((LANES, A_HEADS * LANES), np.float32)
    selk = np.zeros((LANES, A_HEADS * LANES), np.float32)
    ones_lane = 6 * A_HEADS
    for h in range(A_HEADS):
        base = h * LANES + X_BASE
        for piece in range(3):
            selq[A_HEADS * piece + h, base + piece] = 1.0
            selq[ones_lane, base + 3 + piece] = 1.0
            selk[ones_lane, base + piece] = 1.0
            selk[A_HEADS * (3 + piece) + h, base + 3 + piece] = -1.0
    return jnp.asarray(selq, BF16), jnp.asarray(selk, BF16)


def _block_diag_ones(seg):
    idx = np.arange(MXU_DIM) // seg
    return jnp.asarray((idx[:, None] == idx[None, :]).astype(np.float32), BF16)


def _a_proj(h2d, seq, norm_gain, w_in, b_f, q_gain, k_gain, shift):
    t = h2d.shape[0]
    tm = 512
    wide = A_HEADS * LANES
    w_qkv = w_in[:, :3 * D_MODEL].astype(BF16)
    w_f = jnp.pad(w_in[:, 3 * D_MODEL:], ((0, 0), (0, LANES - A_HEADS))).astype(BF16)
    b_row = jnp.pad(b_f, (0, LANES - A_HEADS)).reshape(1, LANES)
    qg = (jnp.tile(q_gain, A_HEADS) * (A_HEAD_DIM ** -0.5 * LOG2E)).reshape(1, D_MODEL)
    kg = jnp.tile(k_gain, A_HEADS).reshape(1, D_MODEL)
    tri = jnp.asarray(np.tril(np.ones((tm, tm), np.float32)), BF16)
    selq, selk = _a_sel_matrices()
    shift_row = jnp.full((1, LANES), LOG2E, F32) * shift
    full = lambda shape: pl.BlockSpec(shape, lambda i: (0,) * len(shape))
    out_spec = pl.BlockSpec((tm, wide), lambda i: (i, 0))
    per_tile = tm // FOX_BLOCK
    return pl.pallas_call(
        functools.partial(_a_proj_kernel, seq // tm),
        grid=(t // tm,),
        in_specs=[pl.BlockSpec((tm, D_MODEL), lambda i: (i, 0)),
                  full((1, D_MODEL)), full((D_MODEL, 3 * D_MODEL)), full((D_MODEL, LANES)),
                  full((1, LANES)), full((1, D_MODEL)), full((1, D_MODEL)),
                  full((MXU_DIM, MXU_DIM)), full((tm, tm)), full((LANES, wide)), full((LANES, wide)),
                  full((1, LANES))],
        out_specs=[out_spec, out_spec,
                   pl.BlockSpec((per_tile, A_HEADS, LANES, FOX_BLOCK), lambda i: (i, 0, 0, 0))],
        out_shape=[jax.ShapeDtypeStruct((t, wide), BF16), jax.ShapeDtypeStruct((t, wide), BF16),
                   jax.ShapeDtypeStruct((t // FOX_BLOCK, A_HEADS, LANES, FOX_BLOCK), BF16)],
        scratch_shapes=[pltpu.VMEM((8, LANES), F32)],
        compiler_params=_params(("arbitrary",)),
        name="a_proj",
    )(h2d, norm_gain.reshape(1, D_MODEL), w_qkv, w_f, b_row, qg, kg,
      _block_diag_ones(A_HEAD_DIM), tri, selq, selk, shift_row)


def _fox_pair_output(accs_t, o_ref, rows):
    outs = [(a / a[X_BASE:X_BASE + 1, :]).T for a in accs_t]
    lane = lax.broadcasted_iota(jnp.int32, outs[0].shape, 1)
    o_ref[rows, :] = jnp.where(lane < A_HEAD_DIM, outs[0],
                               pltpu.roll(outs[1], A_HEAD_DIM, 1)).astype(o_ref.dtype)


def _fox_shifted_kernel(q_ref, k_ref, vt_ref, o_ref, acc_ref):
    blk = FOX_BLOCK
    row = lax.broadcasted_iota(jnp.int32, (blk, blk), 0)
    col = lax.broadcasted_iota(jnp.int32, (blk, blk), 1)
    key_visible = row <= col

    def q_tile(i, _):
        rows = pl.ds(pl.multiple_of(i * blk, blk), blk)
        qs = [q_ref[rows, hh * LANES:(hh + 1) * LANES] for hh in range(2)]

        def block(j, masked):
            keys = pl.ds(pl.multiple_of(j * blk, blk), blk)
            ss = [_dot_nt(k_ref[keys, hh * LANES:(hh + 1) * LANES], qs[hh]) for hh in range(2)]
            for hh in range(2):
                p = jnp.exp2(ss[hh])
                if masked:
                    p = jnp.where(key_visible, p, 0.0)
                acc_ref[hh] += _dot(vt_ref[j, hh], p.astype(BF16))

        acc_ref[...] = jnp.zeros_like(acc_ref)

        @pl.loop(0, i // 2)
        def _(t):
            block(2 * t, False)
            block(2 * t + 1, False)

        @pl.when(i % 2 == 1)
        def _():
            block(i - 1, False)

        block(i, True)
        _fox_pair_output((acc_ref[0], acc_ref[1]), o_ref, rows)
        return 0

    lax.fori_loop(0, q_ref.shape[0] // blk, q_tile, 0)


def _fox_online_kernel(q_ref, k_ref, vt_ref, o_ref):
    blk = FOX_BLOCK
    qi = pl.program_id(2)
    qs = [q_ref[:, hh * LANES:(hh + 1) * LANES] for hh in range(2)]
    row = lax.broadcasted_iota(jnp.int32, (blk, blk), 0)
    col = lax.broadcasted_iota(jnp.int32, (blk, blk), 1)
    key_visible = row <= col

    def block(j, carry, masked):
        keys = pl.ds(pl.multiple_of(j * blk, blk), blk)
        new = []
        for hh in range(2):
            m, acc = carry[2 * hh], carry[2 * hh + 1]
            s = _dot_nt(k_ref[keys, hh * LANES:(hh + 1) * LANES], qs[hh])
            if masked:
                s = jnp.where(key_visible, s, NEG_INF)
            m_new = jnp.maximum(m, jnp.max(s, axis=0, keepdims=True))
            p = jnp.exp2(s - m_new)
            acc = jnp.exp2(m - m_new) * acc + _dot(vt_ref[j, hh], p.astype(BF16))
            new += [m_new, acc]
        return tuple(new)

    init = (jnp.full((1, blk), NEG_INF, F32), jnp.zeros((LANES, blk), F32)) * 2
    carry = lax.fori_loop(0, qi, functools.partial(block, masked=False), init)
    carry = block(qi, carry, True)
    _fox_pair_output((carry[1], carry[3]), o_ref, slice(None))


def _fox_attention(q, k, vt, batch, seq, fixed_shift):
    pairs = A_HEADS // 2
    blk = FOX_BLOCK
    q3, k3 = (a.reshape(batch, seq, A_HEADS * LANES) for a in (q, k))
    vt5 = vt.reshape(batch, seq // blk, A_HEADS, LANES, blk)
    out_shape = jax.ShapeDtypeStruct((batch, seq, D_MODEL), BF16)

    def shifted(q3, k3, vt5):
        pair = pl.BlockSpec((None, seq, 2 * LANES), lambda b, p: (b, 0, p))
        return pl.pallas_call(
            _fox_shifted_kernel,
            grid=(batch, pairs),
            in_specs=[pair, pair,
                      pl.BlockSpec((None, seq // blk, 2, LANES, blk), lambda b, p: (b, 0, p, 0, 0))],
            out_specs=pl.BlockSpec((None, seq, LANES), lambda b, p: (b, 0, p)),
            out_shape=out_shape,
            scratch_shapes=[pltpu.VMEM((2, LANES, blk), F32)],
            compiler_params=_params(("parallel", "parallel")),
            name="fox_shifted",
        )(q3, k3, vt5)

    def online(q3, k3, vt5):
        return pl.pallas_call(
            _fox_online_kernel,
            grid=(batch, pairs, seq // blk),
            in_specs=[pl.BlockSpec((None, blk, 2 * LANES), lambda b, p, i: (b, i, p)),
                      pl.BlockSpec((None, seq, 2 * LANES), lambda b, p, i: (b, 0, p)),
                      pl.BlockSpec((None, seq // blk, 2, LANES, blk), lambda b, p, i: (b, 0, p, 0, 0))],
            out_specs=pl.BlockSpec((None, blk, LANES), lambda b, p, i: (b, i, p)),
            out_shape=out_shape,
            compiler_params=_params(("parallel", "parallel", "arbitrary")),
            name="fox_online",
        )(q3, k3, vt5)

    return shifted(q3, k3, vt5) if fixed_shift else online(q3, k3, vt5)


def _out_proj_kernel(h_ref, o_ref, w_ref, out_ref):
    out_ref[...] = h_ref[...] + _dot(o_ref[...], w_ref[...])


def _out_proj(h2d, o2d, w_out):
    t = h2d.shape[0]
    tm = 1024
    tile = pl.BlockSpec((tm, D_MODEL), lambda i: (i, 0))
    return pl.pallas_call(
        _out_proj_kernel,
        grid=(t // tm,),
        in_specs=[tile, tile, pl.BlockSpec((D_MODEL, D_MODEL), lambda i: (0, 0))],
        out_specs=tile,
        out_shape=jax.ShapeDtypeStruct((t, D_MODEL), F32),
        compiler_params=_params(("parallel",)),
        name="out_proj",
    )(h2d, o2d, w_out.astype(BF16))


def _b_proj_kernel(n_rope_cols, h_ref, g_ref, w_ref, gain_ref, ones_ref, cos_ref, sin_ref, out_ref):
    xn = _rms_rows(h_ref[...], g_ref[...]).astype(BF16)
    cos, sin = cos_ref[...], sin_ref[...]
    ones_bd = ones_ref[...]
    for ch in range(w_ref.shape[1] // MXU_DIM):
        cs = slice(ch * MXU_DIM, (ch + 1) * MXU_DIM)
        y = _dot(xn, w_ref[:, cs])
        if ch * MXU_DIM < n_rope_cols:
            y = _seg_norm(y, ones_bd, B_HEAD_DIM) * gain_ref[:, cs]
            for j in range(MXU_DIM // B_HEAD_DIM):
                blk = slice(j * B_HEAD_DIM, (j + 1) * B_HEAD_DIM)
                out_ref[:, ch * MXU_DIM + j * B_HEAD_DIM:ch * MXU_DIM + (j + 1) * B_HEAD_DIM] = (
                    _rope(y[:, blk], cos, sin).astype(out_ref.dtype))
        else:
            out_ref[:, cs] = y.astype(out_ref.dtype)


def _b_proj(h2d, norm_gain, w, head_gain_row, n_rope_cols, cos, sin):
    t = h2d.shape[0]
    n = w.shape[1]
    tm = 512
    full = lambda shape: pl.BlockSpec(shape, lambda i: (0,) * len(shape))
    return pl.pallas_call(
        functools.partial(_b_proj_kernel, n_rope_cols),
        grid=(t // tm,),
        in_specs=[pl.BlockSpec((tm, D_MODEL), lambda i: (i, 0)), full((1, D_MODEL)),
                  full((D_MODEL, n)), full((1, n)), full((MXU_DIM, MXU_DIM)),
                  pl.BlockSpec((tm, B_HEAD_DIM), lambda i: (i, 0)),
                  pl.BlockSpec((tm, B_HEAD_DIM), lambda i: (i, 0))],
        out_specs=pl.BlockSpec((tm, n), lambda i: (i, 0)),
        out_shape=jax.ShapeDtypeStruct((t, n), BF16),
        compiler_params=_params(("parallel",)),
        name="b_proj",
    )(h2d, norm_gain.reshape(1, D_MODEL), w.astype(BF16), head_gain_row,
      _block_diag_ones(B_HEAD_DIM), cos, sin)


def _dilated_kernel(q_ref, kc_ref, kp_ref, vc_ref, vp_ref, bp_ref, bc_ref, o_ref, lse_ref):
    tl = q_ref.shape[0]
    blk = BAND_BLOCK
    first_pen = jnp.where(pl.program_id(2) == 0, NEG_INF, 0.0)
    bias_prev, bias_cur = bp_ref[...], bc_ref[...]
    lane = lax.broadcasted_iota(jnp.int32, (blk, LANES), 1)
    for i in range(tl // blk):
        rows = slice(i * blk, (i + 1) * blk)
        lse_tile = jnp.zeros((blk, LANES), F32)
        for h in range(B_HEADS):
            hs = slice(h * B_HEAD_DIM, (h + 1) * B_HEAD_DIM)
            q = q_ref[rows, hs]
            if i == 0:
                k_prev, v_prev = kp_ref[:, hs], vp_ref[:, hs]
            else:
                prev_rows = slice((i - 1) * blk, i * blk)
                k_prev, v_prev = kc_ref[prev_rows, hs], vc_ref[prev_rows, hs]
            s_prev = _dot_nt(q, k_prev) + bias_prev
            if i == 0:
                s_prev = s_prev + first_pen
            s_cur = _dot_nt(q, kc_ref[rows, hs]) + bias_cur
            m = jnp.maximum(jnp.max(s_prev, axis=-1, keepdims=True),
                            jnp.max(s_cur, axis=-1, keepdims=True))
            p_prev = jnp.exp(s_prev - m)
            p_cur = jnp.exp(s_cur - m)
            l = jnp.sum(p_prev, axis=-1, keepdims=True) + jnp.sum(p_cur, axis=-1, keepdims=True)
            o = (_dot(p_prev.astype(BF16), v_prev) + _dot(p_cur.astype(BF16), vc_ref[rows, hs])) / l
            o_ref[rows, hs] = o.astype(o_ref.dtype)
            lse_tile = jnp.where(lane == h, m + jnp.log(l), lse_tile)
        lse_ref[rows, :] = lse_tile


def _band_biases():
    qi = np.arange(BAND_BLOCK)[:, None]
    kk = np.arange(BAND_BLOCK)[None, :]
    prev = np.where(kk >= qi, 0.0, -np.inf).astype(np.float32)
    cur = np.where(kk <= qi, 0.0, -np.inf).astype(np.float32)
    return jnp.asarray(prev), jnp.asarray(cur)


def _dilated_group(q_all, kv, group, batch, seq):
    d = B_DILATIONS[group]
    assert B_WINDOWS[group] // d == BAND_BLOCK and seq % (d * BAND_BLOCK) == 0
    length = seq // d
    tl = 256
    width = B_HEADS * B_HEAD_DIM
    sub = tl // BAND_BLOCK
    n_groups = len(B_DILATIONS)
    qv = q_all.reshape(batch, length, d * n_groups * width)
    kvv = kv.reshape(batch, length, d * 2 * width)

    def cur(part):
        return pl.BlockSpec((None, tl, width), lambda b, r, n: (b, n, r * 2 + part))

    def prev(part):
        return pl.BlockSpec((None, BAND_BLOCK, width),
                            lambda b, r, n: (b, jnp.maximum(n * sub - 1, 0), r * 2 + part))

    bias = pl.BlockSpec((BAND_BLOCK, BAND_BLOCK), lambda b, r, n: (0, 0))
    bias_prev, bias_cur = _band_biases()
    o, lse = pl.pallas_call(
        _dilated_kernel,
        grid=(batch, d, length // tl),
        in_specs=[pl.BlockSpec((None, tl, width), lambda b, r, n: (b, n, r * n_groups + group)),
                  cur(0), prev(0), cur(1), prev(1), bias, bias],
        out_specs=[pl.BlockSpec((None, tl, width), lambda b, r, n: (b, n, r)),
                   pl.BlockSpec((None, tl, LANES), lambda b, r, n: (b, n, r))],
        out_shape=[jax.ShapeDtypeStruct((batch, length, d * width), BF16),
                   jax.ShapeDtypeStruct((batch, length, d * LANES), F32)],
        compiler_params=_params(("parallel", "parallel", "arbitrary")),
        name=f"dilated_g{group}",
    )(qv, kvv, kvv, kvv, kvv, bias_prev, bias_cur)
    return o.reshape(batch * seq, width), lse.reshape(batch * seq, LANES)


def _b_out_kernel(h_ref, o0_ref, o1_ref, o2_ref, l0_ref, l1_ref, l2_ref, w_ref, out_ref, merged_ref):
    lses = [l0_ref[...], l1_ref[...], l2_ref[...]]
    top = jnp.maximum(jnp.maximum(lses[0], lses[1]), lses[2])
    es = [jnp.exp(x - top) for x in lses]
    den = es[0] + es[1] + es[2]
    ws = [x / den for x in es]
    o_refs = (o0_ref, o1_ref, o2_ref)
    for h in range(B_HEADS):
        hs = slice(h * B_HEAD_DIM, (h + 1) * B_HEAD_DIM)
        acc = ws[0][:, h:h + 1] * o_refs[0][:, hs].astype(F32)
        for g in (1, 2):
            acc = acc + ws[g][:, h:h + 1] * o_refs[g][:, hs].astype(F32)
        merged_ref[:, hs] = acc.astype(BF16)
    out_ref[...] = h_ref[...] + _dot(merged_ref[...], w_ref[...])


def _b_out(h2d, outs, lses, w_out):
    t = h2d.shape[0]
    tm = 512
    tile = pl.BlockSpec((tm, D_MODEL), lambda i: (i, 0))
    ltile = pl.BlockSpec((tm, LANES), lambda i: (i, 0))
    return pl.pallas_call(
        _b_out_kernel,
        grid=(t // tm,),
        in_specs=[tile, tile, tile, tile, ltile, ltile, ltile,
                  pl.BlockSpec((D_MODEL, D_MODEL), lambda i: (0, 0))],
        out_specs=tile,
        out_shape=jax.ShapeDtypeStruct((t, D_MODEL), F32),
        scratch_shapes=[pltpu.VMEM((tm, D_MODEL), BF16)],
        compiler_params=_params(("parallel",)),
        name="b_out",
    )(h2d, *outs, *lses, w_out.astype(BF16))


def _b_proj_res_kernel(plan, n_out, h_ref, g_ref, w_ref, gain_ref, cos_ref, sin_ref, *rest):
    out_refs, stage_ref = rest[:n_out], rest[n_out]
    tm = h_ref.shape[0]
    xn = _rms_rows(h_ref[...], g_ref[...]).astype(BF16)
    cos, sin = cos_ref[...], sin_ref[...]
    for ch, (roped, dests) in enumerate(plan):
        y = _dot(xn, w_ref[:, ch * MXU_DIM:(ch + 1) * MXU_DIM])
        slot = ch % 2
        for j in range(MXU_DIM // LANES):
            blk = y[:, j * LANES:(j + 1) * LANES]
            if roped:
                cols = slice(ch * MXU_DIM + j * LANES, ch * MXU_DIM + (j + 1) * LANES)
                blk = _rope(_rms_rows(blk, gain_ref[:, cols]), cos, sin)
            stage_ref[slot, j] = blk
        for oi, d, c0 in dests:
            for r in range(d):
                for j in range(MXU_DIM // LANES):
                    out_refs[oi][r, :, c0 + j * LANES:c0 + (j + 1) * LANES] = (
                        stage_ref[slot, j, pl.ds(r, tm // d, stride=d), :].astype(BF16))


def _b_proj_res(h2d, batch, seq, norm_gain, w, head_gain_row, plan, outs, cos, sin):
    n = w.shape[1]
    tm = 512
    per_seq = seq // tm
    full = lambda shape: pl.BlockSpec(shape, lambda b, i: (0,) * len(shape))
    rows = lambda width: pl.BlockSpec((tm, width), lambda b, i: (b * per_seq + i, 0))
    return pl.pallas_call(
        functools.partial(_b_proj_res_kernel, plan, len(outs)),
        grid=(batch, per_seq),
        in_specs=[rows(D_MODEL), full((1, D_MODEL)), full((D_MODEL, n)), full((1, n)),
                  rows(B_HEAD_DIM), rows(B_HEAD_DIM)],
        out_specs=[pl.BlockSpec((None, d, tm // d, width), lambda b, i: (b, 0, i, 0)) for d, width in outs],
        out_shape=[jax.ShapeDtypeStruct((batch, d, seq // d, width), BF16) for d, width in outs],
        scratch_shapes=[pltpu.VMEM((2, MXU_DIM // LANES, tm, LANES), F32)],
        compiler_params=_params(("parallel", "parallel")),
        name="b_proj_res",
    )(h2d, norm_gain.reshape(1, D_MODEL), w.astype(BF16), head_gain_row, cos, sin)


def _dilated_shifted_kernel(q_ref, kc_ref, kp_ref, vc_ref, vp_ref, bias_ref, bias0_ref,
                            o_ref, l_ref, kbuf, vbuf):
    tl = q_ref.shape[0]
    blk = BAND_BLOCK
    kbuf[0:blk, :] = kp_ref[...]
    kbuf[blk:, :] = kc_ref[...]
    vbuf[0:blk, :] = vp_ref[...]
    vbuf[blk:, :] = vc_ref[...]
    bias = bias_ref[...]
    bias_first = jnp.where(pl.program_id(2) == 0, bias0_ref[...], bias)
    lane = lax.broadcasted_iota(jnp.int32, (blk, LANES), 1)
    for i in range(tl // blk):
        rows = slice(i * blk, (i + 1) * blk)
        keys = slice(i * blk, (i + 2) * blk)
        l_tile = jnp.zeros((blk, LANES), F32)
        for h in range(B_HEADS):
            hs = slice(h * B_HEAD_DIM, (h + 1) * B_HEAD_DIM)
            s = _dot_nt(q_ref[rows, hs], kbuf[keys, hs]) + (bias_first if i == 0 else bias)
            p = jnp.exp2(s)
            o_ref[rows, hs] = _dot(p.astype(BF16), vbuf[keys, hs]).astype(o_ref.dtype)
            l_tile = jnp.where(lane == h, jnp.sum(p, axis=-1, keepdims=True), l_tile)
        l_ref[rows, :] = l_tile


def _band_shift_biases(shift_log2):
    qi = np.arange(BAND_BLOCK)[:, None]
    kk = np.arange(2 * BAND_BLOCK)[None, :]
    band = (kk >= qi) & (kk <= qi + BAND_BLOCK)
    band0 = band & (kk >= BAND_BLOCK)
    neg = jnp.full((BAND_BLOCK, 2 * BAND_BLOCK), NEG_INF, F32)
    return jnp.where(band, -shift_log2, neg), jnp.where(band0, -shift_log2, neg)


def _dilated_shifted(q_g, kv_d, d, batch, seq, biases):
    length = seq // d
    tl = min(1024, length)
    width = B_HEADS * B_HEAD_DIM
    sub = tl // BAND_BLOCK

    def cur(part):
        return pl.BlockSpec((None, None, tl, width), lambda b, r, n: (b, r, n, part))

    def prev(part):
        return pl.BlockSpec((None, None, BAND_BLOCK, width),
                            lambda b, r, n: (b, r, jnp.maximum(n * sub - 1, 0), part))

    bias = pl.BlockSpec((BAND_BLOCK, 2 * BAND_BLOCK), lambda b, r, n: (0, 0))
    return pl.pallas_call(
        _dilated_shifted_kernel,
        grid=(batch, d, length // tl),
        in_specs=[cur(0), cur(0), prev(0), cur(1), prev(1), bias, bias],
        out_specs=[cur(0), pl.BlockSpec((None, None, tl, LANES), lambda b, r, n: (b, r, n, 0))],
        out_shape=[jax.ShapeDtypeStruct((batch, d, length, width), BF16),
                   jax.ShapeDtypeStruct((batch, d, length, LANES), F32)],
        scratch_shapes=[pltpu.VMEM((tl + BAND_BLOCK, width), BF16)] * 2,
        compiler_params=_params(("parallel", "parallel", "arbitrary")),
        name=f"dilated_shifted_d{d}",
    )(q_g, kv_d, kv_d, kv_d, kv_d, *biases)


def _b_out_sum_kernel(dils, h_ref, *rest):
    n = len(dils)
    a_refs, l_refs = rest[:n], rest[n:2 * n]
    w_ref, out_ref, acc_ref, den_ref, merged_ref = rest[2 * n:]
    tm = h_ref.shape[0]
    for g, d in enumerate(dils):
        for r in range(d):
            idx = pl.ds(r, tm // d, stride=d)
            planes = [(den_ref, (idx, slice(None)), l_refs[g][r])]
            planes += [(acc_ref, (h, idx, slice(None)),
                        a_refs[g][r, :, h * B_HEAD_DIM:(h + 1) * B_HEAD_DIM].astype(F32))
                       for h in range(B_HEADS)]
            for ref, at, val in planes:
                ref[at] = val if g == 0 else ref[at] + val
    den = den_ref[...]
    for h in range(B_HEADS):
        hs = slice(h * B_HEAD_DIM, (h + 1) * B_HEAD_DIM)
        merged_ref[:, hs] = (acc_ref[h] / den[:, h:h + 1]).astype(BF16)
    out_ref[...] = h_ref[...] + _dot(merged_ref[...], w_ref[...])


def _b_out_sum(h2d, batch, seq, accs, dens, w_out):
    tm = 512
    per_seq = seq // tm
    dils = B_DILATIONS
    tile = pl.BlockSpec((tm, D_MODEL), lambda b, i: (b * per_seq + i, 0))
    res = lambda d, width: pl.BlockSpec((None, d, tm // d, width), lambda b, i: (b, 0, i, 0))
    return pl.pallas_call(
        functools.partial(_b_out_sum_kernel, dils),
        grid=(batch, per_seq),
        in_specs=([tile] + [res(d, D_MODEL) for d in dils] + [res(d, LANES) for d in dils]
                  + [pl.BlockSpec((D_MODEL, D_MODEL), lambda b, i: (0, 0))]),
        out_specs=tile,
        out_shape=jax.ShapeDtypeStruct(h2d.shape, F32),
        scratch_shapes=[pltpu.VMEM((B_HEADS, tm, B_HEAD_DIM), F32), pltpu.VMEM((tm, LANES), F32),
                        pltpu.VMEM((tm, D_MODEL), BF16)],
        compiler_params=_params(("parallel", "parallel")),
        name="b_out_sum",
    )(h2d, *accs, *dens, w_out.astype(BF16))


def _route(logits):
    tm = logits.shape[0]
    lane = lax.broadcasted_iota(jnp.int32, (tm, LANES), 1)
    lanef = lane.astype(F32)
    far = float(LANES)
    is_g = lane < MOE_GROUPS
    g_max = jnp.max(jnp.where(is_g, logits, NEG_INF), axis=-1, keepdims=True)
    g_sum = jnp.sum(jnp.where(is_g, jnp.exp(logits - g_max), 0.0), axis=-1, keepdims=True)
    g_idx = jnp.min(jnp.where(is_g, jnp.where(logits == g_max, lanef, far), far), axis=-1, keepdims=True)
    per_group = MOE_EXPERTS // MOE_GROUPS
    e_lo = ROUTE_E0 + g_idx * per_group
    in_group = jnp.where(lanef >= e_lo, jnp.where(lanef < e_lo + per_group, 1.0, 0.0), 0.0)
    cand1 = jnp.where(in_group > 0.0, logits, NEG_INF)
    v1 = jnp.max(cand1, axis=-1, keepdims=True)
    i1 = jnp.min(jnp.where(cand1 == v1, lanef, far), axis=-1, keepdims=True)
    cand2 = jnp.where(lanef == i1, NEG_INF, cand1)
    v2 = jnp.max(cand2, axis=-1, keepdims=True)
    i2 = jnp.min(jnp.where(cand2 == v2, lanef, far), axis=-1, keepdims=True)
    e21 = jnp.exp(v2 - v1)
    w1 = 1.0 / (1.0 + e21)
    w2 = e21 / (1.0 + e21)
    comb = (jnp.where(lanef == i1, w1, 0.0) + jnp.where(lanef == i2, w2, 0.0)) / g_sum
    return comb, jnp.where(lanef == g_idx, 1.0, 0.0)


def _moe_kernel(h_ref, g_ref, wr12_ref, wr1_ref, br_ref, tri_ref, wgu_ref, wd_ref, out_ref,
                xn_ref, c1_ref, c2_ref, posc_ref, posr_ref, hid_ref, meta_ref):
    grp = pl.program_id(1)
    per_group = MOE_EXPERTS // MOE_GROUPS
    tm = h_ref.shape[0]
    ch = MOE_CHUNK

    @pl.when(grp == 0)
    def _():
        xn = _rms_rows(h_ref[...], g_ref[...])
        x1 = xn.astype(BF16)
        x2 = (xn - x1.astype(F32)).astype(BF16)
        first = _dot(x1, wr12_ref[...])
        logits = first[:, :LANES] + (first[:, LANES:] + _dot(x2, wr1_ref[...])) + br_ref[...]
        comb, onehot = _route(logits)
        c1 = comb.astype(BF16)
        c1_ref[...] = c1
        c2_ref[...] = (comb - c1.astype(F32)).astype(BF16)
        xn_ref[...] = x1
        count = _dot(tri_ref[...], onehot.astype(BF16))
        padded = jnp.floor((count[tm - 1:tm, :] + (ch - 1)) * (1.0 / ch)) * ch
        lane1 = lax.broadcasted_iota(jnp.int32, (1, LANES), 1)
        prev = jnp.where(lane1 >= 1, pltpu.roll(padded, 1, 1), 0.0)
        start = (prev + jnp.where(lane1 >= 2, pltpu.roll(prev, 1, 1), 0.0)
                 + jnp.where(lane1 >= 3, pltpu.roll(prev, 2, 1), 0.0))
        slot = jnp.sum(onehot * (start + count - 1.0), axis=-1, keepdims=True)
        slot_b = jnp.broadcast_to(slot, (tm, LANES))
        posc_ref[...] = slot_b
        posr_ref[...] = slot_b.T[0:8, :]
        for g in range(MOE_GROUPS):
            pick = lane1 == g
            meta_ref[g] = jnp.sum(jnp.where(pick, padded, 0.0)).astype(jnp.int32) // ch
            meta_ref[MOE_GROUPS + g] = jnp.sum(jnp.where(pick, start, 0.0)).astype(jnp.int32)
        out_ref[...] = h_ref[...]

    lane = lax.broadcasted_iota(jnp.int32, (ch, LANES), 1)
    base = meta_ref[MOE_GROUPS + grp]

    @pl.loop(0, meta_ref[grp])
    def _(c):
        first_slot = (base + c * ch).astype(F32)
        rows = lax.broadcasted_iota(jnp.int32, (ch, tm), 0).astype(F32) + first_slot
        gather = jnp.where(posr_ref[0:1, :] == rows, 1.0, 0.0).astype(BF16)
        xs = _dot(gather, xn_ref[...]).astype(BF16)
        cs = _dot(gather, c1_ref[...]) + _dot(gather, c2_ref[...])
        for e in range(per_group):
            gate_w = jnp.sum(jnp.where(lane == grp * per_group + (ROUTE_E0 + e), cs, 0.0),
                             axis=-1, keepdims=True)
            gu = _dot(xs, wgu_ref[e])
            g, u = gu[:, :MOE_FF], gu[:, MOE_FF:]
            hid_ref[:, e * MOE_FF:(e + 1) * MOE_FF] = (
                (g / (1.0 + jnp.exp(-g))) * u * gate_w).astype(BF16)
        y = _dot(hid_ref[...], wd_ref[...]).astype(BF16)
        cols = lax.broadcasted_iota(jnp.int32, (tm, ch), 1).astype(F32) + first_slot
        scatter = jnp.where(posc_ref[:, 0:1] == cols, 1.0, 0.0).astype(BF16)
        out_ref[...] += _dot(scatter, y)


def _moe_weights(group_w, group_b, expert_w, expert_b, w_gate, w_up, w_down):
    pad = LANES - MOE_GROUPS - MOE_EXPERTS
    w_r = jnp.pad(jnp.concatenate([group_w, expert_w], axis=-1), ((0, 0), (0, 0), (0, pad)))
    r1 = w_r.astype(BF16)
    r2 = (w_r - r1.astype(F32)).astype(BF16)
    b_r = jnp.pad(jnp.concatenate([group_b, expert_b], axis=-1), ((0, 0), (0, pad)))[:, None, :]
    layers = w_gate.shape[0]
    per_group = MOE_EXPERTS // MOE_GROUPS
    w_gu = jnp.concatenate([w_gate, w_up], axis=-1).astype(BF16).reshape(
        layers, MOE_GROUPS, per_group, D_MODEL, 2 * MOE_FF)
    w_d = w_down.astype(BF16).reshape(layers, MOE_GROUPS, per_group * MOE_FF, D_MODEL)
    return jnp.concatenate([r1, r2], axis=-1), r1, b_r, w_gu, w_d


def _moe(h2d, norm_gain, layer, weights):
    wr12, wr1, b_r, w_gu, w_d = weights
    t = h2d.shape[0]
    tm = 1024
    per_group = MOE_EXPERTS // MOE_GROUPS
    tile = pl.BlockSpec((tm, D_MODEL), lambda i, g: (i, 0))
    of_layer = lambda shape: pl.BlockSpec((None,) + shape, lambda i, g: (layer,) + (0,) * len(shape))
    tri = jnp.asarray(np.tril(np.ones((tm, tm), np.float32)), BF16)
    return pl.pallas_call(
        _moe_kernel,
        grid=(t // tm, MOE_GROUPS),
        in_specs=[tile, pl.BlockSpec((1, D_MODEL), lambda i, g: (0, 0)),
                  of_layer((D_MODEL, 2 * LANES)), of_layer((D_MODEL, LANES)), of_layer((1, LANES)),
                  pl.BlockSpec((tm, tm), lambda i, g: (0, 0)),
                  pl.BlockSpec((None, None, per_group, D_MODEL, 2 * MOE_FF), lambda i, g: (layer, g, 0, 0, 0)),
                  pl.BlockSpec((None, None, per_group * MOE_FF, D_MODEL), lambda i, g: (layer, g, 0, 0))],
        out_specs=tile,
        out_shape=jax.ShapeDtypeStruct((t, D_MODEL), F32),
        scratch_shapes=[pltpu.VMEM((tm, D_MODEL), BF16), pltpu.VMEM((tm, LANES), BF16),
                        pltpu.VMEM((tm, LANES), BF16), pltpu.VMEM((tm, LANES), F32),
                        pltpu.VMEM((8, tm), F32), pltpu.VMEM((MOE_CHUNK, per_group * MOE_FF), BF16),
                        pltpu.SMEM((2 * MOE_GROUPS,), jnp.int32)],
        compiler_params=_params(("parallel", "arbitrary")),
        name="moe",
    )(h2d, norm_gain.reshape(1, D_MODEL), wr12, wr1, b_r, tri, w_gu, w_d)


def kernel(x, positions, a_norm, a_w_in, a_b_f, a_q_gain, a_k_gain, a_w_out, kv_norm, kv_w, kv_k_gain, b_norm, b_w_q, b_q_gain, b_w_out, ffn_norm, moe_group_w, moe_group_b, moe_expert_w, moe_expert_b, moe_w_gate, moe_w_up, moe_w_down):
    batch, seq, _ = x.shape
    n_a = a_norm.shape[0]
    n_b = b_norm.shape[0]
    b_width = B_HEADS * B_HEAD_DIM
    n_groups = len(B_DILATIONS)
    chunks = b_width // MXU_DIM

    a_shifts = [(A_HEAD_DIM ** 0.5) * jnp.max(jnp.abs(a_q_gain[i])) * jnp.max(jnp.abs(a_k_gain[i]))
                for i in range(n_a)]
    b_shifts = [(B_HEAD_DIM ** 0.5) * jnp.max(jnp.abs(b_q_gain[j])) * jnp.max(jnp.abs(kv_k_gain))
                for j in range(n_b)]

    moe_weights = _moe_weights(moe_group_w, moe_group_b, moe_expert_w, moe_expert_b,
                               moe_w_gate, moe_w_up, moe_w_down)

    def trunk(fixed_shift, x):
        h = x.reshape(batch * seq, D_MODEL)
        kv_sh = cos = sin = None
        for layer in range(n_a + n_b):
            if layer < n_a:
                i = layer
                q, k, vt = _a_proj(h, seq, a_norm[i], a_w_in[i], a_b_f[i], a_q_gain[i], a_k_gain[i],
                                   a_shifts[i])
                o = _fox_attention(q, k, vt, batch, seq, fixed_shift)
                h = _out_proj(h, o.reshape(batch * seq, D_MODEL), a_w_out[i])
            else:
                j = layer - n_a
                k_gain_row = jnp.concatenate(
                    [jnp.tile(kv_k_gain, B_HEADS), jnp.ones((b_width,), F32)]).reshape(1, 2 * b_width)
                q_gain_row = (jnp.tile(b_q_gain[j], (1, B_HEADS)) * (B_HEAD_DIM ** -0.5)).reshape(
                    1, n_groups * b_width)
                if j == 0:
                    cos, sin = _rope_tables(positions)
                if fixed_shift:
                    if j == 0:
                        kv_plan = tuple((ch < chunks, tuple((g, d, ch * MXU_DIM) for g, d in enumerate(B_DILATIONS)))
                                        for ch in range(2 * chunks))
                        kv_sh = _b_proj_res(h, batch, seq, kv_norm, kv_w, k_gain_row, kv_plan,
                                            [(d, 2 * b_width) for d in B_DILATIONS], cos, sin)
                    q_plan = tuple((True, ((ch // chunks, B_DILATIONS[ch // chunks], (ch % chunks) * MXU_DIM),))
                                   for ch in range(n_groups * chunks))
                    q_res = _b_proj_res(h, batch, seq, b_norm[j], b_w_q[j], q_gain_row * LOG2E, q_plan,
                                        [(d, b_width) for d in B_DILATIONS], cos, sin)
                    biases = _band_shift_biases(b_shifts[j] * LOG2E)
                    accs, dens = zip(*[_dilated_shifted(q_res[g], kv_sh[g], d, batch, seq, biases)
                                       for g, d in enumerate(B_DILATIONS)])
                    h = _b_out_sum(h, batch, seq, accs, dens, b_w_out[j])
                else:
                    if j == 0:
                        kv_sh = _b_proj(h, kv_norm, kv_w, k_gain_row, b_width, cos, sin)
                    q_all = _b_proj(h, b_norm[j], b_w_q[j], q_gain_row, n_groups * b_width, cos, sin)
                    outs, lses = zip(*[_dilated_group(q_all, kv_sh, g, batch, seq) for g in range(n_groups)])
                    h = _b_out(h, outs, lses, b_w_out[j])
            h = _moe(h, ffn_norm[layer], layer, moe_weights)
        return h.reshape(batch, seq, D_MODEL)

    largest = functools.reduce(jnp.maximum, a_shifts + b_shifts)
    return lax.cond(largest <= MAX_SAFE_SHIFT, functools.partial(trunk, True),
                    functools.partial(trunk, False), x)
```

```python
import functools
import math

import numpy as np
import jax
import jax.numpy as jnp
from jax import lax
from jax.experimental import pallas as pl
from jax.experimental.pallas import tpu as pltpu

F32 = jnp.float32
BF16 = jnp.bfloat16

D_MODEL = 1024
EPS = 1e-6
ROPE_THETA = 10000.0
A_HEADS = 16
A_HEAD_DIM = 64
B_HEADS = 8
B_HEAD_DIM = 128
B_WINDOWS = (128, 512, 2048)
B_DILATIONS = (1, 4, 16)
BAND_BLOCK = 128
MOE_GROUPS = 4
MOE_EXPERTS = 16
MOE_FF = 256
MOE_CHUNK = 256

LANES = 128
MXU_DIM = 256
VMEM_LIMIT = 56 * 1024 * 1024

NEG_INF = float("-inf")
LOG2E = math.log2(math.e)
FOX_BLOCK = 512
FOX_UNROLL = 4
MAX_SAFE_SHIFT = 40.0

X_BASE = A_HEAD_DIM
ROUTE_E0 = MOE_GROUPS


def _params(sem):
    return pltpu.CompilerParams(dimension_semantics=sem, vmem_limit_bytes=VMEM_LIMIT)


def _rms_rows(x, gain_row):
    ms = jnp.mean(x * x, axis=-1, keepdims=True)
    return x * lax.rsqrt(ms + EPS) * gain_row


def _dot(a, b):
    return jnp.dot(a, b, preferred_element_type=F32)


def _dot_nt(a, b):
    return lax.dot_general(a, b, (((1,), (1,)), ((), ())), preferred_element_type=F32)


def _split3(x):
    p1 = x.astype(BF16)
    r1 = x - p1.astype(F32)
    p2 = r1.astype(BF16)
    p3 = (r1 - p2.astype(F32)).astype(BF16)
    return p1, p2, p3


def _seg_norm(x, ones_bd, seg):
    ss = _dot((x * x).astype(BF16), ones_bd)
    return x * lax.rsqrt(ss * (1.0 / seg) + EPS)


def _rope_table_kernel(pos_ref, inv_ref, sign_ref, cos_ref, sin_ref):
    ang = pos_ref[...].astype(F32) * inv_ref[...]
    cos_ref[...] = jnp.cos(ang)
    sin_ref[...] = jnp.sin(ang) * sign_ref[...]


def _rope_tables(positions):
    t = positions.size
    tm = 1024
    half = B_HEAD_DIM // 2
    inv = ROPE_THETA ** (-jnp.arange(half, dtype=F32) / half)
    inv = jnp.concatenate([inv, inv]).reshape(1, B_HEAD_DIM)
    sign = jnp.concatenate([-jnp.ones((half,), F32), jnp.ones((half,), F32)]).reshape(1, B_HEAD_DIM)
    row = pl.BlockSpec((1, B_HEAD_DIM), lambda i: (0, 0))
    tab = pl.BlockSpec((tm, B_HEAD_DIM), lambda i: (i, 0))
    return pl.pallas_call(
        _rope_table_kernel,
        grid=(t // tm,),
        in_specs=[pl.BlockSpec((tm, 1), lambda i: (i, 0)), row, row],
        out_specs=[tab, tab],
        out_shape=[jax.ShapeDtypeStruct((t, B_HEAD_DIM), F32)] * 2,
        compiler_params=_params(("parallel",)),
        name="rope_tables",
    )(positions.reshape(t, 1), inv, sign)


def _rope(x, cos, sin_signed):
    return x * cos + pltpu.roll(x, B_HEAD_DIM // 2, 1) * sin_signed


def _a_proj_kernel(tiles_per_seq, h_ref, g_ref, wqkv_ref, wf_ref, bf_ref, qg_ref, kg_ref,
                   ones_ref, tri_ref, selq_ref, selk_ref, shift_ref, q_ref, k_ref, vt_ref, carry_ref):
    tm = h_ref.shape[0]

    @pl.when(pl.program_id(0) % tiles_per_seq == 0)
    def _():
        carry_ref[...] = jnp.zeros_like(carry_ref)

    xn = _rms_rows(h_ref[...], g_ref[...]).astype(BF16)

    fl = _dot(xn, wf_ref[...]) + bf_ref[...]
    lf = jnp.minimum(fl, 0.0) - jnp.log(1.0 + jnp.exp(-jnp.abs(fl)))
    tri = tri_ref[...]
    l1, l2, l3 = _split3(lf)
    c = _dot(tri, l1) + _dot(tri, l2) + _dot(tri, l3) + carry_ref[0:1, :]
    carry_ref[0:1, :] = c[tm - 1:tm, :]

    lane = lax.broadcasted_iota(jnp.int32, (tm, LANES), 1)
    ck = c * LOG2E
    cq = ck - shift_ref[...]
    pieces = [p.astype(F32) for p in _split3(cq) + _split3(ck)]
    e = jnp.where(lane == 6 * A_HEADS, 1.0, 0.0)
    for n in reversed(range(6)):
        piece = pieces[n] if n == 0 else pltpu.roll(pieces[n], n * A_HEADS, 1)
        e = jnp.where((lane >= n * A_HEADS) & (lane < (n + 1) * A_HEADS), piece, e)
    e = e.astype(BF16)
    v_extra = jnp.where(lane == X_BASE, 1.0, 0.0)
    low = lane < A_HEAD_DIM
    tkv = vt_ref.shape[-1]

    ones_bd = ones_ref[...]
    n_chunks = D_MODEL // MXU_DIM
    for ch in range(n_chunks):
        cs = slice(ch * MXU_DIM, (ch + 1) * MXU_DIM)
        qf = _dot(xn, wqkv_ref[:, cs])
        kf = _dot(xn, wqkv_ref[:, D_MODEL + ch * MXU_DIM:D_MODEL + (ch + 1) * MXU_DIM])
        vf = _dot(xn, wqkv_ref[:, 2 * D_MODEL + ch * MXU_DIM:2 * D_MODEL + (ch + 1) * MXU_DIM])
        qn = _seg_norm(qf, ones_bd, A_HEAD_DIM) * qg_ref[:, cs]
        kn = _seg_norm(kf, ones_bd, A_HEAD_DIM) * kg_ref[:, cs]
        heads_per_chunk = MXU_DIM // A_HEAD_DIM
        hs = slice(ch * heads_per_chunk * LANES, (ch + 1) * heads_per_chunk * LANES)
        exq = _dot(e, selq_ref[:, hs])
        exk = _dot(e, selk_ref[:, hs])
        for j in range(heads_per_chunk):
            pair = slice((j // 2) * LANES, (j // 2 + 1) * LANES)
            blk = slice(j * LANES, (j + 1) * LANES)
            out = slice((ch * heads_per_chunk + j) * LANES, (ch * heads_per_chunk + j + 1) * LANES)
            qp, kp, vp = qn[:, pair], kn[:, pair], vf[:, pair]
            if j % 2 == 1:
                qp = pltpu.roll(qp, A_HEAD_DIM, 1)
                kp = pltpu.roll(kp, A_HEAD_DIM, 1)
                vp = pltpu.roll(vp, A_HEAD_DIM, 1)
            q_ref[:, out] = jnp.where(low, qp, exq[:, blk]).astype(BF16)
            k_ref[:, out] = jnp.where(low, kp, exk[:, blk]).astype(BF16)
            v_aug = jnp.where(low, vp, v_extra)
            for cc in range(tm // tkv):
                vt_ref[cc, ch * heads_per_chunk + j] = (
                    v_aug[cc * tkv:(cc + 1) * tkv, :].T.astype(BF16))


def _a_sel_matrices():
    selq = np.zeros((LANES, A_HEADS * LANES), np.float32)
    selk = np.zeros((LANES, A_HEADS * LANES), np.float32)
    ones_lane = 6 * A_HEADS
    for h in range(A_HEADS):
        base = h * LANES + X_BASE
        for piece in range(3):
            selq[A_HEADS * piece + h, base + piece] = 1.0
            selq[ones_lane, base + 3 + piece] = 1.0
            selk[ones_lane, base + piece] = 1.0
            selk[A_HEADS * (3 + piece) + h, base + 3 + piece] = -1.0
    return jnp.asarray(selq, BF16), jnp.asarray(selk, BF16)


def _block_diag_ones(seg):
    idx = np.arange(MXU_DIM) // seg
    return jnp.asarray((idx[:, None] == idx[None, :]).astype(np.float32), BF16)


def _a_proj(h2d, seq, norm_gain, w_in, b_f, q_gain, k_gain, shift):
    t = h2d.shape[0]
    tm = 512
    wide = A_HEADS * LANES
    w_qkv = w_in[:, :3 * D_MODEL].astype(BF16)
    w_f = jnp.pad(w_in[:, 3 * D_MODEL:], ((0, 0), (0, LANES - A_HEADS))).astype(BF16)
    b_row = jnp.pad(b_f, (0, LANES - A_HEADS)).reshape(1, LANES)
    qg = (jnp.tile(q_gain, A_HEADS) * (A_HEAD_DIM ** -0.5 * LOG2E)).reshape(1, D_MODEL)
    kg = jnp.tile(k_gain, A_HEADS).reshape(1, D_MODEL)
    tri = jnp.asarray(np.tril(np.ones((tm, tm), np.float32)), BF16)
    selq, selk = _a_sel_matrices()
    shift_row = jnp.full((1, LANES), LOG2E, F32) * shift
    full = lambda shape: pl.BlockSpec(shape, lambda i: (0,) * len(shape))
    out_spec = pl.BlockSpec((tm, wide), lambda i: (i, 0))
    per_tile = tm // FOX_BLOCK
    return pl.pallas_call(
        functools.partial(_a_proj_kernel, seq // tm),
        grid=(t // tm,),
        in_specs=[pl.BlockSpec((tm, D_MODEL), lambda i: (i, 0)),
                  full((1, D_MODEL)), full((D_MODEL, 3 * D_MODEL)), full((D_MODEL, LANES)),
                  full((1, LANES)), full((1, D_MODEL)), full((1, D_MODEL)),
                  full((MXU_DIM, MXU_DIM)), full((tm, tm)), full((LANES, wide)), full((LANES, wide)),
                  full((1, LANES))],
        out_specs=[out_spec, out_spec,
                   pl.BlockSpec((per_tile, A_HEADS, LANES, FOX_BLOCK), lambda i: (i, 0, 0, 0))],
        out_shape=[jax.ShapeDtypeStruct((t, wide), BF16), jax.ShapeDtypeStruct((t, wide), BF16),
                   jax.ShapeDtypeStruct((t // FOX_BLOCK, A_HEADS, LANES, FOX_BLOCK), BF16)],
        scratch_shapes=[pltpu.VMEM((8, LANES), F32)],
        compiler_params=_params(("arbitrary",)),
        name="a_proj",
    )(h2d, norm_gain.reshape(1, D_MODEL), w_qkv, w_f, b_row, qg, kg,
      _block_diag_ones(A_HEAD_DIM), tri, selq, selk, shift_row)


def _fox_pair_output(accs_t, o_ref, rows):
    outs = [(a / a[X_BASE:X_BASE + 1, :]).T for a in accs_t]
    lane = lax.broadcasted_iota(jnp.int32, outs[0].shape, 1)
    o_ref[rows, :] = jnp.where(lane < A_HEAD_DIM, outs[0],
                               pltpu.roll(outs[1], A_HEAD_DIM, 1)).astype(o_ref.dtype)


def _fox_shifted_kernel(q_ref, k_ref, vt_ref, o_ref, acc_ref):
    blk = FOX_BLOCK
    row = lax.broadcasted_iota(jnp.int32, (blk, blk), 0)
    col = lax.broadcasted_iota(jnp.int32, (blk, blk), 1)
    key_visible = row <= col

    def q_tile(i, _):
        rows = pl.ds(pl.multiple_of(i * blk, blk), blk)
        qs = [q_ref[rows, hh * LANES:(hh + 1) * LANES] for hh in range(2)]

        def block(j, masked):
            keys = pl.ds(pl.multiple_of(j * blk, blk), blk)
            ss = [_dot_nt(k_ref[keys, hh * LANES:(hh + 1) * LANES], qs[hh]) for hh in range(2)]
            for hh in range(2):
                p = jnp.exp2(ss[hh])
                if masked:
                    p = jnp.where(key_visible, p, 0.0)
                acc_ref[hh] += _dot(vt_ref[j, hh], p.astype(BF16))

        acc_ref[...] = jnp.zeros_like(acc_ref)

        @pl.loop(0, i // FOX_UNROLL)
        def _(t):
            for u in range(FOX_UNROLL):
                block(FOX_UNROLL * t + u, False)

        @pl.loop((i // FOX_UNROLL) * FOX_UNROLL, i)
        def _(j):
            block(j, False)

        block(i, True)
        _fox_pair_output((acc_ref[0], acc_ref[1]), o_ref, rows)
        return 0

    lax.fori_loop(0, q_ref.shape[0] // blk, q_tile, 0)


def _fox_online_kernel(q_ref, k_ref, vt_ref, o_ref):
    blk = FOX_BLOCK
    qi = pl.program_id(2)
    qs = [q_ref[:, hh * LANES:(hh + 1) * LANES] for hh in range(2)]
    row = lax.broadcasted_iota(jnp.int32, (blk, blk), 0)
    col = lax.broadcasted_iota(jnp.int32, (blk, blk), 1)
    key_visible = row <= col

    def block(j, carry, masked):
        keys = pl.ds(pl.multiple_of(j * blk, blk), blk)
        new = []
        for hh in range(2):
            m, acc = carry[2 * hh], carry[2 * hh + 1]
            s = _dot_nt(k_ref[keys, hh * LANES:(hh + 1) * LANES], qs[hh])
            if masked:
                s = jnp.where(key_visible, s, NEG_INF)
            m_new = jnp.maximum(m, jnp.max(s, axis=0, keepdims=True))
            p = jnp.exp2(s - m_new)
            acc = jnp.exp2(m - m_new) * acc + _dot(vt_ref[j, hh], p.astype(BF16))
            new += [m_new, acc]
        return tuple(new)

    init = (jnp.full((1, blk), NEG_INF, F32), jnp.zeros((LANES, blk), F32)) * 2
    carry = lax.fori_loop(0, qi, functools.partial(block, masked=False), init)
    carry = block(qi, carry, True)
    _fox_pair_output((carry[1], carry[3]), o_ref, slice(None))


def _fox_attention(q, k, vt, batch, seq, fixed_shift):
    pairs = A_HEADS // 2
    blk = FOX_BLOCK
    q3, k3 = (a.reshape(batch, seq, A_HEADS * LANES) for a in (q, k))
    vt5 = vt.reshape(batch, seq // blk, A_HEADS, LANES, blk)
    out_shape = jax.ShapeDtypeStruct((batch, seq, D_MODEL), BF16)

    def shifted(q3, k3, vt5):
        pair = pl.BlockSpec((None, seq, 2 * LANES), lambda b, p: (b, 0, p))
        return pl.pallas_call(
            _fox_shifted_kernel,
            grid=(batch, pairs),
            in_specs=[pair, pair,
                      pl.BlockSpec((None, seq // blk, 2, LANES, blk), lambda b, p: (b, 0, p, 0, 0))],
            out_specs=pl.BlockSpec((None, seq, LANES), lambda b, p: (b, 0, p)),
            out_shape=out_shape,
            scratch_shapes=[pltpu.VMEM((2, LANES, blk), F32)],
            compiler_params=_params(("parallel", "parallel")),
            name="fox_shifted",
        )(q3, k3, vt5)

    def online(q3, k3, vt5):
        return pl.pallas_call(
            _fox_online_kernel,
            grid=(batch, pairs, seq // blk),
            in_specs=[pl.BlockSpec((None, blk, 2 * LANES), lambda b, p, i: (b, i, p)),
                      pl.BlockSpec((None, seq, 2 * LANES), lambda b, p, i: (b, 0, p)),
                      pl.BlockSpec((None, seq // blk, 2, LANES, blk), lambda b, p, i: (b, 0, p, 0, 0))],
            out_specs=pl.BlockSpec((None, blk, LANES), lambda b, p, i: (b, i, p)),
            out_shape=out_shape,
            compiler_params=_params(("parallel", "parallel", "arbitrary")),
            name="fox_online",
        )(q3, k3, vt5)

    return shifted(q3, k3, vt5) if fixed_shift else online(q3, k3, vt5)


def _out_proj_kernel(h_ref, o_ref, w_ref, out_ref):
    out_ref[...] = h_ref[...] + _dot(o_ref[...], w_ref[...])


def _out_proj(h2d, o2d, w_out):
    t = h2d.shape[0]
    tm = 1024
    tile = pl.BlockSpec((tm, D_MODEL), lambda i: (i, 0))
    return pl.pallas_call(
        _out_proj_kernel,
        grid=(t // tm,),
        in_specs=[tile, tile, pl.BlockSpec((D_MODEL, D_MODEL), lambda i: (0, 0))],
        out_specs=tile,
        out_shape=jax.ShapeDtypeStruct((t, D_MODEL), F32),
        compiler_params=_params(("parallel",)),
        name="out_proj",
    )(h2d, o2d, w_out.astype(BF16))


def _b_proj_kernel(n_rope_cols, h_ref, g_ref, w_ref, gain_ref, ones_ref, cos_ref, sin_ref, out_ref):
    xn = _rms_rows(h_ref[...], g_ref[...]).astype(BF16)
    cos, sin = cos_ref[...], sin_ref[...]
    ones_bd = ones_ref[...]
    for ch in range(w_ref.shape[1] // MXU_DIM):
        cs = slice(ch * MXU_DIM, (ch + 1) * MXU_DIM)
        y = _dot(xn, w_ref[:, cs])
        if ch * MXU_DIM < n_rope_cols:
            y = _seg_norm(y, ones_bd, B_HEAD_DIM) * gain_ref[:, cs]
            for j in range(MXU_DIM // B_HEAD_DIM):
                blk = slice(j * B_HEAD_DIM, (j + 1) * B_HEAD_DIM)
                out_ref[:, ch * MXU_DIM + j * B_HEAD_DIM:ch * MXU_DIM + (j + 1) * B_HEAD_DIM] = (
                    _rope(y[:, blk], cos, sin).astype(out_ref.dtype))
        else:
            out_ref[:, cs] = y.astype(out_ref.dtype)


def _b_proj(h2d, norm_gain, w, head_gain_row, n_rope_cols, cos, sin):
    t = h2d.shape[0]
    n = w.shape[1]
    tm = 512
    full = lambda shape: pl.BlockSpec(shape, lambda i: (0,) * len(shape))
    return pl.pallas_call(
        functools.partial(_b_proj_kernel, n_rope_cols),
        grid=(t // tm,),
        in_specs=[pl.BlockSpec((tm, D_MODEL), lambda i: (i, 0)), full((1, D_MODEL)),
                  full((D_MODEL, n)), full((1, n)), full((MXU_DIM, MXU_DIM)),
                  pl.BlockSpec((tm, B_HEAD_DIM), lambda i: (i, 0)),
                  pl.BlockSpec((tm, B_HEAD_DIM), lambda i: (i, 0))],
        out_specs=pl.BlockSpec((tm, n), lambda i: (i, 0)),
        out_shape=jax.ShapeDtypeStruct((t, n), BF16),
        compiler_params=_params(("parallel",)),
        name="b_proj",
    )(h2d, norm_gain.reshape(1, D_MODEL), w.astype(BF16), head_gain_row,
      _block_diag_ones(B_HEAD_DIM), cos, sin)


def _dilated_kernel(q_ref, kc_ref, kp_ref, vc_ref, vp_ref, bp_ref, bc_ref, o_ref, lse_ref):
    tl = q_ref.shape[0]
    blk = BAND_BLOCK
    first_pen = jnp.where(pl.program_id(2) == 0, NEG_INF, 0.0)
    bias_prev, bias_cur = bp_ref[...], bc_ref[...]
    lane = lax.broadcasted_iota(jnp.int32, (blk, LANES), 1)
    for i in range(tl // blk):
        rows = slice(i * blk, (i + 1) * blk)
        lse_tile = jnp.zeros((blk, LANES), F32)
        for h in range(B_HEADS):
            hs = slice(h * B_HEAD_DIM, (h + 1) * B_HEAD_DIM)
            q = q_ref[rows, hs]
            if i == 0:
                k_prev, v_prev = kp_ref[:, hs], vp_ref[:, hs]
            else:
                prev_rows = slice((i - 1) * blk, i * blk)
                k_prev, v_prev = kc_ref[prev_rows, hs], vc_ref[prev_rows, hs]
            s_prev = _dot_nt(q, k_prev) + bias_prev
            if i == 0:
                s_prev = s_prev + first_pen
            s_cur = _dot_nt(q, kc_ref[rows, hs]) + bias_cur
            m = jnp.maximum(jnp.max(s_prev, axis=-1, keepdims=True),
                            jnp.max(s_cur, axis=-1, keepdims=True))
            p_prev = jnp.exp(s_prev - m)
            p_cur = jnp.exp(s_cur - m)
            l = jnp.sum(p_prev, axis=-1, keepdims=True) + jnp.sum(p_cur, axis=-1, keepdims=True)
            o = (_dot(p_prev.astype(BF16), v_prev) + _dot(p_cur.astype(BF16), vc_ref[rows, hs])) / l
            o_ref[rows, hs] = o.astype(o_ref.dtype)
            lse_tile = jnp.where(lane == h, m + jnp.log(l), lse_tile)
        lse_ref[rows, :] = lse_tile


def _band_biases():
    qi = np.arange(BAND_BLOCK)[:, None]
    kk = np.arange(BAND_BLOCK)[None, :]
    prev = np.where(kk >= qi, 0.0, -np.inf).astype(np.float32)
    cur = np.where(kk <= qi, 0.0, -np.inf).astype(np.float32)
    return jnp.asarray(prev), jnp.asarray(cur)


def _dilated_group(q_all, kv, group, batch, seq):
    d = B_DILATIONS[group]
    assert B_WINDOWS[group] // d == BAND_BLOCK and seq % (d * BAND_BLOCK) == 0
    length = seq // d
    tl = 256
    width = B_HEADS * B_HEAD_DIM
    sub = tl // BAND_BLOCK
    n_groups = len(B_DILATIONS)
    qv = q_all.reshape(batch, length, d * n_groups * width)
    kvv = kv.reshape(batch, length, d * 2 * width)

    def cur(part):
        return pl.BlockSpec((None, tl, width), lambda b, r, n: (b, n, r * 2 + part))

    def prev(part):
        return pl.BlockSpec((None, BAND_BLOCK, width),
                            lambda b, r, n: (b, jnp.maximum(n * sub - 1, 0), r * 2 + part))

    bias = pl.BlockSpec((BAND_BLOCK, BAND_BLOCK), lambda b, r, n: (0, 0))
    bias_prev, bias_cur = _band_biases()
    o, lse = pl.pallas_call(
        _dilated_kernel,
        grid=(batch, d, length // tl),
        in_specs=[pl.BlockSpec((None, tl, width), lambda b, r, n: (b, n, r * n_groups + group)),
                  cur(0), prev(0), cur(1), prev(1), bias, bias],
        out_specs=[pl.BlockSpec((None, tl, width), lambda b, r, n: (b, n, r)),
                   pl.BlockSpec((None, tl, LANES), lambda b, r, n: (b, n, r))],
        out_shape=[jax.ShapeDtypeStruct((batch, length, d * width), BF16),
                   jax.ShapeDtypeStruct((batch, length, d * LANES), F32)],
        compiler_params=_params(("parallel", "parallel", "arbitrary")),
        name=f"dilated_g{group}",
    )(qv, kvv, kvv, kvv, kvv, bias_prev, bias_cur)
    return o.reshape(batch * seq, width), lse.reshape(batch * seq, LANES)


def _b_out_kernel(h_ref, o0_ref, o1_ref, o2_ref, l0_ref, l1_ref, l2_ref, w_ref, out_ref, merged_ref):
    lses = [l0_ref[...], l1_ref[...], l2_ref[...]]
    top = jnp.maximum(jnp.maximum(lses[0], lses[1]), lses[2])
    es = [jnp.exp(x - top) for x in lses]
    den = es[0] + es[1] + es[2]
    ws = [x / den for x in es]
    o_refs = (o0_ref, o1_ref, o2_ref)
    for h in range(B_HEADS):
        hs = slice(h * B_HEAD_DIM, (h + 1) * B_HEAD_DIM)
        acc = ws[0][:, h:h + 1] * o_refs[0][:, hs].astype(F32)
        for g in (1, 2):
            acc = acc + ws[g][:, h:h + 1] * o_refs[g][:, hs].astype(F32)
        merged_ref[:, hs] = acc.astype(BF16)
    out_ref[...] = h_ref[...] + _dot(merged_ref[...], w_ref[...])


def _b_out(h2d, outs, lses, w_out):
    t = h2d.shape[0]
    tm = 512
    tile = pl.BlockSpec((tm, D_MODEL), lambda i: (i, 0))
    ltile = pl.BlockSpec((tm, LANES), lambda i: (i, 0))
    return pl.pallas_call(
        _b_out_kernel,
        grid=(t // tm,),
        in_specs=[tile, tile, tile, tile, ltile, ltile, ltile,
                  pl.BlockSpec((D_MODEL, D_MODEL), lambda i: (0, 0))],
        out_specs=tile,
        out_shape=jax.ShapeDtypeStruct((t, D_MODEL), F32),
        scratch_shapes=[pltpu.VMEM((tm, D_MODEL), BF16)],
        compiler_params=_params(("parallel",)),
        name="b_out",
    )(h2d, *outs, *lses, w_out.astype(BF16))


def _b_proj_res_kernel(plan, n_out, n_tiles, h_ref, g_ref, w_ref, gain_ref, cos_ref, sin_ref, *rest):
    out_refs, y_ref = rest[:n_out], rest[n_out]
    tm = h_ref.shape[0]
    step = pl.program_id(0)

    def project(slot):
        xn = _rms_rows(h_ref[...], g_ref[...]).astype(BF16)
        for ch in range(len(plan)):
            y = _dot(xn, w_ref[:, ch * MXU_DIM:(ch + 1) * MXU_DIM])
            for j in range(MXU_DIM // LANES):
                y_ref[slot, ch * (MXU_DIM // LANES) + j] = y[:, j * LANES:(j + 1) * LANES]

    def finish(slot):
        cos, sin = cos_ref[...], sin_ref[...]
        for ch, (roped, dests) in enumerate(plan):
            for j in range(MXU_DIM // LANES):
                plane = ch * (MXU_DIM // LANES) + j
                if roped:
                    gain = gain_ref[:, plane * LANES:(plane + 1) * LANES]
                    y_ref[slot, plane] = _rope(_rms_rows(y_ref[slot, plane], gain), cos, sin)
                for oi, d, c0 in dests:
                    for r in range(d):
                        out_refs[oi][r, :, c0 + j * LANES:c0 + (j + 1) * LANES] = (
                            y_ref[slot, plane, pl.ds(r, tm // d, stride=d), :].astype(BF16))

    @pl.when(step == 0)
    def _():
        project(0)

    for parity in range(2):
        @pl.when((step > 0) & (step < n_tiles) & (step % 2 == parity))
        def _():
            project(parity)
            finish(1 - parity)

    @pl.when(step == n_tiles)
    def _():
        finish((n_tiles - 1) % 2)


def _b_proj_res(h2d, batch, seq, norm_gain, w, head_gain_row, plan, outs, cos, sin):
    n = w.shape[1]
    tm = 512
    per_seq = seq // tm
    n_tiles = batch * per_seq
    full = lambda shape: pl.BlockSpec(shape, lambda s: (0,) * len(shape))
    ahead = lambda s: jnp.minimum(s, n_tiles - 1)
    behind = lambda s: jnp.maximum(s - 1, 0)
    return pl.pallas_call(
        functools.partial(_b_proj_res_kernel, plan, len(outs), n_tiles),
        grid=(n_tiles + 1,),
        in_specs=[pl.BlockSpec((tm, D_MODEL), lambda s: (ahead(s), 0)),
                  full((1, D_MODEL)), full((D_MODEL, n)), full((1, n)),
                  pl.BlockSpec((tm, B_HEAD_DIM), lambda s: (behind(s), 0)),
                  pl.BlockSpec((tm, B_HEAD_DIM), lambda s: (behind(s), 0))],
        out_specs=[pl.BlockSpec((None, d, tm // d, width),
                                lambda s: (behind(s) // per_seq, 0, behind(s) % per_seq, 0))
                   for d, width in outs],
        out_shape=[jax.ShapeDtypeStruct((batch, d, seq // d, width), BF16) for d, width in outs],
        scratch_shapes=[pltpu.VMEM((2, n // LANES, tm, LANES), F32)],
        compiler_params=_params(("arbitrary",)),
        name="b_proj_res",
    )(h2d, norm_gain.reshape(1, D_MODEL), w.astype(BF16), head_gain_row, cos, sin)


def _dilated_shifted_kernel(q_ref, kc_ref, kp_ref, vc_ref, vp_ref, bias_ref, bias0_ref,
                            o_ref, l_ref, kbuf, vbuf):
    tl = q_ref.shape[0]
    blk = BAND_BLOCK
    kbuf[0:blk, :] = kp_ref[...]
    kbuf[blk:, :] = kc_ref[...]
    vbuf[0:blk, :] = vp_ref[...]
    vbuf[blk:, :] = vc_ref[...]
    bias = bias_ref[...]
    bias_first = jnp.where(pl.program_id(2) == 0, bias0_ref[...], bias)
    lane = lax.broadcasted_iota(jnp.int32, (blk, LANES), 1)
    for i in range(tl // blk):
        rows = slice(i * blk, (i + 1) * blk)
        keys = slice(i * blk, (i + 2) * blk)
        l_tile = jnp.zeros((blk, LANES), F32)
        for h in range(B_HEADS):
            hs = slice(h * B_HEAD_DIM, (h + 1) * B_HEAD_DIM)
            s = _dot_nt(q_ref[rows, hs], kbuf[keys, hs]) + (bias_first if i == 0 else bias)
            p = jnp.exp2(s)
            o_ref[rows, hs] = _dot(p.astype(BF16), vbuf[keys, hs]).astype(o_ref.dtype)
            l_tile = jnp.where(lane == h, jnp.sum(p, axis=-1, keepdims=True), l_tile)
        l_ref[rows, :] = l_tile


def _band_shift_biases(shift_log2):
    qi = np.arange(BAND_BLOCK)[:, None]
    kk = np.arange(2 * BAND_BLOCK)[None, :]
    band = (kk >= qi) & (kk <= qi + BAND_BLOCK)
    band0 = band & (kk >= BAND_BLOCK)
    neg = jnp.full((BAND_BLOCK, 2 * BAND_BLOCK), NEG_INF, F32)
    return jnp.where(band, -shift_log2, neg), jnp.where(band0, -shift_log2, neg)


def _dilated_shifted(q_g, kv_d, d, batch, seq, biases):
    length = seq // d
    tl = min(1024, length)
    width = B_HEADS * B_HEAD_DIM
    sub = tl // BAND_BLOCK

    def cur(part):
        return pl.BlockSpec((None, None, tl, width), lambda b, r, n: (b, r, n, part))

    def prev(part):
        return pl.BlockSpec((None, None, BAND_BLOCK, width),
                            lambda b, r, n: (b, r, jnp.maximum(n * sub - 1, 0), part))

    bias = pl.BlockSpec((BAND_BLOCK, 2 * BAND_BLOCK), lambda b, r, n: (0, 0))
    return pl.pallas_call(
        _dilated_shifted_kernel,
        grid=(batch, d, length // tl),
        in_specs=[cur(0), cur(0), prev(0), cur(1), prev(1), bias, bias],
        out_specs=[cur(0), pl.BlockSpec((None, None, tl, LANES), lambda b, r, n: (b, r, n, 0))],
        out_shape=[jax.ShapeDtypeStruct((batch, d, length, width), BF16),
                   jax.ShapeDtypeStruct((batch, d, length, LANES), F32)],
        scratch_shapes=[pltpu.VMEM((tl + BAND_BLOCK, width), BF16)] * 2,
        compiler_params=_params(("parallel", "parallel", "arbitrary")),
        name=f"dilated_shifted_d{d}",
    )(q_g, kv_d, kv_d, kv_d, kv_d, *biases)


def _b_out_sum_kernel(dils, h_ref, *rest):
    n = len(dils)
    a_refs, l_refs = rest[:n], rest[n:2 * n]
    w_ref, out_ref, acc_ref, den_ref, merged_ref = rest[2 * n:]
    tm = h_ref.shape[0]
    for g, d in enumerate(dils):
        for r in range(d):
            idx = pl.ds(r, tm // d, stride=d)
            planes = [(den_ref, (idx, slice(None)), l_refs[g][r])]
            planes += [(acc_ref, (h, idx, slice(None)),
                        a_refs[g][r, :, h * B_HEAD_DIM:(h + 1) * B_HEAD_DIM].astype(F32))
                       for h in range(B_HEADS)]
            for ref, at, val in planes:
                ref[at] = val if g == 0 else ref[at] + val
    den = den_ref[...]
    for h in range(B_HEADS):
        hs = slice(h * B_HEAD_DIM, (h + 1) * B_HEAD_DIM)
        merged_ref[:, hs] = (acc_ref[h] / den[:, h:h + 1]).astype(BF16)
    out_ref[...] = h_ref[...] + _dot(merged_ref[...], w_ref[...])


def _b_out_sum(h2d, batch, seq, accs, dens, w_out):
    tm = 512
    per_seq = seq // tm
    dils = B_DILATIONS
    tile = pl.BlockSpec((tm, D_MODEL), lambda b, i: (b * per_seq + i, 0))
    res = lambda d, width: pl.BlockSpec((None, d, tm // d, width), lambda b, i: (b, 0, i, 0))
    return pl.pallas_call(
        functools.partial(_b_out_sum_kernel, dils),
        grid=(batch, per_seq),
        in_specs=([tile] + [res(d, D_MODEL) for d in dils] + [res(d, LANES) for d in dils]
                  + [pl.BlockSpec((D_MODEL, D_MODEL), lambda b, i: (0, 0))]),
        out_specs=tile,
        out_shape=jax.ShapeDtypeStruct(h2d.shape, F32),
        scratch_shapes=[pltpu.VMEM((B_HEADS, tm, B_HEAD_DIM), F32), pltpu.VMEM((tm, LANES), F32),
                        pltpu.VMEM((tm, D_MODEL), BF16)],
        compiler_params=_params(("parallel", "parallel")),
        name="b_out_sum",
    )(h2d, *accs, *dens, w_out.astype(BF16))


def _route(logits):
    tm = logits.shape[0]
    lane = lax.broadcasted_iota(jnp.int32, (tm, LANES), 1)
    lanef = lane.astype(F32)
    far = float(LANES)
    is_g = lane < MOE_GROUPS
    g_max = jnp.max(jnp.where(is_g, logits, NEG_INF), axis=-1, keepdims=True)
    g_sum = jnp.sum(jnp.where(is_g, jnp.exp(logits - g_max), 0.0), axis=-1, keepdims=True)
    g_idx = jnp.min(jnp.where(is_g, jnp.where(logits == g_max, lanef, far), far), axis=-1, keepdims=True)
    per_group = MOE_EXPERTS // MOE_GROUPS
    e_lo = ROUTE_E0 + g_idx * per_group
    in_group = jnp.where(lanef >= e_lo, jnp.where(lanef < e_lo + per_group, 1.0, 0.0), 0.0)
    cand1 = jnp.where(in_group > 0.0, logits, NEG_INF)
    v1 = jnp.max(cand1, axis=-1, keepdims=True)
    i1 = jnp.min(jnp.where(cand1 == v1, lanef, far), axis=-1, keepdims=True)
    cand2 = jnp.where(lanef == i1, NEG_INF, cand1)
    v2 = jnp.max(cand2, axis=-1, keepdims=True)
    i2 = jnp.min(jnp.where(cand2 == v2, lanef, far), axis=-1, keepdims=True)
    e21 = jnp.exp(v2 - v1)
    w1 = 1.0 / (1.0 + e21)
    w2 = e21 / (1.0 + e21)
    comb = (jnp.where(lanef == i1, w1, 0.0) + jnp.where(lanef == i2, w2, 0.0)) / g_sum
    return comb, jnp.where(lanef == g_idx, 1.0, 0.0)


def _moe_kernel(h_ref, g_ref, wr12_ref, wr1_ref, br_ref, tri_ref, wgu_ref, wd_ref, out_ref,
                xn_ref, c1_ref, c2_ref, posc_ref, posr_ref, hid_ref, meta_ref):
    grp = pl.program_id(1)
    per_group = MOE_EXPERTS // MOE_GROUPS
    tm = h_ref.shape[0]
    ch = MOE_CHUNK

    @pl.when(grp == 0)
    def _():
        xn = _rms_rows(h_ref[...], g_ref[...])
        x1 = xn.astype(BF16)
        x2 = (xn - x1.astype(F32)).astype(BF16)
        first = _dot(x1, wr12_ref[...])
        logits = first[:, :LANES] + (first[:, LANES:] + _dot(x2, wr1_ref[...])) + br_ref[...]
        comb, onehot = _route(logits)
        c1 = comb.astype(BF16)
        c1_ref[...] = c1
        c2_ref[...] = (comb - c1.astype(F32)).astype(BF16)
        xn_ref[...] = x1
        count = _dot(tri_ref[...], onehot.astype(BF16))
        padded = jnp.floor((count[tm - 1:tm, :] + (ch - 1)) * (1.0 / ch)) * ch
        lane1 = lax.broadcasted_iota(jnp.int32, (1, LANES), 1)
        prev = jnp.where(lane1 >= 1, pltpu.roll(padded, 1, 1), 0.0)
        start = (prev + jnp.where(lane1 >= 2, pltpu.roll(prev, 1, 1), 0.0)
                 + jnp.where(lane1 >= 3, pltpu.roll(prev, 2, 1), 0.0))
        slot = jnp.sum(onehot * (start + count - 1.0), axis=-1, keepdims=True)
        slot_b = jnp.broadcast_to(slot, (tm, LANES))
        posc_ref[...] = slot_b
        posr_ref[...] = slot_b.T[0:8, :]
        for g in range(MOE_GROUPS):
            pick = lane1 == g
            meta_ref[g] = jnp.sum(jnp.where(pick, padded, 0.0)).astype(jnp.int32) // ch
            meta_ref[MOE_GROUPS + g] = jnp.sum(jnp.where(pick, start, 0.0)).astype(jnp.int32)
        out_ref[...] = h_ref[...]

    lane = lax.broadcasted_iota(jnp.int32, (ch, LANES), 1)
    base = meta_ref[MOE_GROUPS + grp]

    @pl.loop(0, meta_ref[grp])
    def _(c):
        first_slot = (base + c * ch).astype(F32)
        rows = lax.broadcasted_iota(jnp.int32, (ch, tm), 0).astype(F32) + first_slot
        gather = jnp.where(posr_ref[0:1, :] == rows, 1.0, 0.0).astype(BF16)
        xs = _dot(gather, xn_ref[...]).astype(BF16)
        cs = _dot(gather, c1_ref[...]) + _dot(gather, c2_ref[...])
        for e in range(per_group):
            gate_w = jnp.sum(jnp.where(lane == grp * per_group + (ROUTE_E0 + e), cs, 0.0),
                             axis=-1, keepdims=True)
            gu = _dot(xs, wgu_ref[e])
            g, u = gu[:, :MOE_FF], gu[:, MOE_FF:]
            hid_ref[:, e * MOE_FF:(e + 1) * MOE_FF] = (
                (g / (1.0 + jnp.exp(-g))) * u * gate_w).astype(BF16)
        y = _dot(hid_ref[...], wd_ref[...]).astype(BF16)
        cols = lax.broadcasted_iota(jnp.int32, (tm, ch), 1).astype(F32) + first_slot
        scatter = jnp.where(posc_ref[:, 0:1] == cols, 1.0, 0.0).astype(BF16)
        out_ref[...] += _dot(scatter, y)


def _moe_weights(group_w, group_b, expert_w, expert_b, w_gate, w_up, w_down):
    pad = LANES - MOE_GROUPS - MOE_EXPERTS
    w_r = jnp.pad(jnp.concatenate([group_w, expert_w], axis=-1), ((0, 0), (0, 0), (0, pad)))
    r1 = w_r.astype(BF16)
    r2 = (w_r - r1.astype(F32)).astype(BF16)
    b_r = jnp.pad(jnp.concatenate([group_b, expert_b], axis=-1), ((0, 0), (0, pad)))[:, None, :]
    layers = w_gate.shape[0]
    per_group = MOE_EXPERTS // MOE_GROUPS
    w_gu = jnp.concatenate([w_gate, w_up], axis=-1).astype(BF16).reshape(
        layers, MOE_GROUPS, per_group, D_MODEL, 2 * MOE_FF)
    w_d = w_down.astype(BF16).reshape(layers, MOE_GROUPS, per_group * MOE_FF, D_MODEL)
    return jnp.concatenate([r1, r2], axis=-1), r1, b_r, w_gu, w_d


def _moe(h2d, norm_gain, layer, weights):
    wr12, wr1, b_r, w_gu, w_d = weights
    t = h2d.shape[0]
    tm = 1024
    per_group = MOE_EXPERTS // MOE_GROUPS
    tile = pl.BlockSpec((tm, D_MODEL), lambda i, g: (i, 0))
    of_layer = lambda shape: pl.BlockSpec((None,) + shape, lambda i, g: (layer,) + (0,) * len(shape))
    tri = jnp.asarray(np.tril(np.ones((tm, tm), np.float32)), BF16)
    return pl.pallas_call(
        _moe_kernel,
        grid=(t // tm, MOE_GROUPS),
        in_specs=[tile, pl.BlockSpec((1, D_MODEL), lambda i, g: (0, 0)),
                  of_layer((D_MODEL, 2 * LANES)), of_layer((D_MODEL, LANES)), of_layer((1, LANES)),
                  pl.BlockSpec((tm, tm), lambda i, g: (0, 0)),
                  pl.BlockSpec((None, None, per_group, D_MODEL, 2 * MOE_FF), lambda i, g: (layer, g, 0, 0, 0)),
                  pl.BlockSpec((None, None, per_group * MOE_FF, D_MODEL), lambda i, g: (layer, g, 0, 0))],
        out_specs=tile,
        out_shape=jax.ShapeDtypeStruct((t, D_MODEL), F32),
        scratch_shapes=[pltpu.VMEM((tm, D_MODEL), BF16), pltpu.VMEM((tm, LANES), BF16),
                        pltpu.VMEM((tm, LANES), BF16), pltpu.VMEM((tm, LANES), F32),
                        pltpu.VMEM((8, tm), F32), pltpu.VMEM((MOE_CHUNK, per_group * MOE_FF), BF16),
                        pltpu.SMEM((2 * MOE_GROUPS,), jnp.int32)],
        compiler_params=_params(("parallel", "arbitrary")),
        name="moe",
    )(h2d, norm_gain.reshape(1, D_MODEL), wr12, wr1, b_r, tri, w_gu, w_d)


def kernel(x, positions, a_norm, a_w_in, a_b_f, a_q_gain, a_k_gain, a_w_out, kv_norm, kv_w, kv_k_gain, b_norm, b_w_q, b_q_gain, b_w_out, ffn_norm, moe_group_w, moe_group_b, moe_expert_w, moe_expert_b, moe_w_gate, moe_w_up, moe_w_down):
    batch, seq, _ = x.shape
    n_a = a_norm.shape[0]
    n_b = b_norm.shape[0]
    b_width = B_HEADS * B_HEAD_DIM
    n_groups = len(B_DILATIONS)
    chunks = b_width // MXU_DIM

    a_shifts = [(A_HEAD_DIM ** 0.5) * jnp.max(jnp.abs(a_q_gain[i])) * jnp.max(jnp.abs(a_k_gain[i]))
                for i in range(n_a)]
    b_shifts = [(B_HEAD_DIM ** 0.5) * jnp.max(jnp.abs(b_q_gain[j])) * jnp.max(jnp.abs(kv_k_gain))
                for j in range(n_b)]

    moe_weights = _moe_weights(moe_group_w, moe_group_b, moe_expert_w, moe_expert_b,
                               moe_w_gate, moe_w_up, moe_w_down)

    def trunk(fixed_shift, x):
        h = x.reshape(batch * seq, D_MODEL)
        kv_sh = cos = sin = None
        for layer in range(n_a + n_b):
            if layer < n_a:
                i = layer
                q, k, vt = _a_proj(h, seq, a_norm[i], a_w_in[i], a_b_f[i], a_q_gain[i], a_k_gain[i],
                                   a_shifts[i])
                o = _fox_attention(q, k, vt, batch, seq, fixed_shift)
                h = _out_proj(h, o.reshape(batch * seq, D_MODEL), a_w_out[i])
            else:
                j = layer - n_a
                k_gain_row = jnp.concatenate(
                    [jnp.tile(kv_k_gain, B_HEADS), jnp.ones((b_width,), F32)]).reshape(1, 2 * b_width)
                q_gain_row = (jnp.tile(b_q_gain[j], (1, B_HEADS)) * (B_HEAD_DIM ** -0.5)).reshape(
                    1, n_groups * b_width)
                if j == 0:
                    cos, sin = _rope_tables(positions)
                if fixed_shift:
                    if j == 0:
                        kv_plan = tuple((ch < chunks, tuple((g, d, ch * MXU_DIM) for g, d in enumerate(B_DILATIONS)))
                                        for ch in range(2 * chunks))
                        kv_sh = _b_proj_res(h, batch, seq, kv_norm, kv_w, k_gain_row, kv_plan,
                                            [(d, 2 * b_width) for d in B_DILATIONS], cos, sin)
                    q_plan = tuple((True, ((ch // chunks, B_DILATIONS[ch // chunks], (ch % chunks) * MXU_DIM),))
                                   for ch in range(n_groups * chunks))
                    q_res = _b_proj_res(h, batch, seq, b_norm[j], b_w_q[j], q_gain_row * LOG2E, q_plan,
                                        [(d, b_width) for d in B_DILATIONS], cos, sin)
                    biases = _band_shift_biases(b_shifts[j] * LOG2E)
                    accs, dens = zip(*[_dilated_shifted(q_res[g], kv_sh[g], d, batch, seq, biases)
                                       for g, d in enumerate(B_DILATIONS)])
                    h = _b_out_sum(h, batch, seq, accs, dens, b_w_out[j])
                else:
                    if j == 0:
                        kv_sh = _b_proj(h, kv_norm, kv_w, k_gain_row, b_width, cos, sin)
                    q_all = _b_proj(h, b_norm[j], b_w_q[j], q_gain_row, n_groups * b_width, cos, sin)
                    outs, lses = zip(*[_dilated_group(q_all, kv_sh, g, batch, seq) for g in range(n_groups)])
                    h = _b_out(h, outs, lses, b_w_out[j])
            h = _moe(h, ffn_norm[layer], layer, moe_weights)
        return h.reshape(batch, seq, D_MODEL)

    largest = functools.reduce(jnp.maximum, a_shifts + b_shifts)
    return lax.cond(largest <= MAX_SAFE_SHIFT, functools.partial(trunk, True),
                    functools.partial(trunk, False), x)
```

```python
import functools
import math

import numpy as np
import jax
import jax.numpy as jnp
from jax import lax
from jax.experimental import pallas as pl
from jax.experimental.pallas import tpu as pltpu

F32 = jnp.float32
BF16 = jnp.bfloat16

D_MODEL = 1024
EPS = 1e-6
ROPE_THETA = 10000.0
A_HEADS = 16
A_HEAD_DIM = 64
B_HEADS = 8
B_HEAD_DIM = 128
B_WINDOWS = (128, 512, 2048)
B_DILATIONS = (1, 4, 16)
BAND_BLOCK = 128
MOE_GROUPS = 4
MOE_EXPERTS = 16
MOE_FF = 256
MOE_CHUNK = 256

LANES = 128
MXU_DIM = 256
VMEM_LIMIT = 56 * 1024 * 1024

NEG_INF = float("-inf")
LOG2E = math.log2(math.e)
FOX_BLOCK = 512
FOX_UNROLL = 4
MAX_SAFE_SHIFT = 40.0

X_BASE = A_HEAD_DIM
ROUTE_E0 = MOE_GROUPS


def _params(sem):
    return pltpu.CompilerParams(dimension_semantics=sem, vmem_limit_bytes=VMEM_LIMIT)


def _rms_rows(x, gain_row):
    ms = jnp.mean(x * x, axis=-1, keepdims=True)
    return x * lax.rsqrt(ms + EPS) * gain_row


def _dot(a, b):
    return jnp.dot(a, b, preferred_element_type=F32)


def _dot_nt(a, b):
    return lax.dot_general(a, b, (((1,), (1,)), ((), ())), preferred_element_type=F32)


def _split3(x):
    p1 = x.astype(BF16)
    r1 = x - p1.astype(F32)
    p2 = r1.astype(BF16)
    p3 = (r1 - p2.astype(F32)).astype(BF16)
    return p1, p2, p3


def _seg_norm(x, ones_bd, seg):
    ss = _dot((x * x).astype(BF16), ones_bd)
    return x * lax.rsqrt(ss * (1.0 / seg) + EPS)


def _rope_table_kernel(pos_ref, inv_ref, sign_ref, cos_ref, sin_ref):
    ang = pos_ref[...].astype(F32) * inv_ref[...]
    cos_ref[...] = jnp.cos(ang)
    sin_ref[...] = jnp.sin(ang) * sign_ref[...]


def _rope_tables(positions):
    t = positions.size
    tm = 1024
    half = B_HEAD_DIM // 2
    inv = ROPE_THETA ** (-jnp.arange(half, dtype=F32) / half)
    inv = jnp.concatenate([inv, inv]).reshape(1, B_HEAD_DIM)
    sign = jnp.concatenate([-jnp.ones((half,), F32), jnp.ones((half,), F32)]).reshape(1, B_HEAD_DIM)
    row = pl.BlockSpec((1, B_HEAD_DIM), lambda i: (0, 0))
    tab = pl.BlockSpec((tm, B_HEAD_DIM), lambda i: (i, 0))
    return pl.pallas_call(
        _rope_table_kernel,
        grid=(t // tm,),
        in_specs=[pl.BlockSpec((tm, 1), lambda i: (i, 0)), row, row],
        out_specs=[tab, tab],
        out_shape=[jax.ShapeDtypeStruct((t, B_HEAD_DIM), F32)] * 2,
        compiler_params=_params(("parallel",)),
        name="rope_tables",
    )(positions.reshape(t, 1), inv, sign)


def _rope(x, cos, sin_signed):
    return x * cos + pltpu.roll(x, B_HEAD_DIM // 2, 1) * sin_signed


def _a_proj_kernel(tiles_per_seq, h_ref, g_ref, wqkv_ref, wf_ref, bf_ref, qg_ref, kg_ref,
                   ones_ref, tri_ref, selq_ref, selk_ref, shift_ref, q_ref, k_ref, vt_ref, carry_ref):
    tm = h_ref.shape[0]

    @pl.when(pl.program_id(0) % tiles_per_seq == 0)
    def _():
        carry_ref[...] = jnp.zeros_like(carry_ref)

    xn = _rms_rows(h_ref[...], g_ref[...]).astype(BF16)

    fl = _dot(xn, wf_ref[...]) + bf_ref[...]
    lf = jnp.minimum(fl, 0.0) - jnp.log(1.0 + jnp.exp(-jnp.abs(fl)))
    tri = tri_ref[...]
    l1, l2, l3 = _split3(lf)
    c = _dot(tri, l1) + _dot(tri, l2) + _dot(tri, l3) + carry_ref[0:1, :]
    carry_ref[0:1, :] = c[tm - 1:tm, :]

    lane = lax.broadcasted_iota(jnp.int32, (tm, LANES), 1)
    ck = c * LOG2E
    cq = ck - shift_ref[...]
    pieces = [p.astype(F32) for p in _split3(cq) + _split3(ck)]
    e = jnp.where(lane == 6 * A_HEADS, 1.0, 0.0)
    for n in reversed(range(6)):
        piece = pieces[n] if n == 0 else pltpu.roll(pieces[n], n * A_HEADS, 1)
        e = jnp.where((lane >= n * A_HEADS) & (lane < (n + 1) * A_HEADS), piece, e)
    e = e.astype(BF16)
    v_extra = jnp.where(lane == X_BASE, 1.0, 0.0)
    low = lane < A_HEAD_DIM
    tkv = vt_ref.shape[-1]

    ones_bd = ones_ref[...]
    n_chunks = D_MODEL // MXU_DIM
    for ch in range(n_chunks):
        cs = slice(ch * MXU_DIM, (ch + 1) * MXU_DIM)
        qf = _dot(xn, wqkv_ref[:, cs])
        kf = _dot(xn, wqkv_ref[:, D_MODEL + ch * MXU_DIM:D_MODEL + (ch + 1) * MXU_DIM])
        vf = _dot(xn, wqkv_ref[:, 2 * D_MODEL + ch * MXU_DIM:2 * D_MODEL + (ch + 1) * MXU_DIM])
        qn = _seg_norm(qf, ones_bd, A_HEAD_DIM) * qg_ref[:, cs]
        kn = _seg_norm(kf, ones_bd, A_HEAD_DIM) * kg_ref[:, cs]
        heads_per_chunk = MXU_DIM // A_HEAD_DIM
        hs = slice(ch * heads_per_chunk * LANES, (ch + 1) * heads_per_chunk * LANES)
        exq = _dot(e, selq_ref[:, hs])
        exk = _dot(e, selk_ref[:, hs])
        for j in range(heads_per_chunk):
            pair = slice((j // 2) * LANES, (j // 2 + 1) * LANES)
            blk = slice(j * LANES, (j + 1) * LANES)
            out = slice((ch * heads_per_chunk + j) * LANES, (ch * heads_per_chunk + j + 1) * LANES)
            qp, kp, vp = qn[:, pair], kn[:, pair], vf[:, pair]
            if j % 2 == 1:
                qp = pltpu.roll(qp, A_HEAD_DIM, 1)
                kp = pltpu.roll(kp, A_HEAD_DIM, 1)
                vp = pltpu.roll(vp, A_HEAD_DIM, 1)
            q_ref[:, out] = jnp.where(low, qp, exq[:, blk]).astype(BF16)
            k_ref[:, out] = jnp.where(low, kp, exk[:, blk]).astype(BF16)
            v_aug = jnp.where(low, vp, v_extra)
            for cc in range(tm // tkv):
                vt_ref[cc, ch * heads_per_chunk + j] = (
                    v_aug[cc * tkv:(cc + 1) * tkv, :].T.astype(BF16))


def _a_sel_matrices():
    selq = np.zeros((LANES, A_HEADS * LANES), np.float32)
    selk = np.zeros((LANES, A_HEADS * LANES), np.float32)
    ones_lane = 6 * A_HEADS
    for h in range(A_HEADS):
        base = h * LANES + X_BASE
        for piece in range(3):
            selq[A_HEADS * piece + h, base + piece] = 1.0
            selq[ones_lane, base + 3 + piece] = 1.0
            selk[ones_lane, base + piece] = 1.0
            selk[A_HEADS * (3 + piece) + h, base + 3 + piece] = -1.0
    return jnp.asarray(selq, BF16), jnp.asarray(selk, BF16)


def _block_diag_ones(seg):
    idx = np.arange(MXU_DIM) // seg
    return jnp.asarray((idx[:, None] == idx[None, :]).astype(np.float32), BF16)


def _a_proj(h2d, seq, norm_gain, w_in, b_f, q_gain, k_gain, shift):
    t = h2d.shape[0]
    tm = 512
    wide = A_HEADS * LANES
    w_qkv = w_in[:, :3 * D_MODEL].astype(BF16)
    w_f = jnp.pad(w_in[:, 3 * D_MODEL:], ((0, 0), (0, LANES - A_HEADS))).astype(BF16)
    b_row = jnp.pad(b_f, (0, LANES - A_HEADS)).reshape(1, LANES)
    qg = (jnp.tile(q_gain, A_HEADS) * (A_HEAD_DIM ** -0.5 * LOG2E)).reshape(1, D_MODEL)
    kg = jnp.tile(k_gain, A_HEADS).reshape(1, D_MODEL)
    tri = jnp.asarray(np.tril(np.ones((tm, tm), np.float32)), BF16)
    selq, selk = _a_sel_matrices()
    shift_row = jnp.full((1, LANES), LOG2E, F32) * shift
    full = lambda shape: pl.BlockSpec(shape, lambda i: (0,) * len(shape))
    out_spec = pl.BlockSpec((tm, wide), lambda i: (i, 0))
    per_tile = tm // FOX_BLOCK
    return pl.pallas_call(
        functools.partial(_a_proj_kernel, seq // tm),
        grid=(t // tm,),
        in_specs=[pl.BlockSpec((tm, D_MODEL), lambda i: (i, 0)),
                  full((1, D_MODEL)), full((D_MODEL, 3 * D_MODEL)), full((D_MODEL, LANES)),
                  full((1, LANES)), full((1, D_MODEL)), full((1, D_MODEL)),
                  full((MXU_DIM, MXU_DIM)), full((tm, tm)), full((LANES, wide)), full((LANES, wide)),
                  full((1, LANES))],
        out_specs=[out_spec, out_spec,
                   pl.BlockSpec((per_tile, A_HEADS, LANES, FOX_BLOCK), lambda i: (i, 0, 0, 0))],
        out_shape=[jax.ShapeDtypeStruct((t, wide), BF16), jax.ShapeDtypeStruct((t, wide), BF16),
                   jax.ShapeDtypeStruct((t // FOX_BLOCK, A_HEADS, LANES, FOX_BLOCK), BF16)],
        scratch_shapes=[pltpu.VMEM((8, LANES), F32)],
        compiler_params=_params(("arbitrary",)),
        name="a_proj",
    )(h2d, norm_gain.reshape(1, D_MODEL), w_qkv, w_f, b_row, qg, kg,
      _block_diag_ones(A_HEAD_DIM), tri, selq, selk, shift_row)


def _fox_pair_output(accs_t, o_ref, rows):
    outs = [(a / a[X_BASE:X_BASE + 1, :]).T for a in accs_t]
    lane = lax.broadcasted_iota(jnp.int32, outs[0].shape, 1)
    o_ref[rows, :] = jnp.where(lane < A_HEAD_DIM, outs[0],
                               pltpu.roll(outs[1], A_HEAD_DIM, 1)).astype(o_ref.dtype)


def _fox_shifted_kernel(q_ref, k_ref, vt_ref, o_ref, acc_ref):
    blk = FOX_BLOCK
    row = lax.broadcasted_iota(jnp.int32, (blk, blk), 0)
    col = lax.broadcasted_iota(jnp.int32, (blk, blk), 1)
    key_visible = row <= col

    def q_tile(i, _):
        rows = pl.ds(pl.multiple_of(i * blk, blk), blk)
        qs = [q_ref[rows, hh * LANES:(hh + 1) * LANES] for hh in range(2)]

        def block(j, masked=False):
            keys = pl.ds(pl.multiple_of(j * blk, blk), blk)
            ss = [_dot_nt(k_ref[keys, hh * LANES:(hh + 1) * LANES], qs[hh]) for hh in range(2)]
            for hh in range(2):
                p = jnp.exp2(ss[hh])
                if masked:
                    p = jnp.where(key_visible, p, 0.0)
                acc_ref[hh] += _dot(vt_ref[j, hh], p.astype(BF16))

        acc_ref[...] = jnp.zeros_like(acc_ref)

        @pl.loop(0, i // FOX_UNROLL)
        def _(t):
            for u in range(FOX_UNROLL):
                block(FOX_UNROLL * t + u)

        @pl.loop((i // FOX_UNROLL) * FOX_UNROLL, i)
        def _(j):
            block(j)

        block(i, masked=True)
        _fox_pair_output((acc_ref[0], acc_ref[1]), o_ref, rows)
        return 0

    lax.fori_loop(0, q_ref.shape[0] // blk, q_tile, 0)


def _fox_online_kernel(q_ref, k_ref, vt_ref, o_ref):
    blk = FOX_BLOCK
    qi = pl.program_id(2)
    qs = [q_ref[:, hh * LANES:(hh + 1) * LANES] for hh in range(2)]
    row = lax.broadcasted_iota(jnp.int32, (blk, blk), 0)
    col = lax.broadcasted_iota(jnp.int32, (blk, blk), 1)
    key_visible = row <= col

    def block(j, carry, masked):
        keys = pl.ds(pl.multiple_of(j * blk, blk), blk)
        new = []
        for hh in range(2):
            m, acc = carry[2 * hh], carry[2 * hh + 1]
            s = _dot_nt(k_ref[keys, hh * LANES:(hh + 1) * LANES], qs[hh])
            if masked:
                s = jnp.where(key_visible, s, NEG_INF)
            m_new = jnp.maximum(m, jnp.max(s, axis=0, keepdims=True))
            p = jnp.exp2(s - m_new)
            acc = jnp.exp2(m - m_new) * acc + _dot(vt_ref[j, hh], p.astype(BF16))
            new += [m_new, acc]
        return tuple(new)

    init = (jnp.full((1, blk), NEG_INF, F32), jnp.zeros((LANES, blk), F32)) * 2
    carry = lax.fori_loop(0, qi, functools.partial(block, masked=False), init)
    carry = block(qi, carry, True)
    _fox_pair_output((carry[1], carry[3]), o_ref, slice(None))


def _fox_attention(q, k, vt, batch, seq, fixed_shift):
    pairs = A_HEADS // 2
    blk = FOX_BLOCK
    q3, k3 = (a.reshape(batch, seq, A_HEADS * LANES) for a in (q, k))
    vt5 = vt.reshape(batch, seq // blk, A_HEADS, LANES, blk)
    out_shape = jax.ShapeDtypeStruct((batch, seq, D_MODEL), BF16)

    def shifted(q3, k3, vt5):
        pair = pl.BlockSpec((None, seq, 2 * LANES), lambda b, p: (b, 0, p))
        return pl.pallas_call(
            _fox_shifted_kernel,
            grid=(batch, pairs),
            in_specs=[pair, pair,
                      pl.BlockSpec((None, seq // blk, 2, LANES, blk), lambda b, p: (b, 0, p, 0, 0))],
            out_specs=pl.BlockSpec((None, seq, LANES), lambda b, p: (b, 0, p)),
            out_shape=out_shape,
            scratch_shapes=[pltpu.VMEM((2, LANES, blk), F32)],
            compiler_params=_params(("parallel", "parallel")),
            name="fox_shifted",
        )(q3, k3, vt5)

    def online(q3, k3, vt5):
        return pl.pallas_call(
            _fox_online_kernel,
            grid=(batch, pairs, seq // blk),
            in_specs=[pl.BlockSpec((None, blk, 2 * LANES), lambda b, p, i: (b, i, p)),
                      pl.BlockSpec((None, seq, 2 * LANES), lambda b, p, i: (b, 0, p)),
                      pl.BlockSpec((None, seq // blk, 2, LANES, blk), lambda b, p, i: (b, 0, p, 0, 0))],
            out_specs=pl.BlockSpec((None, blk, LANES), lambda b, p, i: (b, i, p)),
            out_shape=out_shape,
            compiler_params=_params(("parallel", "parallel", "arbitrary")),
            name="fox_online",
        )(q3, k3, vt5)

    return shifted(q3, k3, vt5) if fixed_shift else online(q3, k3, vt5)


def _out_proj_kernel(h_ref, o_ref, w_ref, out_ref):
    out_ref[...] = h_ref[...] + _dot(o_ref[...], w_ref[...])


def _out_proj(h2d, o2d, w_out):
    t = h2d.shape[0]
    tm = 1024
    tile = pl.BlockSpec((tm, D_MODEL), lambda i: (i, 0))
    return pl.pallas_call(
        _out_proj_kernel,
        grid=(t // tm,),
        in_specs=[tile, tile, pl.BlockSpec((D_MODEL, D_MODEL), lambda i: (0, 0))],
        out_specs=tile,
        out_shape=jax.ShapeDtypeStruct((t, D_MODEL), F32),
        compiler_params=_params(("parallel",)),
        name="out_proj",
    )(h2d, o2d, w_out.astype(BF16))


def _b_proj_kernel(n_rope_cols, h_ref, g_ref, w_ref, gain_ref, ones_ref, cos_ref, sin_ref, out_ref):
    xn = _rms_rows(h_ref[...], g_ref[...]).astype(BF16)
    cos, sin = cos_ref[...], sin_ref[...]
    ones_bd = ones_ref[...]
    for ch in range(w_ref.shape[1] // MXU_DIM):
        cs = slice(ch * MXU_DIM, (ch + 1) * MXU_DIM)
        y = _dot(xn, w_ref[:, cs])
        if ch * MXU_DIM < n_rope_cols:
            y = _seg_norm(y, ones_bd, B_HEAD_DIM) * gain_ref[:, cs]
            for j in range(MXU_DIM // B_HEAD_DIM):
                blk = slice(j * B_HEAD_DIM, (j + 1) * B_HEAD_DIM)
                out_ref[:, ch * MXU_DIM + j * B_HEAD_DIM:ch * MXU_DIM + (j + 1) * B_HEAD_DIM] = (
                    _rope(y[:, blk], cos, sin).astype(out_ref.dtype))
        else:
            out_ref[:, cs] = y.astype(out_ref.dtype)


def _b_proj(h2d, norm_gain, w, head_gain_row, n_rope_cols, cos, sin):
    t = h2d.shape[0]
    n = w.shape[1]
    tm = 512
    full = lambda shape: pl.BlockSpec(shape, lambda i: (0,) * len(shape))
    return pl.pallas_call(
        functools.partial(_b_proj_kernel, n_rope_cols),
        grid=(t // tm,),
        in_specs=[pl.BlockSpec((tm, D_MODEL), lambda i: (i, 0)), full((1, D_MODEL)),
                  full((D_MODEL, n)), full((1, n)), full((MXU_DIM, MXU_DIM)),
                  pl.BlockSpec((tm, B_HEAD_DIM), lambda i: (i, 0)),
                  pl.BlockSpec((tm, B_HEAD_DIM), lambda i: (i, 0))],
        out_specs=pl.BlockSpec((tm, n), lambda i: (i, 0)),
        out_shape=jax.ShapeDtypeStruct((t, n), BF16),
        compiler_params=_params(("parallel",)),
        name="b_proj",
    )(h2d, norm_gain.reshape(1, D_MODEL), w.astype(BF16), head_gain_row,
      _block_diag_ones(B_HEAD_DIM), cos, sin)


def _dilated_kernel(q_ref, kc_ref, kp_ref, vc_ref, vp_ref, bp_ref, bc_ref, o_ref, lse_ref):
    tl = q_ref.shape[0]
    blk = BAND_BLOCK
    first_pen = jnp.where(pl.program_id(2) == 0, NEG_INF, 0.0)
    bias_prev, bias_cur = bp_ref[...], bc_ref[...]
    lane = lax.broadcasted_iota(jnp.int32, (blk, LANES), 1)
    for i in range(tl // blk):
        rows = slice(i * blk, (i + 1) * blk)
        lse_tile = jnp.zeros((blk, LANES), F32)
        for h in range(B_HEADS):
            hs = slice(h * B_HEAD_DIM, (h + 1) * B_HEAD_DIM)
            q = q_ref[rows, hs]
            if i == 0:
                k_prev, v_prev = kp_ref[:, hs], vp_ref[:, hs]
            else:
                prev_rows = slice((i - 1) * blk, i * blk)
                k_prev, v_prev = kc_ref[prev_rows, hs], vc_ref[prev_rows, hs]
            s_prev = _dot_nt(q, k_prev) + bias_prev
            if i == 0:
                s_prev = s_prev + first_pen
            s_cur = _dot_nt(q, kc_ref[rows, hs]) + bias_cur
            m = jnp.maximum(jnp.max(s_prev, axis=-1, keepdims=True),
                            jnp.max(s_cur, axis=-1, keepdims=True))
            p_prev = jnp.exp(s_prev - m)
            p_cur = jnp.exp(s_cur - m)
            l = jnp.sum(p_prev, axis=-1, keepdims=True) + jnp.sum(p_cur, axis=-1, keepdims=True)
            o = (_dot(p_prev.astype(BF16), v_prev) + _dot(p_cur.astype(BF16), vc_ref[rows, hs])) / l
            o_ref[rows, hs] = o.astype(o_ref.dtype)
            lse_tile = jnp.where(lane == h, m + jnp.log(l), lse_tile)
        lse_ref[rows, :] = lse_tile


def _band_biases():
    qi = np.arange(BAND_BLOCK)[:, None]
    kk = np.arange(BAND_BLOCK)[None, :]
    prev = np.where(kk >= qi, 0.0, -np.inf).astype(np.float32)
    cur = np.where(kk <= qi, 0.0, -np.inf).astype(np.float32)
    return jnp.asarray(prev), jnp.asarray(cur)


def _dilated_group(q_all, kv, group, batch, seq):
    d = B_DILATIONS[group]
    assert B_WINDOWS[group] // d == BAND_BLOCK and seq % (d * BAND_BLOCK) == 0
    length = seq // d
    tl = 256
    width = B_HEADS * B_HEAD_DIM
    sub = tl // BAND_BLOCK
    n_groups = len(B_DILATIONS)
    qv = q_all.reshape(batch, length, d * n_groups * width)
    kvv = kv.reshape(batch, length, d * 2 * width)

    def cur(part):
        return pl.BlockSpec((None, tl, width), lambda b, r, n: (b, n, r * 2 + part))

    def prev(part):
        return pl.BlockSpec((None, BAND_BLOCK, width),
                            lambda b, r, n: (b, jnp.maximum(n * sub - 1, 0), r * 2 + part))

    bias = pl.BlockSpec((BAND_BLOCK, BAND_BLOCK), lambda b, r, n: (0, 0))
    bias_prev, bias_cur = _band_biases()
    o, lse = pl.pallas_call(
        _dilated_kernel,
        grid=(batch, d, length // tl),
        in_specs=[pl.BlockSpec((None, tl, width), lambda b, r, n: (b, n, r * n_groups + group)),
                  cur(0), prev(0), cur(1), prev(1), bias, bias],
        out_specs=[pl.BlockSpec((None, tl, width), lambda b, r, n: (b, n, r)),
                   pl.BlockSpec((None, tl, LANES), lambda b, r, n: (b, n, r))],
        out_shape=[jax.ShapeDtypeStruct((batch, length, d * width), BF16),
                   jax.ShapeDtypeStruct((batch, length, d * LANES), F32)],
        compiler_params=_params(("parallel", "parallel", "arbitrary")),
        name=f"dilated_g{group}",
    )(qv, kvv, kvv, kvv, kvv, bias_prev, bias_cur)
    return o.reshape(batch * seq, width), lse.reshape(batch * seq, LANES)


def _b_out_kernel(h_ref, o0_ref, o1_ref, o2_ref, l0_ref, l1_ref, l2_ref, w_ref, out_ref, merged_ref):
    lses = [l0_ref[...], l1_ref[...], l2_ref[...]]
    top = jnp.maximum(jnp.maximum(lses[0], lses[1]), lses[2])
    es = [jnp.exp(x - top) for x in lses]
    den = es[0] + es[1] + es[2]
    ws = [x / den for x in es]
    o_refs = (o0_ref, o1_ref, o2_ref)
    for h in range(B_HEADS):
        hs = slice(h * B_HEAD_DIM, (h + 1) * B_HEAD_DIM)
        acc = ws[0][:, h:h + 1] * o_refs[0][:, hs].astype(F32)
        for g in (1, 2):
            acc = acc + ws[g][:, h:h + 1] * o_refs[g][:, hs].astype(F32)
        merged_ref[:, hs] = acc.astype(BF16)
    out_ref[...] = h_ref[...] + _dot(merged_ref[...], w_ref[...])


def _b_out(h2d, outs, lses, w_out):
    t = h2d.shape[0]
    tm = 512
    tile = pl.BlockSpec((tm, D_MODEL), lambda i: (i, 0))
    ltile = pl.BlockSpec((tm, LANES), lambda i: (i, 0))
    return pl.pallas_call(
        _b_out_kernel,
        grid=(t // tm,),
        in_specs=[tile, tile, tile, tile, ltile, ltile, ltile,
                  pl.BlockSpec((D_MODEL, D_MODEL), lambda i: (0, 0))],
        out_specs=tile,
        out_shape=jax.ShapeDtypeStruct((t, D_MODEL), F32),
        scratch_shapes=[pltpu.VMEM((tm, D_MODEL), BF16)],
        compiler_params=_params(("parallel",)),
        name="b_out",
    )(h2d, *outs, *lses, w_out.astype(BF16))


def _b_proj_res_kernel(plan, n_out, n_tiles, h_ref, g_ref, w_ref, gain_ref, cos_ref, sin_ref, *rest):
    out_refs, y_ref = rest[:n_out], rest[n_out]
    tm = h_ref.shape[0]
    step = pl.program_id(0)

    def project(slot):
        xn = _rms_rows(h_ref[...], g_ref[...]).astype(BF16)
        for ch in range(len(plan)):
            y = _dot(xn, w_ref[:, ch * MXU_DIM:(ch + 1) * MXU_DIM])
            for j in range(MXU_DIM // LANES):
                y_ref[slot, ch * (MXU_DIM // LANES) + j] = y[:, j * LANES:(j + 1) * LANES]

    def finish(slot):
        cos, sin = cos_ref[...], sin_ref[...]
        for ch, (roped, dests) in enumerate(plan):
            for j in range(MXU_DIM // LANES):
                plane = ch * (MXU_DIM // LANES) + j
                if roped:
                    gain = gain_ref[:, plane * LANES:(plane + 1) * LANES]
                    y_ref[slot, plane] = _rope(_rms_rows(y_ref[slot, plane], gain), cos, sin)
                for oi, d, c0 in dests:
                    for r in range(d):
                        out_refs[oi][r, :, c0 + j * LANES:c0 + (j + 1) * LANES] = (
                            y_ref[slot, plane, pl.ds(r, tm // d, stride=d), :].astype(BF16))

    @pl.when(step == 0)
    def _():
        project(0)

    for parity in range(2):
        @pl.when((step > 0) & (step < n_tiles) & (step % 2 == parity))
        def _():
            project(parity)
            finish(1 - parity)

    @pl.when(step == n_tiles)
    def _():
        finish((n_tiles - 1) % 2)


def _b_proj_res(h2d, batch, seq, norm_gain, w, head_gain_row, plan, outs, cos, sin):
    n = w.shape[1]
    tm = 512
    per_seq = seq // tm
    n_tiles = batch * per_seq
    full = lambda shape: pl.BlockSpec(shape, lambda s: (0,) * len(shape))
    ahead = lambda s: jnp.minimum(s, n_tiles - 1)
    behind = lambda s: jnp.maximum(s - 1, 0)
    return pl.pallas_call(
        functools.partial(_b_proj_res_kernel, plan, len(outs), n_tiles),
        grid=(n_tiles + 1,),
        in_specs=[pl.BlockSpec((tm, D_MODEL), lambda s: (ahead(s), 0)),
                  full((1, D_MODEL)), full((D_MODEL, n)), full((1, n)),
                  pl.BlockSpec((tm, B_HEAD_DIM), lambda s: (behind(s), 0)),
                  pl.BlockSpec((tm, B_HEAD_DIM), lambda s: (behind(s), 0))],
        out_specs=[pl.BlockSpec((None, d, tm // d, width),
                                lambda s: (behind(s) // per_seq, 0, behind(s) % per_seq, 0))
                   for d, width in outs],
        out_shape=[jax.ShapeDtypeStruct((batch, d, seq // d, width), BF16) for d, width in outs],
        scratch_shapes=[pltpu.VMEM((2, n // LANES, tm, LANES), F32)],
        compiler_params=_params(("arbitrary",)),
        name="b_proj_res",
    )(h2d, norm_gain.reshape(1, D_MODEL), w.astype(BF16), head_gain_row, cos, sin)


def _dilated_shifted_kernel(q_ref, kc_ref, kp_ref, vc_ref, vp_ref, bias_ref, bias0_ref,
                            o_ref, l_ref, kbuf, vbuf):
    tl = q_ref.shape[0]
    blk = BAND_BLOCK
    kbuf[0:blk, :] = kp_ref[...]
    kbuf[blk:, :] = kc_ref[...]
    vbuf[0:blk, :] = vp_ref[...]
    vbuf[blk:, :] = vc_ref[...]
    bias = bias_ref[...]
    bias_first = jnp.where(pl.program_id(2) == 0, bias0_ref[...], bias)
    lane = lax.broadcasted_iota(jnp.int32, (blk, LANES), 1)
    for i in range(tl // blk):
        rows = slice(i * blk, (i + 1) * blk)
        keys = slice(i * blk, (i + 2) * blk)
        l_tile = jnp.zeros((blk, LANES), F32)
        for h in range(B_HEADS):
            hs = slice(h * B_HEAD_DIM, (h + 1) * B_HEAD_DIM)
            s = _dot_nt(q_ref[rows, hs], kbuf[keys, hs]) + (bias_first if i == 0 else bias)
            p = jnp.exp2(s)
            o_ref[rows, hs] = _dot(p.astype(BF16), vbuf[keys, hs]).astype(o_ref.dtype)
            l_tile = jnp.where(lane == h, jnp.sum(p, axis=-1, keepdims=True), l_tile)
        l_ref[rows, :] = l_tile


def _band_shift_biases(shift_log2):
    qi = np.arange(BAND_BLOCK)[:, None]
    kk = np.arange(2 * BAND_BLOCK)[None, :]
    band = (kk >= qi) & (kk <= qi + BAND_BLOCK)
    band0 = band & (kk >= BAND_BLOCK)
    neg = jnp.full((BAND_BLOCK, 2 * BAND_BLOCK), NEG_INF, F32)
    return jnp.where(band, -shift_log2, neg), jnp.where(band0, -shift_log2, neg)


def _dilated_shifted(q_g, kv_d, d, batch, seq, biases):
    length = seq // d
    tl = min(1024, length)
    width = B_HEADS * B_HEAD_DIM
    sub = tl // BAND_BLOCK

    def cur(part):
        return pl.BlockSpec((None, None, tl, width), lambda b, r, n: (b, r, n, part))

    def prev(part):
        return pl.BlockSpec((None, None, BAND_BLOCK, width),
                            lambda b, r, n: (b, r, jnp.maximum(n * sub - 1, 0), part))

    bias = pl.BlockSpec((BAND_BLOCK, 2 * BAND_BLOCK), lambda b, r, n: (0, 0))
    return pl.pallas_call(
        _dilated_shifted_kernel,
        grid=(batch, d, length // tl),
        in_specs=[cur(0), cur(0), prev(0), cur(1), prev(1), bias, bias],
        out_specs=[cur(0), pl.BlockSpec((None, None, tl, LANES), lambda b, r, n: (b, r, n, 0))],
        out_shape=[jax.ShapeDtypeStruct((batch, d, length, width), BF16),
                   jax.ShapeDtypeStruct((batch, d, length, LANES), F32)],
        scratch_shapes=[pltpu.VMEM((tl + BAND_BLOCK, width), BF16)] * 2,
        compiler_params=_params(("parallel", "parallel", "arbitrary")),
        name=f"dilated_shifted_d{d}",
    )(q_g, kv_d, kv_d, kv_d, kv_d, *biases)


def _b_out_sum_kernel(dils, h_ref, *rest):
    n = len(dils)
    a_refs, l_refs = rest[:n], rest[n:2 * n]
    w_ref, out_ref, acc_ref, den_ref, merged_ref = rest[2 * n:]
    tm = h_ref.shape[0]
    for g, d in enumerate(dils):
        for r in range(d):
            idx = pl.ds(r, tm // d, stride=d)
            planes = [(den_ref, (idx, slice(None)), l_refs[g][r])]
            planes += [(acc_ref, (h, idx, slice(None)),
                        a_refs[g][r, :, h * B_HEAD_DIM:(h + 1) * B_HEAD_DIM].astype(F32))
                       for h in range(B_HEADS)]
            for ref, at, val in planes:
                ref[at] = val if g == 0 else ref[at] + val
    den = den_ref[...]
    for h in range(B_HEADS):
        hs = slice(h * B_HEAD_DIM, (h + 1) * B_HEAD_DIM)
        merged_ref[:, hs] = (acc_ref[h] / den[:, h:h + 1]).astype(BF16)
    out_ref[...] = h_ref[...] + _dot(merged_ref[...], w_ref[...])


def _b_out_sum(h2d, batch, seq, accs, dens, w_out):
    tm = 512
    per_seq = seq // tm
    dils = B_DILATIONS
    tile = pl.BlockSpec((tm, D_MODEL), lambda b, i: (b * per_seq + i, 0))
    res = lambda d, width: pl.BlockSpec((None, d, tm // d, width), lambda b, i: (b, 0, i, 0))
    return pl.pallas_call(
        functools.partial(_b_out_sum_kernel, dils),
        grid=(batch, per_seq),
        in_specs=([tile] + [res(d, D_MODEL) for d in dils] + [res(d, LANES) for d in dils]
                  + [pl.BlockSpec((D_MODEL, D_MODEL), lambda b, i: (0, 0))]),
        out_specs=tile,
        out_shape=jax.ShapeDtypeStruct(h2d.shape, F32),
        scratch_shapes=[pltpu.VMEM((B_HEADS, tm, B_HEAD_DIM), F32), pltpu.VMEM((tm, LANES), F32),
                        pltpu.VMEM((tm, D_MODEL), BF16)],
        compiler_params=_params(("parallel", "parallel")),
        name="b_out_sum",
    )(h2d, *accs, *dens, w_out.astype(BF16))


def _route(logits):
    tm = logits.shape[0]
    lane = lax.broadcasted_iota(jnp.int32, (tm, LANES), 1)
    lanef = lane.astype(F32)
    far = float(LANES)
    is_g = lane < MOE_GROUPS
    g_max = jnp.max(jnp.where(is_g, logits, NEG_INF), axis=-1, keepdims=True)
    g_sum = jnp.sum(jnp.where(is_g, jnp.exp(logits - g_max), 0.0), axis=-1, keepdims=True)
    g_idx = jnp.min(jnp.where(is_g, jnp.where(logits == g_max, lanef, far), far), axis=-1, keepdims=True)
    per_group = MOE_EXPERTS // MOE_GROUPS
    e_lo = ROUTE_E0 + g_idx * per_group
    in_group = jnp.where(lanef >= e_lo, jnp.where(lanef < e_lo + per_group, 1.0, 0.0), 0.0)
    cand1 = jnp.where(in_group > 0.0, logits, NEG_INF)
    v1 = jnp.max(cand1, axis=-1, keepdims=True)
    i1 = jnp.min(jnp.where(cand1 == v1, lanef, far), axis=-1, keepdims=True)
    cand2 = jnp.where(lanef == i1, NEG_INF, cand1)
    v2 = jnp.max(cand2, axis=-1, keepdims=True)
    i2 = jnp.min(jnp.where(cand2 == v2, lanef, far), axis=-1, keepdims=True)
    e21 = jnp.exp(v2 - v1)
    w1 = 1.0 / (1.0 + e21)
    w2 = e21 / (1.0 + e21)
    comb = (jnp.where(lanef == i1, w1, 0.0) + jnp.where(lanef == i2, w2, 0.0)) / g_sum
    return comb, jnp.where(lanef == g_idx, 1.0, 0.0)


def _moe_kernel(h_ref, g_ref, wr12_ref, wr1_ref, br_ref, tri_ref, wgu_ref, wd_ref, out_ref,
                xn_ref, c1_ref, c2_ref, posc_ref, posr_ref, hid_ref, meta_ref):
    grp = pl.program_id(1)
    per_group = MOE_EXPERTS // MOE_GROUPS
    tm = h_ref.shape[0]
    ch = MOE_CHUNK

    @pl.when(grp == 0)
    def _():
        xn = _rms_rows(h_ref[...], g_ref[...])
        x1 = xn.astype(BF16)
        x2 = (xn - x1.astype(F32)).astype(BF16)
        first = _dot(x1, wr12_ref[...])
        logits = first[:, :LANES] + (first[:, LANES:] + _dot(x2, wr1_ref[...])) + br_ref[...]
        comb, onehot = _route(logits)
        c1 = comb.astype(BF16)
        c1_ref[...] = c1
        c2_ref[...] = (comb - c1.astype(F32)).astype(BF16)
        xn_ref[...] = x1
        count = _dot(tri_ref[...], onehot.astype(BF16))
        half = ch // 2
        padded = jnp.floor((count[tm - 1:tm, :] + (half - 1)) * (1.0 / half)) * half
        lane1 = lax.broadcasted_iota(jnp.int32, (1, LANES), 1)
        prev = jnp.where(lane1 >= 1, pltpu.roll(padded, 1, 1), 0.0)
        start = (prev + jnp.where(lane1 >= 2, pltpu.roll(prev, 1, 1), 0.0)
                 + jnp.where(lane1 >= 3, pltpu.roll(prev, 2, 1), 0.0))
        slot = jnp.sum(onehot * (start + count - 1.0), axis=-1, keepdims=True)
        slot_b = jnp.broadcast_to(slot, (tm, LANES))
        posc_ref[...] = slot_b
        posr_ref[...] = slot_b.T[0:8, :]
        for g in range(MOE_GROUPS):
            pick = lane1 == g
            size = jnp.sum(jnp.where(pick, padded, 0.0)).astype(jnp.int32)
            meta_ref[g] = size // ch
            meta_ref[MOE_GROUPS + g] = jnp.sum(jnp.where(pick, start, 0.0)).astype(jnp.int32)
            meta_ref[2 * MOE_GROUPS + g] = (size % ch) // half
        out_ref[...] = h_ref[...]

    def run_chunk(first, size):
        first_slot = first.astype(F32)
        lane = lax.broadcasted_iota(jnp.int32, (size, LANES), 1)
        rows = lax.broadcasted_iota(jnp.int32, (size, tm), 0).astype(F32) + first_slot
        gather = jnp.where(posr_ref[0:1, :] == rows, 1.0, 0.0).astype(BF16)
        xs = _dot(gather, xn_ref[...]).astype(BF16)
        cs = _dot(gather, c1_ref[...]) + _dot(gather, c2_ref[...])
        for e in range(per_group):
            gate_w = jnp.sum(jnp.where(lane == grp * per_group + (ROUTE_E0 + e), cs, 0.0),
                             axis=-1, keepdims=True)
            gu = _dot(xs, wgu_ref[e])
            g, u = gu[:, :MOE_FF], gu[:, MOE_FF:]
            hid_ref[0:size, e * MOE_FF:(e + 1) * MOE_FF] = (
                (g / (1.0 + jnp.exp(-g))) * u * gate_w).astype(BF16)
        y = _dot(hid_ref[0:size, :], wd_ref[...]).astype(BF16)
        cols = lax.broadcasted_iota(jnp.int32, (tm, size), 1).astype(F32) + first_slot
        scatter = jnp.where(posc_ref[:, 0:1] == cols, 1.0, 0.0).astype(BF16)
        out_ref[...] += _dot(scatter, y)

    base = meta_ref[MOE_GROUPS + grp]
    full_chunks = meta_ref[grp]

    @pl.loop(0, full_chunks)
    def _(c):
        run_chunk(base + c * ch, ch)

    @pl.when(meta_ref[2 * MOE_GROUPS + grp] == 1)
    def _():
        run_chunk(base + full_chunks * ch, ch // 2)


def _moe_weights(group_w, group_b, expert_w, expert_b, w_gate, w_up, w_down):
    pad = LANES - MOE_GROUPS - MOE_EXPERTS
    w_r = jnp.pad(jnp.concatenate([group_w, expert_w], axis=-1), ((0, 0), (0, 0), (0, pad)))
    r1 = w_r.astype(BF16)
    r2 = (w_r - r1.astype(F32)).astype(BF16)
    b_r = jnp.pad(jnp.concatenate([group_b, expert_b], axis=-1), ((0, 0), (0, pad)))[:, None, :]
    layers = w_gate.shape[0]
    per_group = MOE_EXPERTS // MOE_GROUPS
    w_gu = jnp.concatenate([w_gate, w_up], axis=-1).astype(BF16).reshape(
        layers, MOE_GROUPS, per_group, D_MODEL, 2 * MOE_FF)
    w_d = w_down.astype(BF16).reshape(layers, MOE_GROUPS, per_group * MOE_FF, D_MODEL)
    return jnp.concatenate([r1, r2], axis=-1), r1, b_r, w_gu, w_d


def _moe(h2d, norm_gain, layer, weights):
    wr12, wr1, b_r, w_gu, w_d = weights
    t = h2d.shape[0]
    tm = 1024
    per_group = MOE_EXPERTS // MOE_GROUPS
    tile = pl.BlockSpec((tm, D_MODEL), lambda i, g: (i, 0))
    of_layer = lambda shape: pl.BlockSpec((None,) + shape, lambda i, g: (layer,) + (0,) * len(shape))
    tri = jnp.asarray(np.tril(np.ones((tm, tm), np.float32)), BF16)
    return pl.pallas_call(
        _moe_kernel,
        grid=(t // tm, MOE_GROUPS),
        in_specs=[tile, pl.BlockSpec((1, D_MODEL), lambda i, g: (0, 0)),
                  of_layer((D_MODEL, 2 * LANES)), of_layer((D_MODEL, LANES)), of_layer((1, LANES)),
                  pl.BlockSpec((tm, tm), lambda i, g: (0, 0)),
                  pl.BlockSpec((None, None, per_group, D_MODEL, 2 * MOE_FF), lambda i, g: (layer, g, 0, 0, 0)),
                  pl.BlockSpec((None, None, per_group * MOE_FF, D_MODEL), lambda i, g: (layer, g, 0, 0))],
        out_specs=tile,
        out_shape=jax.ShapeDtypeStruct((t, D_MODEL), F32),
        scratch_shapes=[pltpu.VMEM((tm, D_MODEL), BF16), pltpu.VMEM((tm, LANES), BF16),
                        pltpu.VMEM((tm, LANES), BF16), pltpu.VMEM((tm, LANES), F32),
                        pltpu.VMEM((8, tm), F32), pltpu.VMEM((MOE_CHUNK, per_group * MOE_FF), BF16),
                        pltpu.SMEM((3 * MOE_GROUPS,), jnp.int32)],
        compiler_params=_params(("parallel", "arbitrary")),
        name="moe",
    )(h2d, norm_gain.reshape(1, D_MODEL), wr12, wr1, b_r, tri, w_gu, w_d)


def kernel(x, positions, a_norm, a_w_in, a_b_f, a_q_gain, a_k_gain, a_w_out, kv_norm, kv_w, kv_k_gain, b_norm, b_w_q, b_q_gain, b_w_out, ffn_norm, moe_group_w, moe_group_b, moe_expert_w, moe_expert_b, moe_w_gate, moe_w_up, moe_w_down):
    batch, seq, _ = x.shape
    n_a = a_norm.shape[0]
    n_b = b_norm.shape[0]
    b_width = B_HEADS * B_HEAD_DIM
    n_groups = len(B_DILATIONS)
    chunks = b_width // MXU_DIM

    a_shifts = [(A_HEAD_DIM ** 0.5) * jnp.max(jnp.abs(a_q_gain[i])) * jnp.max(jnp.abs(a_k_gain[i]))
                for i in range(n_a)]
    b_shifts = [(B_HEAD_DIM ** 0.5) * jnp.max(jnp.abs(b_q_gain[j])) * jnp.max(jnp.abs(kv_k_gain))
                for j in range(n_b)]

    moe_weights = _moe_weights(moe_group_w, moe_group_b, moe_expert_w, moe_expert_b,
                               moe_w_gate, moe_w_up, moe_w_down)

    def trunk(fixed_shift, x):
        h = x.reshape(batch * seq, D_MODEL)
        kv_sh = cos = sin = None
        for layer in range(n_a + n_b):
            if layer < n_a:
                i = layer
                q, k, vt = _a_proj(h, seq, a_norm[i], a_w_in[i], a_b_f[i], a_q_gain[i], a_k_gain[i],
                                   a_shifts[i])
                o = _fox_attention(q, k, vt, batch, seq, fixed_shift)
                h = _out_proj(h, o.reshape(batch * seq, D_MODEL), a_w_out[i])
            else:
                j = layer - n_a
                k_gain_row = jnp.concatenate(
                    [jnp.tile(kv_k_gain, B_HEADS), jnp.ones((b_width,), F32)]).reshape(1, 2 * b_width)
                q_gain_row = (jnp.tile(b_q_gain[j], (1, B_HEADS)) * (B_HEAD_DIM ** -0.5)).reshape(
                    1, n_groups * b_width)
                if j == 0:
                    cos, sin = _rope_tables(positions)
                if fixed_shift:
                    if j == 0:
                        kv_plan = tuple((ch < chunks, tuple((g, d, ch * MXU_DIM) for g, d in enumerate(B_DILATIONS)))
                                        for ch in range(2 * chunks))
                        kv_sh = _b_proj_res(h, batch, seq, kv_norm, kv_w, k_gain_row, kv_plan,
                                            [(d, 2 * b_width) for d in B_DILATIONS], cos, sin)
                    q_plan = tuple((True, ((ch // chunks, B_DILATIONS[ch // chunks], (ch % chunks) * MXU_DIM),))
                                   for ch in range(n_groups * chunks))
                    q_res = _b_proj_res(h, batch, seq, b_norm[j], b_w_q[j], q_gain_row * LOG2E, q_plan,
                                        [(d, b_width) for d in B_DILATIONS], cos, sin)
                    biases = _band_shift_biases(b_shifts[j] * LOG2E)
                    accs, dens = zip(*[_dilated_shifted(q_res[g], kv_sh[g], d, batch, seq, biases)
                                       for g, d in enumerate(B_DILATIONS)])
                    h = _b_out_sum(h, batch, seq, accs, dens, b_w_out[j])
                else:
                    if j == 0:
                        kv_sh = _b_proj(h, kv_norm, kv_w, k_gain_row, b_width, cos, sin)
                    q_all = _b_proj(h, b_norm[j], b_w_q[j], q_gain_row, n_groups * b_width, cos, sin)
                    outs, lses = zip(*[_dilated_group(q_all, kv_sh, g, batch, seq) for g in range(n_groups)])
                    h = _b_out(h, outs, lses, b_w_out[j])
            h = _moe(h, ffn_norm[layer], layer, moe_weights)
        return h.reshape(batch, seq, D_MODEL)

    largest = functools.reduce(jnp.maximum, a_shifts + b_shifts)
    return lax.cond(largest <= MAX_SAFE_SHIFT, functools.partial(trunk, True),
                    functools.partial(trunk, False), x)
```

```python
import functools
import math

import numpy as np
import jax
import jax.numpy as jnp
from jax import lax
from jax.experimental import pallas as pl
from jax.experimental.pallas import tpu as pltpu

F32 = jnp.float32
BF16 = jnp.bfloat16

D_MODEL = 1024
EPS = 1e-6
ROPE_THETA = 10000.0
A_HEADS = 16
A_HEAD_DIM = 64
B_HEADS = 8
B_HEAD_DIM = 128
B_WINDOWS = (128, 512, 2048)
B_DILATIONS = (1, 4, 16)
BAND_BLOCK = 128
assert all(w // d == BAND_BLOCK for w, d in zip(B_WINDOWS, B_DILATIONS))
MOE_GROUPS = 4
MOE_EXPERTS = 16
MOE_FF = 256
MOE_CHUNK = 256

ROW_TILE = 512
WIDE_ROW_TILE = 1024
LANES = 128
MXU_DIM = 256
VMEM_LIMIT = 56 * 1024 * 1024

NEG_INF = float("-inf")
LOG2E = math.log2(math.e)
FOX_BLOCK = 512
FOX_UNROLL = 4
MAX_SAFE_SHIFT = 40.0

X_BASE = A_HEAD_DIM
ROUTE_E0 = MOE_GROUPS


def _params(sem):
    return pltpu.CompilerParams(dimension_semantics=sem, vmem_limit_bytes=VMEM_LIMIT)


def _rms_rows(x, gain_row):
    ms = jnp.mean(x * x, axis=-1, keepdims=True)
    return x * lax.rsqrt(ms + EPS) * gain_row


def _dot(a, b):
    return jnp.dot(a, b, preferred_element_type=F32)


def _dot_nt(a, b):
    return lax.dot_general(a, b, (((1,), (1,)), ((), ())), preferred_element_type=F32)


def _split3(x):
    p1 = x.astype(BF16)
    r1 = x - p1.astype(F32)
    p2 = r1.astype(BF16)
    p3 = (r1 - p2.astype(F32)).astype(BF16)
    return p1, p2, p3


def _seg_norm(x, ones_bd, seg):
    ss = _dot((x * x).astype(BF16), ones_bd)
    return x * lax.rsqrt(ss * (1.0 / seg) + EPS)


def _rope_table_kernel(pos_ref, inv_ref, sign_ref, cos_ref, sin_ref):
    ang = pos_ref[...].astype(F32) * inv_ref[...]
    cos_ref[...] = jnp.cos(ang)
    sin_ref[...] = jnp.sin(ang) * sign_ref[...]


def _rope_tables(positions):
    t = positions.size
    tm = WIDE_ROW_TILE
    half = B_HEAD_DIM // 2
    inv = ROPE_THETA ** (-jnp.arange(half, dtype=F32) / half)
    inv = jnp.concatenate([inv, inv]).reshape(1, B_HEAD_DIM)
    sign = jnp.concatenate([-jnp.ones((half,), F32), jnp.ones((half,), F32)]).reshape(1, B_HEAD_DIM)
    row = pl.BlockSpec((1, B_HEAD_DIM), lambda i: (0, 0))
    tab = pl.BlockSpec((tm, B_HEAD_DIM), lambda i: (i, 0))
    return pl.pallas_call(
        _rope_table_kernel,
        grid=(t // tm,),
        in_specs=[pl.BlockSpec((tm, 1), lambda i: (i, 0)), row, row],
        out_specs=[tab, tab],
        out_shape=[jax.ShapeDtypeStruct((t, B_HEAD_DIM), F32)] * 2,
        compiler_params=_params(("parallel",)),
        name="rope_tables",
    )(positions.reshape(t, 1), inv, sign)


def _rope(x, cos, sin_signed):
    return x * cos + pltpu.roll(x, B_HEAD_DIM // 2, 1) * sin_signed


def _a_proj_kernel(tiles_per_seq, h_ref, g_ref, wqkv_ref, wf_ref, bf_ref, qg_ref, kg_ref,
                   ones_ref, tri_ref, selq_ref, selk_ref, shift_ref, q_ref, k_ref, vt_ref, carry_ref):
    tm = h_ref.shape[0]

    @pl.when(pl.program_id(0) % tiles_per_seq == 0)
    def _():
        carry_ref[...] = jnp.zeros_like(carry_ref)

    xn = _rms_rows(h_ref[...], g_ref[...]).astype(BF16)

    fl = _dot(xn, wf_ref[...]) + bf_ref[...]
    lf = jnp.minimum(fl, 0.0) - jnp.log(1.0 + jnp.exp(-jnp.abs(fl)))
    tri = tri_ref[...]
    l1, l2, l3 = _split3(lf)
    c = _dot(tri, l1) + _dot(tri, l2) + _dot(tri, l3) + carry_ref[0:1, :]
    carry_ref[0:1, :] = c[tm - 1:tm, :]

    lane = lax.broadcasted_iota(jnp.int32, (tm, LANES), 1)
    ck = c * LOG2E
    cq = ck - shift_ref[...]
    pieces = [p.astype(F32) for p in _split3(cq) + _split3(ck)]
    e = jnp.where(lane == 6 * A_HEADS, 1.0, 0.0)
    for n in reversed(range(6)):
        piece = pieces[n] if n == 0 else pltpu.roll(pieces[n], n * A_HEADS, 1)
        e = jnp.where((lane >= n * A_HEADS) & (lane < (n + 1) * A_HEADS), piece, e)
    e = e.astype(BF16)
    v_extra = jnp.where(lane == X_BASE, 1.0, 0.0)
    low = lane < A_HEAD_DIM
    tkv = vt_ref.shape[-1]

    ones_bd = ones_ref[...]
    n_chunks = D_MODEL // MXU_DIM
    for ch in range(n_chunks):
        cs = slice(ch * MXU_DIM, (ch + 1) * MXU_DIM)
        qf = _dot(xn, wqkv_ref[:, cs])
        kf = _dot(xn, wqkv_ref[:, D_MODEL + ch * MXU_DIM:D_MODEL + (ch + 1) * MXU_DIM])
        vf = _dot(xn, wqkv_ref[:, 2 * D_MODEL + ch * MXU_DIM:2 * D_MODEL + (ch + 1) * MXU_DIM])
        qn = _seg_norm(qf, ones_bd, A_HEAD_DIM) * qg_ref[:, cs]
        kn = _seg_norm(kf, ones_bd, A_HEAD_DIM) * kg_ref[:, cs]
        heads_per_chunk = MXU_DIM // A_HEAD_DIM
        hs = slice(ch * heads_per_chunk * LANES, (ch + 1) * heads_per_chunk * LANES)
        exq = _dot(e, selq_ref[:, hs])
        exk = _dot(e, selk_ref[:, hs])
        for j in range(heads_per_chunk):
            pair = slice((j // 2) * LANES, (j // 2 + 1) * LANES)
            blk = slice(j * LANES, (j + 1) * LANES)
            out = slice((ch * heads_per_chunk + j) * LANES, (ch * heads_per_chunk + j + 1) * LANES)
            qp, kp, vp = qn[:, pair], kn[:, pair], vf[:, pair]
            if j % 2 == 1:
                qp = pltpu.roll(qp, A_HEAD_DIM, 1)
                kp = pltpu.roll(kp, A_HEAD_DIM, 1)
                vp = pltpu.roll(vp, A_HEAD_DIM, 1)
            q_ref[:, out] = jnp.where(low, qp, exq[:, blk]).astype(BF16)
            k_ref[:, out] = jnp.where(low, kp, exk[:, blk]).astype(BF16)
            v_aug = jnp.where(low, vp, v_extra)
            for cc in range(tm // tkv):
                vt_ref[cc, ch * heads_per_chunk + j] = (
                    v_aug[cc * tkv:(cc + 1) * tkv, :].T.astype(BF16))


def _a_sel_matrices():
    selq = np.zeros((LANES, A_HEADS * LANES), np.float32)
    selk = np.zeros((LANES, A_HEADS * LANES), np.float32)
    ones_lane = 6 * A_HEADS
    for h in range(A_HEADS):
        base = h * LANES + X_BASE
        for piece in range(3):
            selq[A_HEADS * piece + h, base + piece] = 1.0
            selq[ones_lane, base + 3 + piece] = 1.0
            selk[ones_lane, base + piece] = 1.0
            selk[A_HEADS * (3 + piece) + h, base + 3 + piece] = -1.0
    return jnp.asarray(selq, BF16), jnp.asarray(selk, BF16)


def _block_diag_ones(seg):
    idx = np.arange(MXU_DIM) // seg
    return jnp.asarray((idx[:, None] == idx[None, :]).astype(np.float32), BF16)


def _a_proj(h2d, seq, norm_gain, w_in, b_f, q_gain, k_gain, shift):
    t = h2d.shape[0]
    tm = ROW_TILE
    wide = A_HEADS * LANES
    w_qkv = w_in[:, :3 * D_MODEL].astype(BF16)
    w_f = jnp.pad(w_in[:, 3 * D_MODEL:], ((0, 0), (0, LANES - A_HEADS))).astype(BF16)
    b_row = jnp.pad(b_f, (0, LANES - A_HEADS)).reshape(1, LANES)
    qg = (jnp.tile(q_gain, A_HEADS) * (A_HEAD_DIM ** -0.5 * LOG2E)).reshape(1, D_MODEL)
    kg = jnp.tile(k_gain, A_HEADS).reshape(1, D_MODEL)
    tri = jnp.asarray(np.tril(np.ones((tm, tm), np.float32)), BF16)
    selq, selk = _a_sel_matrices()
    shift_row = jnp.full((1, LANES), LOG2E, F32) * shift
    full = lambda shape: pl.BlockSpec(shape, lambda i: (0,) * len(shape))
    out_spec = pl.BlockSpec((tm, wide), lambda i: (i, 0))
    per_tile = tm // FOX_BLOCK
    return pl.pallas_call(
        functools.partial(_a_proj_kernel, seq // tm),
        grid=(t // tm,),
        in_specs=[pl.BlockSpec((tm, D_MODEL), lambda i: (i, 0)),
                  full((1, D_MODEL)), full((D_MODEL, 3 * D_MODEL)), full((D_MODEL, LANES)),
                  full((1, LANES)), full((1, D_MODEL)), full((1, D_MODEL)),
                  full((MXU_DIM, MXU_DIM)), full((tm, tm)), full((LANES, wide)), full((LANES, wide)),
                  full((1, LANES))],
        out_specs=[out_spec, out_spec,
                   pl.BlockSpec((per_tile, A_HEADS, LANES, FOX_BLOCK), lambda i: (i, 0, 0, 0))],
        out_shape=[jax.ShapeDtypeStruct((t, wide), BF16), jax.ShapeDtypeStruct((t, wide), BF16),
                   jax.ShapeDtypeStruct((t // FOX_BLOCK, A_HEADS, LANES, FOX_BLOCK), BF16)],
        scratch_shapes=[pltpu.VMEM((8, LANES), F32)],
        compiler_params=_params(("arbitrary",)),
        name="a_proj",
    )(h2d, norm_gain.reshape(1, D_MODEL), w_qkv, w_f, b_row, qg, kg,
      _block_diag_ones(A_HEAD_DIM), tri, selq, selk, shift_row)


def _fox_pair_output(accs_t, o_ref, rows):
    outs = [(a / a[X_BASE:X_BASE + 1, :]).T for a in accs_t]
    lane = lax.broadcasted_iota(jnp.int32, outs[0].shape, 1)
    o_ref[rows, :] = jnp.where(lane < A_HEAD_DIM, outs[0],
                               pltpu.roll(outs[1], A_HEAD_DIM, 1)).astype(o_ref.dtype)


def _fox_shifted_kernel(q_ref, k_ref, vt_ref, o_ref, acc_ref):
    blk = FOX_BLOCK
    row = lax.broadcasted_iota(jnp.int32, (blk, blk), 0)
    col = lax.broadcasted_iota(jnp.int32, (blk, blk), 1)
    key_visible = row <= col

    def q_tile(i, _):
        rows = pl.ds(pl.multiple_of(i * blk, blk), blk)
        qs = [q_ref[rows, hh * LANES:(hh + 1) * LANES] for hh in range(2)]

        def block(j, masked=False):
            keys = pl.ds(pl.multiple_of(j * blk, blk), blk)
            ss = [_dot_nt(k_ref[keys, hh * LANES:(hh + 1) * LANES], qs[hh]) for hh in range(2)]
            for hh in range(2):
                p = jnp.exp2(ss[hh])
                if masked:
                    p = jnp.where(key_visible, p, 0.0)
                acc_ref[hh] += _dot(vt_ref[j, hh], p.astype(BF16))

        acc_ref[...] = jnp.zeros_like(acc_ref)

        @pl.loop(0, i // FOX_UNROLL)
        def _(t):
            for u in range(FOX_UNROLL):
                block(FOX_UNROLL * t + u)

        left = i % FOX_UNROLL
        for n_left in range(FOX_UNROLL):
            @pl.when(left == n_left)
            def _():
                for u in range(n_left):
                    block(i - n_left + u)
                block(i, masked=True)

        _fox_pair_output((acc_ref[0], acc_ref[1]), o_ref, rows)
        return 0

    lax.fori_loop(0, q_ref.shape[0] // blk, q_tile, 0)


def _fox_online_kernel(q_ref, k_ref, vt_ref, o_ref):
    blk = FOX_BLOCK
    qi = pl.program_id(2)
    qs = [q_ref[:, hh * LANES:(hh + 1) * LANES] for hh in range(2)]
    row = lax.broadcasted_iota(jnp.int32, (blk, blk), 0)
    col = lax.broadcasted_iota(jnp.int32, (blk, blk), 1)
    key_visible = row <= col

    def block(j, carry, masked):
        keys = pl.ds(pl.multiple_of(j * blk, blk), blk)
        new = []
        for hh in range(2):
            m, acc = carry[2 * hh], carry[2 * hh + 1]
            s = _dot_nt(k_ref[keys, hh * LANES:(hh + 1) * LANES], qs[hh])
            if masked:
                s = jnp.where(key_visible, s, NEG_INF)
            m_new = jnp.maximum(m, jnp.max(s, axis=0, keepdims=True))
            p = jnp.exp2(s - m_new)
            acc = jnp.exp2(m - m_new) * acc + _dot(vt_ref[j, hh], p.astype(BF16))
            new += [m_new, acc]
        return tuple(new)

    init = (jnp.full((1, blk), NEG_INF, F32), jnp.zeros((LANES, blk), F32)) * 2
    carry = lax.fori_loop(0, qi, functools.partial(block, masked=False), init)
    carry = block(qi, carry, True)
    _fox_pair_output((carry[1], carry[3]), o_ref, slice(None))


def _fox_attention(q, k, vt, batch, seq, fixed_shift):
    pairs = A_HEADS // 2
    blk = FOX_BLOCK
    q3, k3 = (a.reshape(batch, seq, A_HEADS * LANES) for a in (q, k))
    vt5 = vt.reshape(batch, seq // blk, A_HEADS, LANES, blk)
    out_shape = jax.ShapeDtypeStruct((batch, seq, D_MODEL), BF16)

    def shifted(q3, k3, vt5):
        pair = pl.BlockSpec((None, seq, 2 * LANES), lambda b, p: (b, 0, p))
        return pl.pallas_call(
            _fox_shifted_kernel,
            grid=(batch, pairs),
            in_specs=[pair, pair,
                      pl.BlockSpec((None, seq // blk, 2, LANES, blk), lambda b, p: (b, 0, p, 0, 0))],
            out_specs=pl.BlockSpec((None, seq, LANES), lambda b, p: (b, 0, p)),
            out_shape=out_shape,
            scratch_shapes=[pltpu.VMEM((2, LANES, blk), F32)],
            compiler_params=_params(("parallel", "parallel")),
            name="fox_shifted",
        )(q3, k3, vt5)

    def online(q3, k3, vt5):
        return pl.pallas_call(
            _fox_online_kernel,
            grid=(batch, pairs, seq // blk),
            in_specs=[pl.BlockSpec((None, blk, 2 * LANES), lambda b, p, i: (b, i, p)),
                      pl.BlockSpec((None, seq, 2 * LANES), lambda b, p, i: (b, 0, p)),
                      pl.BlockSpec((None, seq // blk, 2, LANES, blk), lambda b, p, i: (b, 0, p, 0, 0))],
            out_specs=pl.BlockSpec((None, blk, LANES), lambda b, p, i: (b, i, p)),
            out_shape=out_shape,
            compiler_params=_params(("parallel", "parallel", "arbitrary")),
            name="fox_online",
        )(q3, k3, vt5)

    return shifted(q3, k3, vt5) if fixed_shift else online(q3, k3, vt5)


def _out_proj_kernel(h_ref, o_ref, w_ref, out_ref):
    out_ref[...] = h_ref[...] + _dot(o_ref[...], w_ref[...])


def _out_proj(h2d, o2d, w_out):
    t = h2d.shape[0]
    tm = WIDE_ROW_TILE
    tile = pl.BlockSpec((tm, D_MODEL), lambda i: (i, 0))
    return pl.pallas_call(
        _out_proj_kernel,
        grid=(t // tm,),
        in_specs=[tile, tile, pl.BlockSpec((D_MODEL, D_MODEL), lambda i: (0, 0))],
        out_specs=tile,
        out_shape=jax.ShapeDtypeStruct((t, D_MODEL), F32),
        compiler_params=_params(("parallel",)),
        name="out_proj",
    )(h2d, o2d, w_out.astype(BF16))


def _b_proj_kernel(n_rope_cols, h_ref, g_ref, w_ref, gain_ref, ones_ref, cos_ref, sin_ref, out_ref):
    xn = _rms_rows(h_ref[...], g_ref[...]).astype(BF16)
    cos, sin = cos_ref[...], sin_ref[...]
    ones_bd = ones_ref[...]
    for ch in range(w_ref.shape[1] // MXU_DIM):
        cs = slice(ch * MXU_DIM, (ch + 1) * MXU_DIM)
        y = _dot(xn, w_ref[:, cs])
        if ch * MXU_DIM < n_rope_cols:
            y = _seg_norm(y, ones_bd, B_HEAD_DIM) * gain_ref[:, cs]
            for j in range(MXU_DIM // B_HEAD_DIM):
                blk = slice(j * B_HEAD_DIM, (j + 1) * B_HEAD_DIM)
                out_ref[:, ch * MXU_DIM + j * B_HEAD_DIM:ch * MXU_DIM + (j + 1) * B_HEAD_DIM] = (
                    _rope(y[:, blk], cos, sin).astype(out_ref.dtype))
        else:
            out_ref[:, cs] = y.astype(out_ref.dtype)


def _b_proj(h2d, norm_gain, w, head_gain_row, n_rope_cols, cos, sin):
    t = h2d.shape[0]
    n = w.shape[1]
    tm = ROW_TILE
    full = lambda shape: pl.BlockSpec(shape, lambda i: (0,) * len(shape))
    return pl.pallas_call(
        functools.partial(_b_proj_kernel, n_rope_cols),
        grid=(t // tm,),
        in_specs=[pl.BlockSpec((tm, D_MODEL), lambda i: (i, 0)), full((1, D_MODEL)),
                  full((D_MODEL, n)), full((1, n)), full((MXU_DIM, MXU_DIM)),
                  pl.BlockSpec((tm, B_HEAD_DIM), lambda i: (i, 0)),
                  pl.BlockSpec((tm, B_HEAD_DIM), lambda i: (i, 0))],
        out_specs=pl.BlockSpec((tm, n), lambda i: (i, 0)),
        out_shape=jax.ShapeDtypeStruct((t, n), BF16),
        compiler_params=_params(("parallel",)),
        name="b_proj",
    )(h2d, norm_gain.reshape(1, D_MODEL), w.astype(BF16), head_gain_row,
      _block_diag_ones(B_HEAD_DIM), cos, sin)


def _dilated_kernel(q_ref, kc_ref, kp_ref, vc_ref, vp_ref, bp_ref, bc_ref, o_ref, lse_ref):
    tl = q_ref.shape[0]
    blk = BAND_BLOCK
    first_pen = jnp.where(pl.program_id(2) == 0, NEG_INF, 0.0)
    bias_prev, bias_cur = bp_ref[...], bc_ref[...]
    lane = lax.broadcasted_iota(jnp.int32, (blk, LANES), 1)
    for i in range(tl // blk):
        rows = slice(i * blk, (i + 1) * blk)
        lse_tile = jnp.zeros((blk, LANES), F32)
        for h in range(B_HEADS):
            hs = slice(h * B_HEAD_DIM, (h + 1) * B_HEAD_DIM)
            q = q_ref[rows, hs]
            if i == 0:
                k_prev, v_prev = kp_ref[:, hs], vp_ref[:, hs]
            else:
                prev_rows = slice((i - 1) * blk, i * blk)
                k_prev, v_prev = kc_ref[prev_rows, hs], vc_ref[prev_rows, hs]
            s_prev = _dot_nt(q, k_prev) + bias_prev
            if i == 0:
                s_prev = s_prev + first_pen
            s_cur = _dot_nt(q, kc_ref[rows, hs]) + bias_cur
            m = jnp.maximum(jnp.max(s_prev, axis=-1, keepdims=True),
                            jnp.max(s_cur, axis=-1, keepdims=True))
            p_prev = jnp.exp(s_prev - m)
            p_cur = jnp.exp(s_cur - m)
            l = jnp.sum(p_prev, axis=-1, keepdims=True) + jnp.sum(p_cur, axis=-1, keepdims=True)
            o = (_dot(p_prev.astype(BF16), v_prev) + _dot(p_cur.astype(BF16), vc_ref[rows, hs])) / l
            o_ref[rows, hs] = o.astype(o_ref.dtype)
            lse_tile = jnp.where(lane == h, m + jnp.log(l), lse_tile)
        lse_ref[rows, :] = lse_tile


def _band_biases():
    qi = np.arange(BAND_BLOCK)[:, None]
    kk = np.arange(BAND_BLOCK)[None, :]
    prev = np.where(kk >= qi, 0.0, -np.inf).astype(np.float32)
    cur = np.where(kk <= qi, 0.0, -np.inf).astype(np.float32)
    return jnp.asarray(prev), jnp.asarray(cur)


def _dilated_group(q_all, kv, group, batch, seq):
    d = B_DILATIONS[group]
    assert B_WINDOWS[group] // d == BAND_BLOCK and seq % (d * BAND_BLOCK) == 0
    length = seq // d
    tl = 256
    width = B_HEADS * B_HEAD_DIM
    sub = tl // BAND_BLOCK
    n_groups = len(B_DILATIONS)
    qv = q_all.reshape(batch, length, d * n_groups * width)
    kvv = kv.reshape(batch, length, d * 2 * width)

    def cur(part):
        return pl.BlockSpec((None, tl, width), lambda b, r, n: (b, n, r * 2 + part))

    def prev(part):
        return pl.BlockSpec((None, BAND_BLOCK, width),
                            lambda b, r, n: (b, jnp.maximum(n * sub - 1, 0), r * 2 + part))

    bias = pl.BlockSpec((BAND_BLOCK, BAND_BLOCK), lambda b, r, n: (0, 0))
    bias_prev, bias_cur = _band_biases()
    o, lse = pl.pallas_call(
        _dilated_kernel,
        grid=(batch, d, length // tl),
        in_specs=[pl.BlockSpec((None, tl, width), lambda b, r, n: (b, n, r * n_groups + group)),
                  cur(0), prev(0), cur(1), prev(1), bias, bias],
        out_specs=[pl.BlockSpec((None, tl, width), lambda b, r, n: (b, n, r)),
                   pl.BlockSpec((None, tl, LANES), lambda b, r, n: (b, n, r))],
        out_shape=[jax.ShapeDtypeStruct((batch, length, d * width), BF16),
                   jax.ShapeDtypeStruct((batch, length, d * LANES), F32)],
        compiler_params=_params(("parallel", "parallel", "arbitrary")),
        name=f"dilated_g{group}",
    )(qv, kvv, kvv, kvv, kvv, bias_prev, bias_cur)
    return o.reshape(batch * seq, width), lse.reshape(batch * seq, LANES)


def _b_out_kernel(h_ref, o0_ref, o1_ref, o2_ref, l0_ref, l1_ref, l2_ref, w_ref, out_ref, merged_ref):
    lses = [l0_ref[...], l1_ref[...], l2_ref[...]]
    top = jnp.maximum(jnp.maximum(lses[0], lses[1]), lses[2])
    es = [jnp.exp(x - top) for x in lses]
    den = es[0] + es[1] + es[2]
    ws = [x / den for x in es]
    o_refs = (o0_ref, o1_ref, o2_ref)
    for h in range(B_HEADS):
        hs = slice(h * B_HEAD_DIM, (h + 1) * B_HEAD_DIM)
        acc = ws[0][:, h:h + 1] * o_refs[0][:, hs].astype(F32)
        for g in (1, 2):
            acc = acc + ws[g][:, h:h + 1] * o_refs[g][:, hs].astype(F32)
        merged_ref[:, hs] = acc.astype(BF16)
    out_ref[...] = h_ref[...] + _dot(merged_ref[...], w_ref[...])


def _b_out(h2d, outs, lses, w_out):
    t = h2d.shape[0]
    tm = ROW_TILE
    tile = pl.BlockSpec((tm, D_MODEL), lambda i: (i, 0))
    ltile = pl.BlockSpec((tm, LANES), lambda i: (i, 0))
    return pl.pallas_call(
        _b_out_kernel,
        grid=(t // tm,),
        in_specs=[tile, tile, tile, tile, ltile, ltile, ltile,
                  pl.BlockSpec((D_MODEL, D_MODEL), lambda i: (0, 0))],
        out_specs=tile,
        out_shape=jax.ShapeDtypeStruct((t, D_MODEL), F32),
        scratch_shapes=[pltpu.VMEM((tm, D_MODEL), BF16)],
        compiler_params=_params(("parallel",)),
        name="b_out",
    )(h2d, *outs, *lses, w_out.astype(BF16))


def _b_proj_res_kernel(plan, n_out, n_tiles, h_ref, g_ref, w_ref, gain_ref, cos_ref, sin_ref, *rest):
    out_refs, y_ref = rest[:n_out], rest[n_out]
    tm = h_ref.shape[0]
    step = pl.program_id(0)

    def project(slot):
        xn = _rms_rows(h_ref[...], g_ref[...]).astype(BF16)
        for ch in range(len(plan)):
            y = _dot(xn, w_ref[:, ch * MXU_DIM:(ch + 1) * MXU_DIM])
            for j in range(MXU_DIM // LANES):
                y_ref[slot, ch * (MXU_DIM // LANES) + j] = y[:, j * LANES:(j + 1) * LANES]

    def finish(slot):
        cos, sin = cos_ref[...], sin_ref[...]
        for ch, (roped, dests) in enumerate(plan):
            for j in range(MXU_DIM // LANES):
                plane = ch * (MXU_DIM // LANES) + j
                if roped:
                    gain = gain_ref[:, plane * LANES:(plane + 1) * LANES]
                    y_ref[slot, plane] = _rope(_rms_rows(y_ref[slot, plane], gain), cos, sin)
                for oi, d, c0 in dests:
                    for r in range(d):
                        out_refs[oi][r, :, c0 + j * LANES:c0 + (j + 1) * LANES] = (
                            y_ref[slot, plane, pl.ds(r, tm // d, stride=d), :].astype(BF16))

    @pl.when(step == 0)
    def _():
        project(0)

    for parity in range(2):
        @pl.when((step > 0) & (step < n_tiles) & (step % 2 == parity))
        def _():
            project(parity)
            finish(1 - parity)

    @pl.when(step == n_tiles)
    def _():
        finish((n_tiles - 1) % 2)


def _b_proj_res(h2d, batch, seq, norm_gain, w, head_gain_row, plan, outs, cos, sin):
    n = w.shape[1]
    tm = ROW_TILE
    per_seq = seq // tm
    n_tiles = batch * per_seq
    full = lambda shape: pl.BlockSpec(shape, lambda s: (0,) * len(shape))
    ahead = lambda s: jnp.minimum(s, n_tiles - 1)
    behind = lambda s: jnp.maximum(s - 1, 0)
    return pl.pallas_call(
        functools.partial(_b_proj_res_kernel, plan, len(outs), n_tiles),
        grid=(n_tiles + 1,),
        in_specs=[pl.BlockSpec((tm, D_MODEL), lambda s: (ahead(s), 0)),
                  full((1, D_MODEL)), full((D_MODEL, n)), full((1, n)),
                  pl.BlockSpec((tm, B_HEAD_DIM), lambda s: (behind(s), 0)),
                  pl.BlockSpec((tm, B_HEAD_DIM), lambda s: (behind(s), 0))],
        out_specs=[pl.BlockSpec((None, d, tm // d, width),
                                lambda s: (behind(s) // per_seq, 0, behind(s) % per_seq, 0))
                   for d, width in outs],
        out_shape=[jax.ShapeDtypeStruct((batch, d, seq // d, width), BF16) for d, width in outs],
        scratch_shapes=[pltpu.VMEM((2, n // LANES, tm, LANES), F32)],
        compiler_params=_params(("arbitrary",)),
        name="b_proj_res",
    )(h2d, norm_gain.reshape(1, D_MODEL), w.astype(BF16), head_gain_row, cos, sin)


def _dilated_shifted_kernel(q_ref, kc_ref, kp_ref, vc_ref, vp_ref, bias_ref, bias0_ref,
                            o_ref, l_ref, kbuf, vbuf):
    tl = q_ref.shape[0]
    blk = BAND_BLOCK
    kbuf[0:blk, :] = kp_ref[...]
    kbuf[blk:, :] = kc_ref[...]
    vbuf[0:blk, :] = vp_ref[...]
    vbuf[blk:, :] = vc_ref[...]
    bias = bias_ref[...]
    bias_first = jnp.where(pl.program_id(2) == 0, bias0_ref[...], bias)
    lane = lax.broadcasted_iota(jnp.int32, (blk, LANES), 1)
    for i in range(tl // blk):
        rows = slice(i * blk, (i + 1) * blk)
        keys = slice(i * blk, (i + 2) * blk)
        l_tile = jnp.zeros((blk, LANES), F32)
        for h in range(B_HEADS):
            hs = slice(h * B_HEAD_DIM, (h + 1) * B_HEAD_DIM)
            s = _dot_nt(q_ref[rows, hs], kbuf[keys, hs]) + (bias_first if i == 0 else bias)
            p = jnp.exp2(s)
            o_ref[rows, hs] = _dot(p.astype(BF16), vbuf[keys, hs]).astype(o_ref.dtype)
            l_tile = jnp.where(lane == h, jnp.sum(p, axis=-1, keepdims=True), l_tile)
        l_ref[rows, :] = l_tile


def _band_shift_biases(shift_log2):
    qi = np.arange(BAND_BLOCK)[:, None]
    kk = np.arange(2 * BAND_BLOCK)[None, :]
    band = (kk >= qi) & (kk <= qi + BAND_BLOCK)
    band0 = band & (kk >= BAND_BLOCK)
    neg = jnp.full((BAND_BLOCK, 2 * BAND_BLOCK), NEG_INF, F32)
    return jnp.where(band, -shift_log2, neg), jnp.where(band0, -shift_log2, neg)


def _dilated_shifted(q_g, kv_d, d, batch, seq, biases):
    length = seq // d
    tl = min(1024, length)
    width = B_HEADS * B_HEAD_DIM
    sub = tl // BAND_BLOCK

    def cur(part):
        return pl.BlockSpec((None, None, tl, width), lambda b, r, n: (b, r, n, part))

    def prev(part):
        return pl.BlockSpec((None, None, BAND_BLOCK, width),
                            lambda b, r, n: (b, r, jnp.maximum(n * sub - 1, 0), part))

    bias = pl.BlockSpec((BAND_BLOCK, 2 * BAND_BLOCK), lambda b, r, n: (0, 0))
    return pl.pallas_call(
        _dilated_shifted_kernel,
        grid=(batch, d, length // tl),
        in_specs=[cur(0), cur(0), prev(0), cur(1), prev(1), bias, bias],
        out_specs=[cur(0), pl.BlockSpec((None, None, tl, LANES), lambda b, r, n: (b, r, n, 0))],
        out_shape=[jax.ShapeDtypeStruct((batch, d, length, width), BF16),
                   jax.ShapeDtypeStruct((batch, d, length, LANES), F32)],
        scratch_shapes=[pltpu.VMEM((tl + BAND_BLOCK, width), BF16)] * 2,
        compiler_params=_params(("parallel", "parallel", "arbitrary")),
        name=f"dilated_shifted_d{d}",
    )(q_g, kv_d, kv_d, kv_d, kv_d, *biases)


def _b_out_sum_kernel(dils, h_ref, *rest):
    n = len(dils)
    a_refs, l_refs = rest[:n], rest[n:2 * n]
    w_ref, out_ref, acc_ref, den_ref, merged_ref = rest[2 * n:]
    tm = h_ref.shape[0]
    for g, d in enumerate(dils):
        for r in range(d):
            idx = pl.ds(r, tm // d, stride=d)
            planes = [(den_ref, (idx, slice(None)), l_refs[g][r])]
            planes += [(acc_ref, (h, idx, slice(None)),
                        a_refs[g][r, :, h * B_HEAD_DIM:(h + 1) * B_HEAD_DIM].astype(F32))
                       for h in range(B_HEADS)]
            for ref, at, val in planes:
                ref[at] = val if g == 0 else ref[at] + val
    den = den_ref[...]
    for h in range(B_HEADS):
        hs = slice(h * B_HEAD_DIM, (h + 1) * B_HEAD_DIM)
        merged_ref[:, hs] = (acc_ref[h] / den[:, h:h + 1]).astype(BF16)
    out_ref[...] = h_ref[...] + _dot(merged_ref[...], w_ref[...])


def _b_out_sum(h2d, batch, seq, accs, dens, w_out):
    tm = ROW_TILE
    per_seq = seq // tm
    dils = B_DILATIONS
    tile = pl.BlockSpec((tm, D_MODEL), lambda b, i: (b * per_seq + i, 0))
    res = lambda d, width: pl.BlockSpec((None, d, tm // d, width), lambda b, i: (b, 0, i, 0))
    return pl.pallas_call(
        functools.partial(_b_out_sum_kernel, dils),
        grid=(batch, per_seq),
        in_specs=([tile] + [res(d, D_MODEL) for d in dils] + [res(d, LANES) for d in dils]
                  + [pl.BlockSpec((D_MODEL, D_MODEL), lambda b, i: (0, 0))]),
        out_specs=tile,
        out_shape=jax.ShapeDtypeStruct(h2d.shape, F32),
        scratch_shapes=[pltpu.VMEM((B_HEADS, tm, B_HEAD_DIM), F32), pltpu.VMEM((tm, LANES), F32),
                        pltpu.VMEM((tm, D_MODEL), BF16)],
        compiler_params=_params(("parallel", "parallel")),
        name="b_out_sum",
    )(h2d, *accs, *dens, w_out.astype(BF16))


def _route(logits):
    tm = logits.shape[0]
    lane = lax.broadcasted_iota(jnp.int32, (tm, LANES), 1)
    lanef = lane.astype(F32)
    far = float(LANES)
    is_g = lane < MOE_GROUPS
    g_max = jnp.max(jnp.where(is_g, logits, NEG_INF), axis=-1, keepdims=True)
    g_sum = jnp.sum(jnp.where(is_g, jnp.exp(logits - g_max), 0.0), axis=-1, keepdims=True)
    g_idx = jnp.min(jnp.where(is_g, jnp.where(logits == g_max, lanef, far), far), axis=-1, keepdims=True)
    per_group = MOE_EXPERTS // MOE_GROUPS
    e_lo = ROUTE_E0 + g_idx * per_group
    in_group = jnp.where(lanef >= e_lo, jnp.where(lanef < e_lo + per_group, 1.0, 0.0), 0.0)
    cand1 = jnp.where(in_group > 0.0, logits, NEG_INF)
    v1 = jnp.max(cand1, axis=-1, keepdims=True)
    i1 = jnp.min(jnp.where(cand1 == v1, lanef, far), axis=-1, keepdims=True)
    cand2 = jnp.where(lanef == i1, NEG_INF, cand1)
    v2 = jnp.max(cand2, axis=-1, keepdims=True)
    i2 = jnp.min(jnp.where(cand2 == v2, lanef, far), axis=-1, keepdims=True)
    e21 = jnp.exp(v2 - v1)
    w1 = 1.0 / (1.0 + e21)
    w2 = e21 / (1.0 + e21)
    comb = (jnp.where(lanef == i1, w1, 0.0) + jnp.where(lanef == i2, w2, 0.0)) / g_sum
    return comb, jnp.where(lanef == g_idx, 1.0, 0.0)


def _moe_kernel(h_ref, g_ref, wr12_ref, wr1_ref, br_ref, tri_ref, wgu_ref, wd_ref, out_ref,
                xn_ref, c1_ref, c2_ref, posc_ref, posr_ref, hid_ref, meta_ref):
    grp = pl.program_id(1)
    per_group = MOE_EXPERTS // MOE_GROUPS
    tm = h_ref.shape[0]
    ch = MOE_CHUNK

    @pl.when(grp == 0)
    def _():
        xn = _rms_rows(h_ref[...], g_ref[...])
        x1 = xn.astype(BF16)
        x2 = (xn - x1.astype(F32)).astype(BF16)
        first = _dot(x1, wr12_ref[...])
        logits = first[:, :LANES] + (first[:, LANES:] + _dot(x2, wr1_ref[...])) + br_ref[...]
        comb, onehot = _route(logits)
        c1 = comb.astype(BF16)
        c1_ref[...] = c1
        c2_ref[...] = (comb - c1.astype(F32)).astype(BF16)
        xn_ref[...] = x1
        count = _dot(tri_ref[...], onehot.astype(BF16))
        half = ch // 2
        padded = jnp.floor((count[tm - 1:tm, :] + (half - 1)) * (1.0 / half)) * half
        lane1 = lax.broadcasted_iota(jnp.int32, (1, LANES), 1)
        prev = jnp.where(lane1 >= 1, pltpu.roll(padded, 1, 1), 0.0)
        start = (prev + jnp.where(lane1 >= 2, pltpu.roll(prev, 1, 1), 0.0)
                 + jnp.where(lane1 >= 3, pltpu.roll(prev, 2, 1), 0.0))
        slot = jnp.sum(onehot * (start + count - 1.0), axis=-1, keepdims=True)
        slot_b = jnp.broadcast_to(slot, (tm, LANES))
        posc_ref[...] = slot_b
        posr_ref[...] = slot_b.T[0:8, :]
        for g in range(MOE_GROUPS):
            pick = lane1 == g
            size = jnp.sum(jnp.where(pick, padded, 0.0)).astype(jnp.int32)
            meta_ref[g] = size // ch
            meta_ref[MOE_GROUPS + g] = jnp.sum(jnp.where(pick, start, 0.0)).astype(jnp.int32)
            meta_ref[2 * MOE_GROUPS + g] = (size % ch) // half
        out_ref[...] = h_ref[...]

    def run_chunk(first, size):
        first_slot = first.astype(F32)
        lane = lax.broadcasted_iota(jnp.int32, (size, LANES), 1)
        rows = lax.broadcasted_iota(jnp.int32, (size, tm), 0).astype(F32) + first_slot
        gather = jnp.where(posr_ref[0:1, :] == rows, 1.0, 0.0).astype(BF16)
        xs = _dot(gather, xn_ref[...]).astype(BF16)
        cs = _dot(gather, c1_ref[...]) + _dot(gather, c2_ref[...])
        for e in range(per_group):
            gate_w = jnp.sum(jnp.where(lane == grp * per_group + (ROUTE_E0 + e), cs, 0.0),
                             axis=-1, keepdims=True)
            gu = _dot(xs, wgu_ref[e])
            g, u = gu[:, :MOE_FF], gu[:, MOE_FF:]
            hid_ref[0:size, e * MOE_FF:(e + 1) * MOE_FF] = (
                (g / (1.0 + jnp.exp(-g))) * u * gate_w).astype(BF16)
        y = _dot(hid_ref[0:size, :], wd_ref[...]).astype(BF16)
        cols = lax.broadcasted_iota(jnp.int32, (tm, size), 1).astype(F32) + first_slot
        scatter = jnp.where(posc_ref[:, 0:1] == cols, 1.0, 0.0).astype(BF16)
        out_ref[...] += _dot(scatter, y)

    base = meta_ref[MOE_GROUPS + grp]
    full_chunks = meta_ref[grp]

    @pl.loop(0, full_chunks)
    def _(c):
        run_chunk(base + c * ch, ch)

    @pl.when(meta_ref[2 * MOE_GROUPS + grp] == 1)
    def _():
        run_chunk(base + full_chunks * ch, ch // 2)


def _moe_weights(group_w, group_b, expert_w, expert_b, w_gate, w_up, w_down):
    pad = LANES - MOE_GROUPS - MOE_EXPERTS
    w_r = jnp.pad(jnp.concatenate([group_w, expert_w], axis=-1), ((0, 0), (0, 0), (0, pad)))
    r1 = w_r.astype(BF16)
    r2 = (w_r - r1.astype(F32)).astype(BF16)
    b_r = jnp.pad(jnp.concatenate([group_b, expert_b], axis=-1), ((0, 0), (0, pad)))[:, None, :]
    layers = w_gate.shape[0]
    per_group = MOE_EXPERTS // MOE_GROUPS
    w_gu = jnp.concatenate([w_gate, w_up], axis=-1).astype(BF16).reshape(
        layers, MOE_GROUPS, per_group, D_MODEL, 2 * MOE_FF)
    w_d = w_down.astype(BF16).reshape(layers, MOE_GROUPS, per_group * MOE_FF, D_MODEL)
    return jnp.concatenate([r1, r2], axis=-1), r1, b_r, w_gu, w_d


def _moe(h2d, norm_gain, layer, weights):
    wr12, wr1, b_r, w_gu, w_d = weights
    t = h2d.shape[0]
    tm = WIDE_ROW_TILE
    per_group = MOE_EXPERTS // MOE_GROUPS
    tile = pl.BlockSpec((tm, D_MODEL), lambda i, g: (i, 0))
    of_layer = lambda shape: pl.BlockSpec((None,) + shape, lambda i, g: (layer,) + (0,) * len(shape))
    tri = jnp.asarray(np.tril(np.ones((tm, tm), np.float32)), BF16)
    return pl.pallas_call(
        _moe_kernel,
        grid=(t // tm, MOE_GROUPS),
        in_specs=[tile, pl.BlockSpec((1, D_MODEL), lambda i, g: (0, 0)),
                  of_layer((D_MODEL, 2 * LANES)), of_layer((D_MODEL, LANES)), of_layer((1, LANES)),
                  pl.BlockSpec((tm, tm), lambda i, g: (0, 0)),
                  pl.BlockSpec((None, None, per_group, D_MODEL, 2 * MOE_FF), lambda i, g: (layer, g, 0, 0, 0)),
                  pl.BlockSpec((None, None, per_group * MOE_FF, D_MODEL), lambda i, g: (layer, g, 0, 0))],
        out_specs=tile,
        out_shape=jax.ShapeDtypeStruct((t, D_MODEL), F32),
        scratch_shapes=[pltpu.VMEM((tm, D_MODEL), BF16), pltpu.VMEM((tm, LANES), BF16),
                        pltpu.VMEM((tm, LANES), BF16), pltpu.VMEM((tm, LANES), F32),
                        pltpu.VMEM((8, tm), F32), pltpu.VMEM((MOE_CHUNK, per_group * MOE_FF), BF16),
                        pltpu.SMEM((3 * MOE_GROUPS,), jnp.int32)],
        compiler_params=_params(("parallel", "arbitrary")),
        name="moe",
    )(h2d, norm_gain.reshape(1, D_MODEL), wr12, wr1, b_r, tri, w_gu, w_d)


def kernel(x, positions, a_norm, a_w_in, a_b_f, a_q_gain, a_k_gain, a_w_out, kv_norm, kv_w, kv_k_gain, b_norm, b_w_q, b_q_gain, b_w_out, ffn_norm, moe_group_w, moe_group_b, moe_expert_w, moe_expert_b, moe_w_gate, moe_w_up, moe_w_down):
    batch, seq, _ = x.shape
    n_a = a_norm.shape[0]
    n_b = b_norm.shape[0]
    b_width = B_HEADS * B_HEAD_DIM
    n_groups = len(B_DILATIONS)
    chunks = b_width // MXU_DIM

    a_shifts = [(A_HEAD_DIM ** 0.5) * jnp.max(jnp.abs(a_q_gain[i])) * jnp.max(jnp.abs(a_k_gain[i]))
                for i in range(n_a)]
    b_shifts = [(B_HEAD_DIM ** 0.5) * jnp.max(jnp.abs(b_q_gain[j])) * jnp.max(jnp.abs(kv_k_gain))
                for j in range(n_b)]

    moe_weights = _moe_weights(moe_group_w, moe_group_b, moe_expert_w, moe_expert_b,
                               moe_w_gate, moe_w_up, moe_w_down)

    def trunk(fixed_shift, x):
        h = x.reshape(batch * seq, D_MODEL)
        kv_sh = cos = sin = None
        for layer in range(n_a + n_b):
            if layer < n_a:
                i = layer
                q, k, vt = _a_proj(h, seq, a_norm[i], a_w_in[i], a_b_f[i], a_q_gain[i], a_k_gain[i],
                                   a_shifts[i])
                o = _fox_attention(q, k, vt, batch, seq, fixed_shift)
                h = _out_proj(h, o.reshape(batch * seq, D_MODEL), a_w_out[i])
            else:
                j = layer - n_a
                k_gain_row = jnp.concatenate(
                    [jnp.tile(kv_k_gain, B_HEADS), jnp.ones((b_width,), F32)]).reshape(1, 2 * b_width)
                q_gain_row = (jnp.tile(b_q_gain[j], (1, B_HEADS)) * (B_HEAD_DIM ** -0.5)).reshape(
                    1, n_groups * b_width)
                if j == 0:
                    cos, sin = _rope_tables(positions)
                if fixed_shift:
                    if j == 0:
                        kv_plan = tuple((ch < chunks, tuple((g, d, ch * MXU_DIM) for g, d in enumerate(B_DILATIONS)))
                                        for ch in range(2 * chunks))
                        kv_sh = _b_proj_res(h, batch, seq, kv_norm, kv_w, k_gain_row, kv_plan,
                                            [(d, 2 * b_width) for d in B_DILATIONS], cos, sin)
                    q_plan = tuple((True, ((ch // chunks, B_DILATIONS[ch // chunks], (ch % chunks) * MXU_DIM),))
                                   for ch in range(n_groups * chunks))
                    q_res = _b_proj_res(h, batch, seq, b_norm[j], b_w_q[j], q_gain_row * LOG2E, q_plan,
                                        [(d, b_width) for d in B_DILATIONS], cos, sin)
                    biases = _band_shift_biases(b_shifts[j] * LOG2E)
                    accs, dens = zip(*[_dilated_shifted(q_res[g], kv_sh[g], d, batch, seq, biases)
                                       for g, d in enumerate(B_DILATIONS)])
                    h = _b_out_sum(h, batch, seq, accs, dens, b_w_out[j])
                else:
                    if j == 0:
                        kv_sh = _b_proj(h, kv_norm, kv_w, k_gain_row, b_width, cos, sin)
                    q_all = _b_proj(h, b_norm[j], b_w_q[j], q_gain_row, n_groups * b_width, cos, sin)
                    outs, lses = zip(*[_dilated_group(q_all, kv_sh, g, batch, seq) for g in range(n_groups)])
                    h = _b_out(h, outs, lses, b_w_out[j])
            h = _moe(h, ffn_norm[layer], layer, moe_weights)
        return h.reshape(batch, seq, D_MODEL)

    largest = functools.reduce(jnp.maximum, a_shifts + b_shifts)
    return lax.cond(largest <= MAX_SAFE_SHIFT, functools.partial(trunk, True),
                    functools.partial(trunk, False), x)
```

```python
import functools
import math

import numpy as np
import jax
import jax.numpy as jnp
from jax import lax
from jax.experimental import pallas as pl
from jax.experimental.pallas import tpu as pltpu

F32 = jnp.float32
BF16 = jnp.bfloat16

D_MODEL = 1024
EPS = 1e-6
ROPE_THETA = 10000.0
A_HEADS = 16
A_HEAD_DIM = 64
B_HEADS = 8
B_HEAD_DIM = 128
B_WINDOWS = (128, 512, 2048)
B_DILATIONS = (1, 4, 16)
BAND_BLOCK = 128
assert all(w // d == BAND_BLOCK for w, d in zip(B_WINDOWS, B_DILATIONS))
MOE_GROUPS = 4
MOE_EXPERTS = 16
MOE_FF = 256
MOE_CHUNK = 256

ROW_TILE = 512
WIDE_ROW_TILE = 1024
LANES = 128
MXU_DIM = 256
VMEM_LIMIT = 56 * 1024 * 1024

NEG_INF = float("-inf")
LOG2E = math.log2(math.e)
FOX_BLOCK = 512
FOX_UNROLL = 4
MAX_SAFE_SHIFT = 40.0

X_BASE = A_HEAD_DIM
ROUTE_E0 = MOE_GROUPS


def _params(sem):
    return pltpu.CompilerParams(dimension_semantics=sem, vmem_limit_bytes=VMEM_LIMIT)


def _rms_rows(x, gain_row):
    ms = jnp.mean(x * x, axis=-1, keepdims=True)
    return x * lax.rsqrt(ms + EPS) * gain_row


def _dot(a, b):
    return jnp.dot(a, b, preferred_element_type=F32)


def _dot_nt(a, b):
    return lax.dot_general(a, b, (((1,), (1,)), ((), ())), preferred_element_type=F32)


def _split3(x):
    p1 = x.astype(BF16)
    r1 = x - p1.astype(F32)
    p2 = r1.astype(BF16)
    p3 = (r1 - p2.astype(F32)).astype(BF16)
    return p1, p2, p3


def _seg_norm(x, ones_bd, seg):
    ss = _dot((x * x).astype(BF16), ones_bd)
    return x * lax.rsqrt(ss * (1.0 / seg) + EPS)


def _rope_table_kernel(pos_ref, inv_ref, sign_ref, cos_ref, sin_ref):
    ang = pos_ref[...].astype(F32) * inv_ref[...]
    cos_ref[...] = jnp.cos(ang)
    sin_ref[...] = jnp.sin(ang) * sign_ref[...]


def _rope_tables(positions):
    t = positions.size
    tm = WIDE_ROW_TILE
    half = B_HEAD_DIM // 2
    inv = ROPE_THETA ** (-jnp.arange(half, dtype=F32) / half)
    inv = jnp.concatenate([inv, inv]).reshape(1, B_HEAD_DIM)
    sign = jnp.concatenate([-jnp.ones((half,), F32), jnp.ones((half,), F32)]).reshape(1, B_HEAD_DIM)
    row = pl.BlockSpec((1, B_HEAD_DIM), lambda i: (0, 0))
    tab = pl.BlockSpec((tm, B_HEAD_DIM), lambda i: (i, 0))
    return pl.pallas_call(
        _rope_table_kernel,
        grid=(t // tm,),
        in_specs=[pl.BlockSpec((tm, 1), lambda i: (i, 0)), row, row],
        out_specs=[tab, tab],
        out_shape=[jax.ShapeDtypeStruct((t, B_HEAD_DIM), F32)] * 2,
        compiler_params=_params(("parallel",)),
        name="rope_tables",
    )(positions.reshape(t, 1), inv, sign)


def _rope(x, cos, sin_signed):
    return x * cos + pltpu.roll(x, B_HEAD_DIM // 2, 1) * sin_signed


def _a_proj_kernel(tiles_per_seq, h_ref, g_ref, wqkv_ref, wf_ref, bf_ref, qg_ref, kg_ref,
                   ones_ref, tri_ref, selq_ref, selk_ref, shift_ref, q_ref, k_ref, vt_ref, carry_ref):
    tm = h_ref.shape[0]

    @pl.when(pl.program_id(0) % tiles_per_seq == 0)
    def _():
        carry_ref[...] = jnp.zeros_like(carry_ref)

    xn = _rms_rows(h_ref[...], g_ref[...]).astype(BF16)

    fl = _dot(xn, wf_ref[...]) + bf_ref[...]
    lf = jnp.minimum(fl, 0.0) - jnp.log(1.0 + jnp.exp(-jnp.abs(fl)))
    tri = tri_ref[...]
    l1, l2, l3 = _split3(lf)
    c = _dot(tri, l1) + _dot(tri, l2) + _dot(tri, l3) + carry_ref[0:1, :]
    carry_ref[0:1, :] = c[tm - 1:tm, :]

    lane = lax.broadcasted_iota(jnp.int32, (tm, LANES), 1)
    ck = c * LOG2E
    cq = ck - shift_ref[...]
    pieces = [p.astype(F32) for p in _split3(cq) + _split3(ck)]
    e = jnp.where(lane == 6 * A_HEADS, 1.0, 0.0)
    for n in reversed(range(6)):
        piece = pieces[n] if n == 0 else pltpu.roll(pieces[n], n * A_HEADS, 1)
        e = jnp.where((lane >= n * A_HEADS) & (lane < (n + 1) * A_HEADS), piece, e)
    e = e.astype(BF16)
    v_extra = jnp.where(lane == X_BASE, 1.0, 0.0)
    low = lane < A_HEAD_DIM
    tkv = vt_ref.shape[-1]

    ones_bd = ones_ref[...]
    n_chunks = D_MODEL // MXU_DIM
    for ch in range(n_chunks):
        cs = slice(ch * MXU_DIM, (ch + 1) * MXU_DIM)
        qf = _dot(xn, wqkv_ref[:, cs])
        kf = _dot(xn, wqkv_ref[:, D_MODEL + ch * MXU_DIM:D_MODEL + (ch + 1) * MXU_DIM])
        vf = _dot(xn, wqkv_ref[:, 2 * D_MODEL + ch * MXU_DIM:2 * D_MODEL + (ch + 1) * MXU_DIM])
        qn = _seg_norm(qf, ones_bd, A_HEAD_DIM) * qg_ref[:, cs]
        kn = _seg_norm(kf, ones_bd, A_HEAD_DIM) * kg_ref[:, cs]
        heads_per_chunk = MXU_DIM // A_HEAD_DIM
        hs = slice(ch * heads_per_chunk * LANES, (ch + 1) * heads_per_chunk * LANES)
        exq = _dot(e, selq_ref[:, hs])
        exk = _dot(e, selk_ref[:, hs])
        for j in range(heads_per_chunk):
            pair = slice((j // 2) * LANES, (j // 2 + 1) * LANES)
            blk = slice(j * LANES, (j + 1) * LANES)
            out = slice((ch * heads_per_chunk + j) * LANES, (ch * heads_per_chunk + j + 1) * LANES)
            qp, kp, vp = qn[:, pair], kn[:, pair], vf[:, pair]
            if j % 2 == 1:
                qp = pltpu.roll(qp, A_HEAD_DIM, 1)
                kp = pltpu.roll(kp, A_HEAD_DIM, 1)
                vp = pltpu.roll(vp, A_HEAD_DIM, 1)
            q_ref[:, out] = jnp.where(low, qp, exq[:, blk]).astype(BF16)
            k_ref[:, out] = jnp.where(low, kp, exk[:, blk]).astype(BF16)
            v_aug = jnp.where(low, vp, v_extra)
            for cc in range(tm // tkv):
                vt_ref[cc, ch * heads_per_chunk + j] = (
                    v_aug[cc * tkv:(cc + 1) * tkv, :].T.astype(BF16))


def _a_sel_matrices():
    selq = np.zeros((LANES, A_HEADS * LANES), np.float32)
    selk = np.zeros((LANES, A_HEADS * LANES), np.float32)
    ones_lane = 6 * A_HEADS
    for h in range(A_HEADS):
        base = h * LANES + X_BASE
        for piece in range(3):
            selq[A_HEADS * piece + h, base + piece] = 1.0
            selq[ones_lane, base + 3 + piece] = 1.0
            selk[ones_lane, base + piece] = 1.0
            selk[A_HEADS * (3 + piece) + h, base + 3 + piece] = -1.0
    return jnp.asarray(selq, BF16), jnp.asarray(selk, BF16)


def _block_diag_ones(seg):
    idx = np.arange(MXU_DIM) // seg
    return jnp.asarray((idx[:, None] == idx[None, :]).astype(np.float32), BF16)


def _a_proj(h2d, seq, norm_gain, w_in, b_f, q_gain, k_gain, shift):
    t = h2d.shape[0]
    tm = ROW_TILE
    wide = A_HEADS * LANES
    w_qkv = w_in[:, :3 * D_MODEL].astype(BF16)
    w_f = jnp.pad(w_in[:, 3 * D_MODEL:], ((0, 0), (0, LANES - A_HEADS))).astype(BF16)
    b_row = jnp.pad(b_f, (0, LANES - A_HEADS)).reshape(1, LANES)
    qg = (jnp.tile(q_gain, A_HEADS) * (A_HEAD_DIM ** -0.5 * LOG2E)).reshape(1, D_MODEL)
    kg = jnp.tile(k_gain, A_HEADS).reshape(1, D_MODEL)
    tri = jnp.asarray(np.tril(np.ones((tm, tm), np.float32)), BF16)
    selq, selk = _a_sel_matrices()
    shift_row = jnp.full((1, LANES), LOG2E, F32) * shift
    full = lambda shape: pl.BlockSpec(shape, lambda i: (0,) * len(shape))
    out_spec = pl.BlockSpec((tm, wide), lambda i: (i, 0))
    per_tile = tm // FOX_BLOCK
    return pl.pallas_call(
        functools.partial(_a_proj_kernel, seq // tm),
        grid=(t // tm,),
        in_specs=[pl.BlockSpec((tm, D_MODEL), lambda i: (i, 0)),
                  full((1, D_MODEL)), full((D_MODEL, 3 * D_MODEL)), full((D_MODEL, LANES)),
                  full((1, LANES)), full((1, D_MODEL)), full((1, D_MODEL)),
                  full((MXU_DIM, MXU_DIM)), full((tm, tm)), full((LANES, wide)), full((LANES, wide)),
                  full((1, LANES))],
        out_specs=[out_spec, out_spec,
                   pl.BlockSpec((per_tile, A_HEADS, LANES, FOX_BLOCK), lambda i: (i, 0, 0, 0))],
        out_shape=[jax.ShapeDtypeStruct((t, wide), BF16), jax.ShapeDtypeStruct((t, wide), BF16),
                   jax.ShapeDtypeStruct((t // FOX_BLOCK, A_HEADS, LANES, FOX_BLOCK), BF16)],
        scratch_shapes=[pltpu.VMEM((8, LANES), F32)],
        compiler_params=_params(("arbitrary",)),
        name="a_proj",
    )(h2d, norm_gain.reshape(1, D_MODEL), w_qkv, w_f, b_row, qg, kg,
      _block_diag_ones(A_HEAD_DIM), tri, selq, selk, shift_row)


def _fox_pair_output(accs_t, o_ref, rows):
    outs = [(a / a[X_BASE:X_BASE + 1, :]).T for a in accs_t]
    lane = lax.broadcasted_iota(jnp.int32, outs[0].shape, 1)
    o_ref[rows, :] = jnp.where(lane < A_HEAD_DIM, outs[0],
                               pltpu.roll(outs[1], A_HEAD_DIM, 1)).astype(o_ref.dtype)


def _fox_shifted_kernel(q_ref, k_ref, vt_ref, o_ref, acc_ref):
    blk = FOX_BLOCK
    row = lax.broadcasted_iota(jnp.int32, (blk, blk), 0)
    col = lax.broadcasted_iota(jnp.int32, (blk, blk), 1)
    key_visible = row <= col

    def q_tile(i, _):
        rows = pl.ds(pl.multiple_of(i * blk, blk), blk)
        qs = [q_ref[rows, hh * LANES:(hh + 1) * LANES] for hh in range(2)]

        def block(j, masked=False):
            keys = pl.ds(pl.multiple_of(j * blk, blk), blk)
            ss = [_dot_nt(k_ref[keys, hh * LANES:(hh + 1) * LANES], qs[hh]) for hh in range(2)]
            for hh in range(2):
                p = jnp.exp2(ss[hh])
                if masked:
                    p = jnp.where(key_visible, p, 0.0)
                acc_ref[hh] += _dot(vt_ref[j, hh], p.astype(BF16))

        acc_ref[...] = jnp.zeros_like(acc_ref)

        @pl.loop(0, i // FOX_UNROLL)
        def _(t):
            for u in range(FOX_UNROLL):
                block(FOX_UNROLL * t + u)

        left = i % FOX_UNROLL
        for n_left in range(FOX_UNROLL):
            @pl.when(left == n_left)
            def _():
                for u in range(n_left):
                    block(i - n_left + u)
                block(i, masked=True)

        _fox_pair_output((acc_ref[0], acc_ref[1]), o_ref, rows)
        return 0

    lax.fori_loop(0, q_ref.shape[0] // blk, q_tile, 0)


def _fox_online_kernel(q_ref, k_ref, vt_ref, o_ref):
    blk = FOX_BLOCK
    qi = pl.program_id(2)
    qs = [q_ref[:, hh * LANES:(hh + 1) * LANES] for hh in range(2)]
    row = lax.broadcasted_iota(jnp.int32, (blk, blk), 0)
    col = lax.broadcasted_iota(jnp.int32, (blk, blk), 1)
    key_visible = row <= col

    def block(j, carry, masked):
        keys = pl.ds(pl.multiple_of(j * blk, blk), blk)
        new = []
        for hh in range(2):
            m, acc = carry[2 * hh], carry[2 * hh + 1]
            s = _dot_nt(k_ref[keys, hh * LANES:(hh + 1) * LANES], qs[hh])
            if masked:
                s = jnp.where(key_visible, s, NEG_INF)
            m_new = jnp.maximum(m, jnp.max(s, axis=0, keepdims=True))
            p = jnp.exp2(s - m_new)
            acc = jnp.exp2(m - m_new) * acc + _dot(vt_ref[j, hh], p.astype(BF16))
            new += [m_new, acc]
        return tuple(new)

    init = (jnp.full((1, blk), NEG_INF, F32), jnp.zeros((LANES, blk), F32)) * 2
    carry = lax.fori_loop(0, qi, functools.partial(block, masked=False), init)
    carry = block(qi, carry, True)
    _fox_pair_output((carry[1], carry[3]), o_ref, slice(None))


def _fox_attention(q, k, vt, batch, seq, fixed_shift):
    pairs = A_HEADS // 2
    blk = FOX_BLOCK
    q3, k3 = (a.reshape(batch, seq, A_HEADS * LANES) for a in (q, k))
    vt5 = vt.reshape(batch, seq // blk, A_HEADS, LANES, blk)
    out_shape = jax.ShapeDtypeStruct((batch, seq, D_MODEL), BF16)

    def shifted(q3, k3, vt5):
        pair = pl.BlockSpec((None, seq, 2 * LANES), lambda b, p: (b, 0, p))
        return pl.pallas_call(
            _fox_shifted_kernel,
            grid=(batch, pairs),
            in_specs=[pair, pair,
                      pl.BlockSpec((None, seq // blk, 2, LANES, blk), lambda b, p: (b, 0, p, 0, 0))],
            out_specs=pl.BlockSpec((None, seq, LANES), lambda b, p: (b, 0, p)),
            out_shape=out_shape,
            scratch_shapes=[pltpu.VMEM((2, LANES, blk), F32)],
            compiler_params=_params(("parallel", "parallel")),
            name="fox_shifted",
        )(q3, k3, vt5)

    def online(q3, k3, vt5):
        return pl.pallas_call(
            _fox_online_kernel,
            grid=(batch, pairs, seq // blk),
            in_specs=[pl.BlockSpec((None, blk, 2 * LANES), lambda b, p, i: (b, i, p)),
                      pl.BlockSpec((None, seq, 2 * LANES), lambda b, p, i: (b, 0, p)),
                      pl.BlockSpec((None, seq // blk, 2, LANES, blk), lambda b, p, i: (b, 0, p, 0, 0))],
            out_specs=pl.BlockSpec((None, blk, LANES), lambda b, p, i: (b, i, p)),
            out_shape=out_shape,
            compiler_params=_params(("parallel", "parallel", "arbitrary")),
            name="fox_online",
        )(q3, k3, vt5)

    return shifted(q3, k3, vt5) if fixed_shift else online(q3, k3, vt5)


def _out_proj_kernel(h_ref, o_ref, w_ref, out_ref):
    out_ref[...] = h_ref[...] + _dot(o_ref[...], w_ref[...])


def _out_proj(h2d, o2d, w_out):
    t = h2d.shape[0]
    tm = WIDE_ROW_TILE
    tile = pl.BlockSpec((tm, D_MODEL), lambda i: (i, 0))
    return pl.pallas_call(
        _out_proj_kernel,
        grid=(t // tm,),
        in_specs=[tile, tile, pl.BlockSpec((D_MODEL, D_MODEL), lambda i: (0, 0))],
        out_specs=tile,
        out_shape=jax.ShapeDtypeStruct((t, D_MODEL), F32),
        compiler_params=_params(("parallel",)),
        name="out_proj",
    )(h2d, o2d, w_out.astype(BF16))


def _b_proj_kernel(n_rope_cols, h_ref, g_ref, w_ref, gain_ref, ones_ref, cos_ref, sin_ref, out_ref):
    xn = _rms_rows(h_ref[...], g_ref[...]).astype(BF16)
    cos, sin = cos_ref[...], sin_ref[...]
    ones_bd = ones_ref[...]
    for ch in range(w_ref.shape[1] // MXU_DIM):
        cs = slice(ch * MXU_DIM, (ch + 1) * MXU_DIM)
        y = _dot(xn, w_ref[:, cs])
        if ch * MXU_DIM < n_rope_cols:
            y = _seg_norm(y, ones_bd, B_HEAD_DIM) * gain_ref[:, cs]
            for j in range(MXU_DIM // B_HEAD_DIM):
                blk = slice(j * B_HEAD_DIM, (j + 1) * B_HEAD_DIM)
                out_ref[:, ch * MXU_DIM + j * B_HEAD_DIM:ch * MXU_DIM + (j + 1) * B_HEAD_DIM] = (
                    _rope(y[:, blk], cos, sin).astype(out_ref.dtype))
        else:
            out_ref[:, cs] = y.astype(out_ref.dtype)


def _b_proj(h2d, norm_gain, w, head_gain_row, n_rope_cols, cos, sin):
    t = h2d.shape[0]
    n = w.shape[1]
    tm = ROW_TILE
    full = lambda shape: pl.BlockSpec(shape, lambda i: (0,) * len(shape))
    return pl.pallas_call(
        functools.partial(_b_proj_kernel, n_rope_cols),
        grid=(t // tm,),
        in_specs=[pl.BlockSpec((tm, D_MODEL), lambda i: (i, 0)), full((1, D_MODEL)),
                  full((D_MODEL, n)), full((1, n)), full((MXU_DIM, MXU_DIM)),
                  pl.BlockSpec((tm, B_HEAD_DIM), lambda i: (i, 0)),
                  pl.BlockSpec((tm, B_HEAD_DIM), lambda i: (i, 0))],
        out_specs=pl.BlockSpec((tm, n), lambda i: (i, 0)),
        out_shape=jax.ShapeDtypeStruct((t, n), BF16),
        compiler_params=_params(("parallel",)),
        name="b_proj",
    )(h2d, norm_gain.reshape(1, D_MODEL), w.astype(BF16), head_gain_row,
      _block_diag_ones(B_HEAD_DIM), cos, sin)


def _dilated_kernel(q_ref, kc_ref, kp_ref, vc_ref, vp_ref, bp_ref, bc_ref, o_ref, lse_ref):
    tl = q_ref.shape[0]
    blk = BAND_BLOCK
    first_pen = jnp.where(pl.program_id(2) == 0, NEG_INF, 0.0)
    bias_prev, bias_cur = bp_ref[...], bc_ref[...]
    lane = lax.broadcasted_iota(jnp.int32, (blk, LANES), 1)
    for i in range(tl // blk):
        rows = slice(i * blk, (i + 1) * blk)
        lse_tile = jnp.zeros((blk, LANES), F32)
        for h in range(B_HEADS):
            hs = slice(h * B_HEAD_DIM, (h + 1) * B_HEAD_DIM)
            q = q_ref[rows, hs]
            if i == 0:
                k_prev, v_prev = kp_ref[:, hs], vp_ref[:, hs]
            else:
                prev_rows = slice((i - 1) * blk, i * blk)
                k_prev, v_prev = kc_ref[prev_rows, hs], vc_ref[prev_rows, hs]
            s_prev = _dot_nt(q, k_prev) + bias_prev
            if i == 0:
                s_prev = s_prev + first_pen
            s_cur = _dot_nt(q, kc_ref[rows, hs]) + bias_cur
            m = jnp.maximum(jnp.max(s_prev, axis=-1, keepdims=True),
                            jnp.max(s_cur, axis=-1, keepdims=True))
            p_prev = jnp.exp(s_prev - m)
            p_cur = jnp.exp(s_cur - m)
            l = jnp.sum(p_prev, axis=-1, keepdims=True) + jnp.sum(p_cur, axis=-1, keepdims=True)
            o = (_dot(p_prev.astype(BF16), v_prev) + _dot(p_cur.astype(BF16), vc_ref[rows, hs])) / l
            o_ref[rows, hs] = o.astype(o_ref.dtype)
            lse_tile = jnp.where(lane == h, m + jnp.log(l), lse_tile)
        lse_ref[rows, :] = lse_tile


def _band_biases():
    qi = np.arange(BAND_BLOCK)[:, None]
    kk = np.arange(BAND_BLOCK)[None, :]
    prev = np.where(kk >= qi, 0.0, -np.inf).astype(np.float32)
    cur = np.where(kk <= qi, 0.0, -np.inf).astype(np.float32)
    return jnp.asarray(prev), jnp.asarray(cur)


def _dilated_group(q_all, kv, group, batch, seq):
    d = B_DILATIONS[group]
    assert B_WINDOWS[group] // d == BAND_BLOCK and seq % (d * BAND_BLOCK) == 0
    length = seq // d
    tl = 256
    width = B_HEADS * B_HEAD_DIM
    sub = tl // BAND_BLOCK
    n_groups = len(B_DILATIONS)
    qv = q_all.reshape(batch, length, d * n_groups * width)
    kvv = kv.reshape(batch, length, d * 2 * width)

    def cur(part):
        return pl.BlockSpec((None, tl, width), lambda b, r, n: (b, n, r * 2 + part))

    def prev(part):
        return pl.BlockSpec((None, BAND_BLOCK, width),
                            lambda b, r, n: (b, jnp.maximum(n * sub - 1, 0), r * 2 + part))

    bias = pl.BlockSpec((BAND_BLOCK, BAND_BLOCK), lambda b, r, n: (0, 0))
    bias_prev, bias_cur = _band_biases()
    o, lse = pl.pallas_call(
        _dilated_kernel,
        grid=(batch, d, length // tl),
        in_specs=[pl.BlockSpec((None, tl, width), lambda b, r, n: (b, n, r * n_groups + group)),
                  cur(0), prev(0), cur(1), prev(1), bias, bias],
        out_specs=[pl.BlockSpec((None, tl, width), lambda b, r, n: (b, n, r)),
                   pl.BlockSpec((None, tl, LANES), lambda b, r, n: (b, n, r))],
        out_shape=[jax.ShapeDtypeStruct((batch, length, d * width), BF16),
                   jax.ShapeDtypeStruct((batch, length, d * LANES), F32)],
        compiler_params=_params(("parallel", "parallel", "arbitrary")),
        name=f"dilated_g{group}",
    )(qv, kvv, kvv, kvv, kvv, bias_prev, bias_cur)
    return o.reshape(batch * seq, width), lse.reshape(batch * seq, LANES)


def _b_out_kernel(h_ref, o0_ref, o1_ref, o2_ref, l0_ref, l1_ref, l2_ref, w_ref, out_ref, merged_ref):
    lses = [l0_ref[...], l1_ref[...], l2_ref[...]]
    top = jnp.maximum(jnp.maximum(lses[0], lses[1]), lses[2])
    es = [jnp.exp(x - top) for x in lses]
    den = es[0] + es[1] + es[2]
    ws = [x / den for x in es]
    o_refs = (o0_ref, o1_ref, o2_ref)
    for h in range(B_HEADS):
        hs = slice(h * B_HEAD_DIM, (h + 1) * B_HEAD_DIM)
        acc = ws[0][:, h:h + 1] * o_refs[0][:, hs].astype(F32)
        for g in (1, 2):
            acc = acc + ws[g][:, h:h + 1] * o_refs[g][:, hs].astype(F32)
        merged_ref[:, hs] = acc.astype(BF16)
    out_ref[...] = h_ref[...] + _dot(merged_ref[...], w_ref[...])


def _b_out(h2d, outs, lses, w_out):
    t = h2d.shape[0]
    tm = ROW_TILE
    tile = pl.BlockSpec((tm, D_MODEL), lambda i: (i, 0))
    ltile = pl.BlockSpec((tm, LANES), lambda i: (i, 0))
    return pl.pallas_call(
        _b_out_kernel,
        grid=(t // tm,),
        in_specs=[tile, tile, tile, tile, ltile, ltile, ltile,
                  pl.BlockSpec((D_MODEL, D_MODEL), lambda i: (0, 0))],
        out_specs=tile,
        out_shape=jax.ShapeDtypeStruct((t, D_MODEL), F32),
        scratch_shapes=[pltpu.VMEM((tm, D_MODEL), BF16)],
        compiler_params=_params(("parallel",)),
        name="b_out",
    )(h2d, *outs, *lses, w_out.astype(BF16))


def _b_proj_res_kernel(plan, n_out, n_tiles, h_ref, g_ref, w_ref, gain_ref, cos_ref, sin_ref, *rest):
    out_refs, y_ref = rest[:n_out], rest[n_out]
    tm = h_ref.shape[0]
    step = pl.program_id(0)

    def project(slot):
        xn = _rms_rows(h_ref[...], g_ref[...]).astype(BF16)
        for ch in range(len(plan)):
            y = _dot(xn, w_ref[:, ch * MXU_DIM:(ch + 1) * MXU_DIM])
            for j in range(MXU_DIM // LANES):
                y_ref[slot, ch * (MXU_DIM // LANES) + j] = y[:, j * LANES:(j + 1) * LANES]

    def finish(slot):
        cos, sin = cos_ref[...], sin_ref[...]
        for ch, (roped, dests) in enumerate(plan):
            for j in range(MXU_DIM // LANES):
                plane = ch * (MXU_DIM // LANES) + j
                if roped:
                    gain = gain_ref[:, plane * LANES:(plane + 1) * LANES]
                    y_ref[slot, plane] = _rope(_rms_rows(y_ref[slot, plane], gain), cos, sin)
                for oi, d, c0 in dests:
                    for r in range(d):
                        out_refs[oi][r, :, c0 + j * LANES:c0 + (j + 1) * LANES] = (
                            y_ref[slot, plane, pl.ds(r, tm // d, stride=d), :].astype(BF16))

    @pl.when(step == 0)
    def _():
        project(0)

    for parity in range(2):
        @pl.when((step > 0) & (step < n_tiles) & (step % 2 == parity))
        def _():
            project(parity)
            finish(1 - parity)

    @pl.when(step == n_tiles)
    def _():
        finish((n_tiles - 1) % 2)


def _b_proj_res(h2d, batch, seq, norm_gain, w, head_gain_row, plan, outs, cos, sin):
    n = w.shape[1]
    tm = ROW_TILE
    per_seq = seq // tm
    n_tiles = batch * per_seq
    full = lambda shape: pl.BlockSpec(shape, lambda s: (0,) * len(shape))
    ahead = lambda s: jnp.minimum(s, n_tiles - 1)
    behind = lambda s: jnp.maximum(s - 1, 0)
    return pl.pallas_call(
        functools.partial(_b_proj_res_kernel, plan, len(outs), n_tiles),
        grid=(n_tiles + 1,),
        in_specs=[pl.BlockSpec((tm, D_MODEL), lambda s: (ahead(s), 0)),
                  full((1, D_MODEL)), full((D_MODEL, n)), full((1, n)),
                  pl.BlockSpec((tm, B_HEAD_DIM), lambda s: (behind(s), 0)),
                  pl.BlockSpec((tm, B_HEAD_DIM), lambda s: (behind(s), 0))],
        out_specs=[pl.BlockSpec((None, d, tm // d, width),
                                lambda s: (behind(s) // per_seq, 0, behind(s) % per_seq, 0))
                   for d, width in outs],
        out_shape=[jax.ShapeDtypeStruct((batch, d, seq // d, width), BF16) for d, width in outs],
        scratch_shapes=[pltpu.VMEM((2, n // LANES, tm, LANES), F32)],
        compiler_params=_params(("arbitrary",)),
        name="b_proj_res",
    )(h2d, norm_gain.reshape(1, D_MODEL), w.astype(BF16), head_gain_row, cos, sin)


def _dilated_shifted_kernel(q_ref, kc_ref, kp_ref, vc_ref, vp_ref, bias_ref, bias0_ref,
                            o_ref, l_ref, kbuf, vbuf):
    tl = q_ref.shape[0]
    blk = BAND_BLOCK
    kbuf[0:blk, :] = kp_ref[...]
    kbuf[blk:, :] = kc_ref[...]
    vbuf[0:blk, :] = vp_ref[...]
    vbuf[blk:, :] = vc_ref[...]
    bias = bias_ref[...]
    bias_first = jnp.where(pl.program_id(2) == 0, bias0_ref[...], bias)
    lane = lax.broadcasted_iota(jnp.int32, (blk, LANES), 1)
    for i in range(tl // blk):
        rows = slice(i * blk, (i + 1) * blk)
        keys = slice(i * blk, (i + 2) * blk)
        l_tile = jnp.zeros((blk, LANES), F32)
        for h in range(B_HEADS):
            hs = slice(h * B_HEAD_DIM, (h + 1) * B_HEAD_DIM)
            s = _dot_nt(q_ref[rows, hs], kbuf[keys, hs]) + (bias_first if i == 0 else bias)
            p = jnp.exp2(s)
            o_ref[rows, hs] = _dot(p.astype(BF16), vbuf[keys, hs]).astype(o_ref.dtype)
            l_tile = jnp.where(lane == h, jnp.sum(p, axis=-1, keepdims=True), l_tile)
        l_ref[rows, :] = l_tile


def _band_shift_biases(shift_log2):
    qi = np.arange(BAND_BLOCK)[:, None]
    kk = np.arange(2 * BAND_BLOCK)[None, :]
    band = (kk >= qi) & (kk <= qi + BAND_BLOCK)
    band0 = band & (kk >= BAND_BLOCK)
    neg = jnp.full((BAND_BLOCK, 2 * BAND_BLOCK), NEG_INF, F32)
    return jnp.where(band, -shift_log2, neg), jnp.where(band0, -shift_log2, neg)


def _dilated_shifted(q_g, kv_d, d, batch, seq, biases):
    length = seq // d
    tl = min(1024, length)
    width = B_HEADS * B_HEAD_DIM
    sub = tl // BAND_BLOCK

    def cur(part):
        return pl.BlockSpec((None, None, tl, width), lambda b, r, n: (b, r, n, part))

    def prev(part):
        return pl.BlockSpec((None, None, BAND_BLOCK, width),
                            lambda b, r, n: (b, r, jnp.maximum(n * sub - 1, 0), part))

    bias = pl.BlockSpec((BAND_BLOCK, 2 * BAND_BLOCK), lambda b, r, n: (0, 0))
    return pl.pallas_call(
        _dilated_shifted_kernel,
        grid=(batch, d, length // tl),
        in_specs=[cur(0), cur(0), prev(0), cur(1), prev(1), bias, bias],
        out_specs=[cur(0), pl.BlockSpec((None, None, tl, LANES), lambda b, r, n: (b, r, n, 0))],
        out_shape=[jax.ShapeDtypeStruct((batch, d, length, width), BF16),
                   jax.ShapeDtypeStruct((batch, d, length, LANES), F32)],
        scratch_shapes=[pltpu.VMEM((tl + BAND_BLOCK, width), BF16)] * 2,
        compiler_params=_params(("parallel", "parallel", "arbitrary")),
        name=f"dilated_shifted_d{d}",
    )(q_g, kv_d, kv_d, kv_d, kv_d, *biases)


def _b_out_sum_kernel(dils, h_ref, *rest):
    n = len(dils)
    a_refs, l_refs = rest[:n], rest[n:2 * n]
    w_ref, out_ref, acc_ref, den_ref, merged_ref = rest[2 * n:]
    tm = h_ref.shape[0]
    for g, d in enumerate(dils):
        for r in range(d):
            idx = pl.ds(r, tm // d, stride=d)
            planes = [(den_ref, (idx, slice(None)), l_refs[g][r])]
            planes += [(acc_ref, (h, idx, slice(None)),
                        a_refs[g][r, :, h * B_HEAD_DIM:(h + 1) * B_HEAD_DIM].astype(F32))
                       for h in range(B_HEADS)]
            for ref, at, val in planes:
                ref[at] = val if g == 0 else ref[at] + val
    den = den_ref[...]
    for h in range(B_HEADS):
        hs = slice(h * B_HEAD_DIM, (h + 1) * B_HEAD_DIM)
        merged_ref[:, hs] = (acc_ref[h] / den[:, h:h + 1]).astype(BF16)
    out_ref[...] = h_ref[...] + _dot(merged_ref[...], w_ref[...])


def _b_out_sum(h2d, batch, seq, accs, dens, w_out):
    tm = ROW_TILE
    per_seq = seq // tm
    dils = B_DILATIONS
    tile = pl.BlockSpec((tm, D_MODEL), lambda b, i: (b * per_seq + i, 0))
    res = lambda d, width: pl.BlockSpec((None, d, tm // d, width), lambda b, i: (b, 0, i, 0))
    return pl.pallas_call(
        functools.partial(_b_out_sum_kernel, dils),
        grid=(batch, per_seq),
        in_specs=([tile] + [res(d, D_MODEL) for d in dils] + [res(d, LANES) for d in dils]
                  + [pl.BlockSpec((D_MODEL, D_MODEL), lambda b, i: (0, 0))]),
        out_specs=tile,
        out_shape=jax.ShapeDtypeStruct(h2d.shape, F32),
        scratch_shapes=[pltpu.VMEM((B_HEADS, tm, B_HEAD_DIM), F32), pltpu.VMEM((tm, LANES), F32),
                        pltpu.VMEM((tm, D_MODEL), BF16)],
        compiler_params=_params(("parallel", "parallel")),
        name="b_out_sum",
    )(h2d, *accs, *dens, w_out.astype(BF16))


def _route(logits):
    tm = logits.shape[0]
    lane = lax.broadcasted_iota(jnp.int32, (tm, LANES), 1)
    lanef = lane.astype(F32)
    far = float(LANES)
    is_g = lane < MOE_GROUPS
    g_max = jnp.max(jnp.where(is_g, logits, NEG_INF), axis=-1, keepdims=True)
    g_sum = jnp.sum(jnp.where(is_g, jnp.exp(logits - g_max), 0.0), axis=-1, keepdims=True)
    g_idx = jnp.min(jnp.where(is_g, jnp.where(logits == g_max, lanef, far), far), axis=-1, keepdims=True)
    per_group = MOE_EXPERTS // MOE_GROUPS
    e_lo = ROUTE_E0 + g_idx * per_group
    in_group = jnp.where(lanef >= e_lo, jnp.where(lanef < e_lo + per_group, 1.0, 0.0), 0.0)
    cand1 = jnp.where(in_group > 0.0, logits, NEG_INF)
    v1 = jnp.max(cand1, axis=-1, keepdims=True)
    i1 = jnp.min(jnp.where(cand1 == v1, lanef, far), axis=-1, keepdims=True)
    cand2 = jnp.where(lanef == i1, NEG_INF, cand1)
    v2 = jnp.max(cand2, axis=-1, keepdims=True)
    i2 = jnp.min(jnp.where(cand2 == v2, lanef, far), axis=-1, keepdims=True)
    e21 = jnp.exp(v2 - v1)
    w1 = 1.0 / (1.0 + e21)
    w2 = e21 / (1.0 + e21)
    comb = (jnp.where(lanef == i1, w1, 0.0) + jnp.where(lanef == i2, w2, 0.0)) / g_sum
    return comb, jnp.where(lanef == g_idx, 1.0, 0.0)


def _moe_kernel(h_ref, g_ref, wr12_ref, wr1_ref, br_ref, tri_ref, wg_ref, wu_ref, wd_ref, out_ref,
                xn_ref, c1_ref, c2_ref, posc_ref, posr_ref, hid_ref, meta_ref):
    grp = pl.program_id(1)
    per_group = MOE_EXPERTS // MOE_GROUPS
    tm = h_ref.shape[0]
    ch = MOE_CHUNK

    @pl.when(grp == 0)
    def _():
        xn = _rms_rows(h_ref[...], g_ref[...])
        x1 = xn.astype(BF16)
        x2 = (xn - x1.astype(F32)).astype(BF16)
        first = _dot(x1, wr12_ref[...])
        logits = first[:, :LANES] + (first[:, LANES:] + _dot(x2, wr1_ref[...])) + br_ref[...]
        comb, onehot = _route(logits)
        c1 = comb.astype(BF16)
        c1_ref[...] = c1
        c2_ref[...] = (comb - c1.astype(F32)).astype(BF16)
        xn_ref[...] = x1
        count = _dot(tri_ref[...], onehot.astype(BF16))
        half = ch // 2
        padded = jnp.floor((count[tm - 1:tm, :] + (half - 1)) * (1.0 / half)) * half
        lane1 = lax.broadcasted_iota(jnp.int32, (1, LANES), 1)
        prev = jnp.where(lane1 >= 1, pltpu.roll(padded, 1, 1), 0.0)
        start = (prev + jnp.where(lane1 >= 2, pltpu.roll(prev, 1, 1), 0.0)
                 + jnp.where(lane1 >= 3, pltpu.roll(prev, 2, 1), 0.0))
        slot = jnp.sum(onehot * (start + count - 1.0), axis=-1, keepdims=True)
        slot_b = jnp.broadcast_to(slot, (tm, LANES))
        posc_ref[...] = slot_b
        posr_ref[...] = slot_b.T[0:8, :]
        for g in range(MOE_GROUPS):
            pick = lane1 == g
            size = jnp.sum(jnp.where(pick, padded, 0.0)).astype(jnp.int32)
            meta_ref[g] = size // ch
            meta_ref[MOE_GROUPS + g] = jnp.sum(jnp.where(pick, start, 0.0)).astype(jnp.int32)
            meta_ref[2 * MOE_GROUPS + g] = (size % ch) // half
        out_ref[...] = h_ref[...]

    def run_chunk(first, size):
        first_slot = first.astype(F32)
        lane = lax.broadcasted_iota(jnp.int32, (size, LANES), 1)
        rows = lax.broadcasted_iota(jnp.int32, (size, tm), 0).astype(F32) + first_slot
        gather = jnp.where(posr_ref[0:1, :] == rows, 1.0, 0.0).astype(BF16)
        xs = _dot(gather, xn_ref[...]).astype(BF16)
        cs = _dot(gather, c1_ref[...]) + _dot(gather, c2_ref[...])
        for e in range(per_group):
            gate_w = jnp.sum(jnp.where(lane == grp * per_group + (ROUTE_E0 + e), cs, 0.0),
                             axis=-1, keepdims=True)
            g = _dot(xs, wg_ref[e])
            u = _dot(xs, wu_ref[e])
            hid_ref[0:size, e * MOE_FF:(e + 1) * MOE_FF] = (
                (g / (1.0 + jnp.exp(-g))) * u * gate_w).astype(BF16)
        y = _dot(hid_ref[0:size, :], wd_ref[...]).astype(BF16)
        cols = lax.broadcasted_iota(jnp.int32, (tm, size), 1).astype(F32) + first_slot
        scatter = jnp.where(posc_ref[:, 0:1] == cols, 1.0, 0.0).astype(BF16)
        out_ref[...] += _dot(scatter, y)

    base = meta_ref[MOE_GROUPS + grp]
    full_chunks = meta_ref[grp]

    @pl.loop(0, full_chunks)
    def _(c):
        run_chunk(base + c * ch, ch)

    @pl.when(meta_ref[2 * MOE_GROUPS + grp] == 1)
    def _():
        run_chunk(base + full_chunks * ch, ch // 2)


def _moe_weights(group_w, group_b, expert_w, expert_b, w_gate, w_up, w_down):
    pad = LANES - MOE_GROUPS - MOE_EXPERTS
    w_r = jnp.pad(jnp.concatenate([group_w, expert_w], axis=-1), ((0, 0), (0, 0), (0, pad)))
    r1 = w_r.astype(BF16)
    r2 = (w_r - r1.astype(F32)).astype(BF16)
    b_r = jnp.pad(jnp.concatenate([group_b, expert_b], axis=-1), ((0, 0), (0, pad)))[:, None, :]
    layers = w_gate.shape[0]
    per_group = MOE_EXPERTS // MOE_GROUPS
    by_group = (layers, MOE_GROUPS, per_group, D_MODEL, MOE_FF)
    w_g = w_gate.astype(BF16).reshape(by_group)
    w_u = w_up.astype(BF16).reshape(by_group)
    w_d = w_down.astype(BF16).reshape(layers, MOE_GROUPS, per_group * MOE_FF, D_MODEL)
    return jnp.concatenate([r1, r2], axis=-1), r1, b_r, w_g, w_u, w_d


def _moe(h2d, norm_gain, layer, weights):
    wr12, wr1, b_r, w_g, w_u, w_d = weights
    t = h2d.shape[0]
    tm = WIDE_ROW_TILE
    per_group = MOE_EXPERTS // MOE_GROUPS
    tile = pl.BlockSpec((tm, D_MODEL), lambda i, g: (i, 0))
    of_layer = lambda shape: pl.BlockSpec((None,) + shape, lambda i, g: (layer,) + (0,) * len(shape))
    tri = jnp.asarray(np.tril(np.ones((tm, tm), np.float32)), BF16)
    experts = pl.BlockSpec((None, None, per_group, D_MODEL, MOE_FF), lambda i, g: (layer, g, 0, 0, 0))
    return pl.pallas_call(
        _moe_kernel,
        grid=(t // tm, MOE_GROUPS),
        in_specs=[tile, pl.BlockSpec((1, D_MODEL), lambda i, g: (0, 0)),
                  of_layer((D_MODEL, 2 * LANES)), of_layer((D_MODEL, LANES)), of_layer((1, LANES)),
                  pl.BlockSpec((tm, tm), lambda i, g: (0, 0)), experts, experts,
                  pl.BlockSpec((None, None, per_group * MOE_FF, D_MODEL), lambda i, g: (layer, g, 0, 0))],
        out_specs=tile,
        out_shape=jax.ShapeDtypeStruct((t, D_MODEL), F32),
        scratch_shapes=[pltpu.VMEM((tm, D_MODEL), BF16), pltpu.VMEM((tm, LANES), BF16),
                        pltpu.VMEM((tm, LANES), BF16), pltpu.VMEM((tm, LANES), F32),
                        pltpu.VMEM((8, tm), F32), pltpu.VMEM((MOE_CHUNK, per_group * MOE_FF), BF16),
                        pltpu.SMEM((3 * MOE_GROUPS,), jnp.int32)],
        compiler_params=_params(("parallel", "arbitrary")),
        name="moe",
    )(h2d, norm_gain.reshape(1, D_MODEL), wr12, wr1, b_r, tri, w_g, w_u, w_d)


def kernel(x, positions, a_norm, a_w_in, a_b_f, a_q_gain, a_k_gain, a_w_out, kv_norm, kv_w, kv_k_gain, b_norm, b_w_q, b_q_gain, b_w_out, ffn_norm, moe_group_w, moe_group_b, moe_expert_w, moe_expert_b, moe_w_gate, moe_w_up, moe_w_down):
    batch, seq, _ = x.shape
    n_a = a_norm.shape[0]
    n_b = b_norm.shape[0]
    b_width = B_HEADS * B_HEAD_DIM
    n_groups = len(B_DILATIONS)
    chunks = b_width // MXU_DIM

    a_shifts = [(A_HEAD_DIM ** 0.5) * jnp.max(jnp.abs(a_q_gain[i])) * jnp.max(jnp.abs(a_k_gain[i]))
                for i in range(n_a)]
    b_shifts = [(B_HEAD_DIM ** 0.5) * jnp.max(jnp.abs(b_q_gain[j])) * jnp.max(jnp.abs(kv_k_gain))
                for j in range(n_b)]

    moe_weights = _moe_weights(moe_group_w, moe_group_b, moe_expert_w, moe_expert_b,
                               moe_w_gate, moe_w_up, moe_w_down)

    def trunk(fixed_shift, x):
        h = x.reshape(batch * seq, D_MODEL)
        kv_sh = cos = sin = None
        for layer in range(n_a + n_b):
            if layer < n_a:
                i = layer
                q, k, vt = _a_proj(h, seq, a_norm[i], a_w_in[i], a_b_f[i], a_q_gain[i], a_k_gain[i],
                                   a_shifts[i])
                o = _fox_attention(q, k, vt, batch, seq, fixed_shift)
                h = _out_proj(h, o.reshape(batch * seq, D_MODEL), a_w_out[i])
            else:
                j = layer - n_a
                k_gain_row = jnp.concatenate(
                    [jnp.tile(kv_k_gain, B_HEADS), jnp.ones((b_width,), F32)]).reshape(1, 2 * b_width)
                q_gain_row = (jnp.tile(b_q_gain[j], (1, B_HEADS)) * (B_HEAD_DIM ** -0.5)).reshape(
                    1, n_groups * b_width)
                if j == 0:
                    cos, sin = _rope_tables(positions)
                if fixed_shift:
                    if j == 0:
                        kv_plan = tuple((ch < chunks, tuple((g, d, ch * MXU_DIM) for g, d in enumerate(B_DILATIONS)))
                                        for ch in range(2 * chunks))
                        kv_sh = _b_proj_res(h, batch, seq, kv_norm, kv_w, k_gain_row, kv_plan,
                                            [(d, 2 * b_width) for d in B_DILATIONS], cos, sin)
                    q_plan = tuple((True, ((ch // chunks, B_DILATIONS[ch // chunks], (ch % chunks) * MXU_DIM),))
                                   for ch in range(n_groups * chunks))
                    q_res = _b_proj_res(h, batch, seq, b_norm[j], b_w_q[j], q_gain_row * LOG2E, q_plan,
                                        [(d, b_width) for d in B_DILATIONS], cos, sin)
                    biases = _band_shift_biases(b_shifts[j] * LOG2E)
                    accs, dens = zip(*[_dilated_shifted(q_res[g], kv_sh[g], d, batch, seq, biases)
                                       for g, d in enumerate(B_DILATIONS)])
                    h = _b_out_sum(h, batch, seq, accs, dens, b_w_out[j])
                else:
                    if j == 0:
                        kv_sh = _b_proj(h, kv_norm, kv_w, k_gain_row, b_width, cos, sin)
                    q_all = _b_proj(h, b_norm[j], b_w_q[j], q_gain_row, n_groups * b_width, cos, sin)
                    outs, lses = zip(*[_dilated_group(q_all, kv_sh, g, batch, seq) for g in range(n_groups)])
                    h = _b_out(h, outs, lses, b_w_out[j])
            h = _moe(h, ffn_norm[layer], layer, moe_weights)
        return h.reshape(batch, seq, D_MODEL)

    largest = functools.reduce(jnp.maximum, a_shifts + b_shifts)
    return lax.cond(largest <= MAX_SAFE_SHIFT, functools.partial(trunk, True),
                    functools.partial(trunk, False), x)
```

```python
import functools
import math

import numpy as np
import jax
import jax.numpy as jnp
from jax import lax
from jax.experimental import pallas as pl
from jax.experimental.pallas import tpu as pltpu

F32 = jnp.float32
BF16 = jnp.bfloat16

D_MODEL = 1024
EPS = 1e-6
ROPE_THETA = 10000.0
A_HEADS = 16
A_HEAD_DIM = 64
B_HEADS = 8
B_HEAD_DIM = 128
B_WINDOWS = (128, 512, 2048)
B_DILATIONS = (1, 4, 16)
BAND_BLOCK = 128
assert all(w // d == BAND_BLOCK for w, d in zip(B_WINDOWS, B_DILATIONS))
MOE_GROUPS = 4
MOE_EXPERTS = 16
MOE_FF = 256
MOE_CHUNK = 256

ROW_TILE = 512
WIDE_ROW_TILE = 1024
LANES = 128
MXU_DIM = 256
VMEM_LIMIT = 56 * 1024 * 1024

NEG_INF = float("-inf")
LOG2E = math.log2(math.e)
FOX_BLOCK = 512
FOX_UNROLL = 8
MAX_SAFE_SHIFT = 40.0

X_BASE = A_HEAD_DIM
ROUTE_E0 = MOE_GROUPS


def _params(sem):
    return pltpu.CompilerParams(dimension_semantics=sem, vmem_limit_bytes=VMEM_LIMIT)


def _rms_rows(x, gain_row):
    ms = jnp.mean(x * x, axis=-1, keepdims=True)
    return x * lax.rsqrt(ms + EPS) * gain_row


def _dot(a, b):
    return jnp.dot(a, b, preferred_element_type=F32)


def _dot_nt(a, b):
    return lax.dot_general(a, b, (((1,), (1,)), ((), ())), preferred_element_type=F32)


def _split3(x):
    p1 = x.astype(BF16)
    r1 = x - p1.astype(F32)
    p2 = r1.astype(BF16)
    p3 = (r1 - p2.astype(F32)).astype(BF16)
    return p1, p2, p3


def _seg_norm(x, ones_bd, seg):
    ss = _dot((x * x).astype(BF16), ones_bd)
    return x * lax.rsqrt(ss * (1.0 / seg) + EPS)


def _rope_table_kernel(pos_ref, inv_ref, sign_ref, cos_ref, sin_ref):
    ang = pos_ref[...].astype(F32) * inv_ref[...]
    cos_ref[...] = jnp.cos(ang)
    sin_ref[...] = jnp.sin(ang) * sign_ref[...]


def _rope_tables(positions):
    t = positions.size
    tm = WIDE_ROW_TILE
    half = B_HEAD_DIM // 2
    inv = ROPE_THETA ** (-jnp.arange(half, dtype=F32) / half)
    inv = jnp.concatenate([inv, inv]).reshape(1, B_HEAD_DIM)
    sign = jnp.concatenate([-jnp.ones((half,), F32), jnp.ones((half,), F32)]).reshape(1, B_HEAD_DIM)
    row = pl.BlockSpec((1, B_HEAD_DIM), lambda i: (0, 0))
    tab = pl.BlockSpec((tm, B_HEAD_DIM), lambda i: (i, 0))
    return pl.pallas_call(
        _rope_table_kernel,
        grid=(t // tm,),
        in_specs=[pl.BlockSpec((tm, 1), lambda i: (i, 0)), row, row],
        out_specs=[tab, tab],
        out_shape=[jax.ShapeDtypeStruct((t, B_HEAD_DIM), F32)] * 2,
        compiler_params=_params(("parallel",)),
        name="rope_tables",
    )(positions.reshape(t, 1), inv, sign)


def _rope(x, cos, sin_signed):
    return x * cos + pltpu.roll(x, B_HEAD_DIM // 2, 1) * sin_signed


def _a_proj_kernel(tiles_per_seq, h_ref, g_ref, wqkv_ref, wf_ref, bf_ref, qg_ref, kg_ref,
                   ones_ref, tri_ref, selq_ref, selk_ref, shift_ref, q_ref, k_ref, vt_ref, carry_ref):
    tm = h_ref.shape[0]

    @pl.when(pl.program_id(0) % tiles_per_seq == 0)
    def _():
        carry_ref[...] = jnp.zeros_like(carry_ref)

    xn = _rms_rows(h_ref[...], g_ref[...]).astype(BF16)

    fl = _dot(xn, wf_ref[...]) + bf_ref[...]
    lf = jnp.minimum(fl, 0.0) - jnp.log(1.0 + jnp.exp(-jnp.abs(fl)))
    tri = tri_ref[...]
    l1, l2, l3 = _split3(lf)
    c = _dot(tri, l1) + _dot(tri, l2) + _dot(tri, l3) + carry_ref[0:1, :]
    carry_ref[0:1, :] = c[tm - 1:tm, :]

    lane = lax.broadcasted_iota(jnp.int32, (tm, LANES), 1)
    ck = c * LOG2E
    cq = ck - shift_ref[...]
    pieces = [p.astype(F32) for p in _split3(cq) + _split3(ck)]
    e = jnp.where(lane == 6 * A_HEADS, 1.0, 0.0)
    for n in reversed(range(6)):
        piece = pieces[n] if n == 0 else pltpu.roll(pieces[n], n * A_HEADS, 1)
        e = jnp.where((lane >= n * A_HEADS) & (lane < (n + 1) * A_HEADS), piece, e)
    e = e.astype(BF16)
    v_extra = jnp.where(lane == X_BASE, 1.0, 0.0)
    low = lane < A_HEAD_DIM
    tkv = vt_ref.shape[-1]

    ones_bd = ones_ref[...]
    n_chunks = D_MODEL // MXU_DIM
    for ch in range(n_chunks):
        cs = slice(ch * MXU_DIM, (ch + 1) * MXU_DIM)
        qf = _dot(xn, wqkv_ref[:, cs])
        kf = _dot(xn, wqkv_ref[:, D_MODEL + ch * MXU_DIM:D_MODEL + (ch + 1) * MXU_DIM])
        vf = _dot(xn, wqkv_ref[:, 2 * D_MODEL + ch * MXU_DIM:2 * D_MODEL + (ch + 1) * MXU_DIM])
        qn = _seg_norm(qf, ones_bd, A_HEAD_DIM) * qg_ref[:, cs]
        kn = _seg_norm(kf, ones_bd, A_HEAD_DIM) * kg_ref[:, cs]
        heads_per_chunk = MXU_DIM // A_HEAD_DIM
        hs = slice(ch * heads_per_chunk * LANES, (ch + 1) * heads_per_chunk * LANES)
        exq = _dot(e, selq_ref[:, hs])
        exk = _dot(e, selk_ref[:, hs])
        for j in range(heads_per_chunk):
            pair = slice((j // 2) * LANES, (j // 2 + 1) * LANES)
            blk = slice(j * LANES, (j + 1) * LANES)
            out = slice((ch * heads_per_chunk + j) * LANES, (ch * heads_per_chunk + j + 1) * LANES)
            qp, kp, vp = qn[:, pair], kn[:, pair], vf[:, pair]
            if j % 2 == 1:
                qp = pltpu.roll(qp, A_HEAD_DIM, 1)
                kp = pltpu.roll(kp, A_HEAD_DIM, 1)
                vp = pltpu.roll(vp, A_HEAD_DIM, 1)
            q_ref[:, out] = jnp.where(low, qp, exq[:, blk]).astype(BF16)
            k_ref[:, out] = jnp.where(low, kp, exk[:, blk]).astype(BF16)
            v_aug = jnp.where(low, vp, v_extra)
            for cc in range(tm // tkv):
                vt_ref[cc, ch * heads_per_chunk + j] = (
                    v_aug[cc * tkv:(cc + 1) * tkv, :].T.astype(BF16))


def _a_sel_matrices():
    selq = np.zeros((LANES, A_HEADS * LANES), np.float32)
    selk = np.zeros((LANES, A_HEADS * LANES), np.float32)
    ones_lane = 6 * A_HEADS
    for h in range(A_HEADS):
        base = h * LANES + X_BASE
        for piece in range(3):
            selq[A_HEADS * piece + h, base + piece] = 1.0
            selq[ones_lane, base + 3 + piece] = 1.0
            selk[ones_lane, base + piece] = 1.0
            selk[A_HEADS * (3 + piece) + h, base + 3 + piece] = -1.0
    return jnp.asarray(selq, BF16), jnp.asarray(selk, BF16)


def _block_diag_ones(seg):
    idx = np.arange(MXU_DIM) // seg
    return jnp.asarray((idx[:, None] == idx[None, :]).astype(np.float32), BF16)


def _a_proj(h2d, seq, norm_gain, w_in, b_f, q_gain, k_gain, shift):
    t = h2d.shape[0]
    tm = ROW_TILE
    wide = A_HEADS * LANES
    w_qkv = w_in[:, :3 * D_MODEL].astype(BF16)
    w_f = jnp.pad(w_in[:, 3 * D_MODEL:], ((0, 0), (0, LANES - A_HEADS))).astype(BF16)
    b_row = jnp.pad(b_f, (0, LANES - A_HEADS)).reshape(1, LANES)
    qg = (jnp.tile(q_gain, A_HEADS) * (A_HEAD_DIM ** -0.5 * LOG2E)).reshape(1, D_MODEL)
    kg = jnp.tile(k_gain, A_HEADS).reshape(1, D_MODEL)
    tri = jnp.asarray(np.tril(np.ones((tm, tm), np.float32)), BF16)
    selq, selk = _a_sel_matrices()
    shift_row = jnp.full((1, LANES), LOG2E, F32) * shift
    full = lambda shape: pl.BlockSpec(shape, lambda i: (0,) * len(shape))
    out_spec = pl.BlockSpec((tm, wide), lambda i: (i, 0))
    per_tile = tm // FOX_BLOCK
    return pl.pallas_call(
        functools.partial(_a_proj_kernel, seq // tm),
        grid=(t // tm,),
        in_specs=[pl.BlockSpec((tm, D_MODEL), lambda i: (i, 0)),
                  full((1, D_MODEL)), full((D_MODEL, 3 * D_MODEL)), full((D_MODEL, LANES)),
                  full((1, LANES)), full((1, D_MODEL)), full((1, D_MODEL)),
                  full((MXU_DIM, MXU_DIM)), full((tm, tm)), full((LANES, wide)), full((LANES, wide)),
                  full((1, LANES))],
        out_specs=[out_spec, out_spec,
                   pl.BlockSpec((per_tile, A_HEADS, LANES, FOX_BLOCK), lambda i: (i, 0, 0, 0))],
        out_shape=[jax.ShapeDtypeStruct((t, wide), BF16), jax.ShapeDtypeStruct((t, wide), BF16),
                   jax.ShapeDtypeStruct((t // FOX_BLOCK, A_HEADS, LANES, FOX_BLOCK), BF16)],
        scratch_shapes=[pltpu.VMEM((8, LANES), F32)],
        compiler_params=_params(("arbitrary",)),
        name="a_proj",
    )(h2d, norm_gain.reshape(1, D_MODEL), w_qkv, w_f, b_row, qg, kg,
      _block_diag_ones(A_HEAD_DIM), tri, selq, selk, shift_row)


def _fox_pair_output(accs_t, o_ref, rows):
    outs = [(a / a[X_BASE:X_BASE + 1, :]).T for a in accs_t]
    lane = lax.broadcasted_iota(jnp.int32, outs[0].shape, 1)
    o_ref[rows, :] = jnp.where(lane < A_HEAD_DIM, outs[0],
                               pltpu.roll(outs[1], A_HEAD_DIM, 1)).astype(o_ref.dtype)


def _fox_shifted_kernel(q_ref, k_ref, vt_ref, o_ref, acc_ref):
    blk = FOX_BLOCK
    row = lax.broadcasted_iota(jnp.int32, (blk, blk), 0)
    col = lax.broadcasted_iota(jnp.int32, (blk, blk), 1)
    key_visible = row <= col

    def q_tile(i, _):
        rows = pl.ds(pl.multiple_of(i * blk, blk), blk)
        qs = [q_ref[rows, hh * LANES:(hh + 1) * LANES] for hh in range(2)]

        def block(j, masked=False):
            keys = pl.ds(pl.multiple_of(j * blk, blk), blk)
            ss = [_dot_nt(k_ref[keys, hh * LANES:(hh + 1) * LANES], qs[hh]) for hh in range(2)]
            for hh in range(2):
                p = jnp.exp2(ss[hh])
                if masked:
                    p = jnp.where(key_visible, p, 0.0)
                acc_ref[hh] += _dot(vt_ref[j, hh], p.astype(BF16))

        acc_ref[...] = jnp.zeros_like(acc_ref)

        @pl.loop(0, i // FOX_UNROLL)
        def _(t):
            for u in range(FOX_UNROLL):
                block(FOX_UNROLL * t + u)

        left = i % FOX_UNROLL
        for n_left in range(FOX_UNROLL):
            @pl.when(left == n_left)
            def _():
                for u in range(n_left):
                    block(i - n_left + u)
                block(i, masked=True)

        _fox_pair_output((acc_ref[0], acc_ref[1]), o_ref, rows)
        return 0

    lax.fori_loop(0, q_ref.shape[0] // blk, q_tile, 0)


def _fox_online_kernel(q_ref, k_ref, vt_ref, o_ref):
    blk = FOX_BLOCK
    qi = pl.program_id(2)
    qs = [q_ref[:, hh * LANES:(hh + 1) * LANES] for hh in range(2)]
    row = lax.broadcasted_iota(jnp.int32, (blk, blk), 0)
    col = lax.broadcasted_iota(jnp.int32, (blk, blk), 1)
    key_visible = row <= col

    def block(j, carry, masked):
        keys = pl.ds(pl.multiple_of(j * blk, blk), blk)
        new = []
        for hh in range(2):
            m, acc = carry[2 * hh], carry[2 * hh + 1]
            s = _dot_nt(k_ref[keys, hh * LANES:(hh + 1) * LANES], qs[hh])
            if masked:
                s = jnp.where(key_visible, s, NEG_INF)
            m_new = jnp.maximum(m, jnp.max(s, axis=0, keepdims=True))
            p = jnp.exp2(s - m_new)
            acc = jnp.exp2(m - m_new) * acc + _dot(vt_ref[j, hh], p.astype(BF16))
            new += [m_new, acc]
        return tuple(new)

    init = (jnp.full((1, blk), NEG_INF, F32), jnp.zeros((LANES, blk), F32)) * 2
    carry = lax.fori_loop(0, qi, functools.partial(block, masked=False), init)
    carry = block(qi, carry, True)
    _fox_pair_output((carry[1], carry[3]), o_ref, slice(None))


def _fox_attention(q, k, vt, batch, seq, fixed_shift):
    pairs = A_HEADS // 2
    blk = FOX_BLOCK
    q3, k3 = (a.reshape(batch, seq, A_HEADS * LANES) for a in (q, k))
    vt5 = vt.reshape(batch, seq // blk, A_HEADS, LANES, blk)
    out_shape = jax.ShapeDtypeStruct((batch, seq, D_MODEL), BF16)

    def shifted(q3, k3, vt5):
        pair = pl.BlockSpec((None, seq, 2 * LANES), lambda b, p: (b, 0, p))
        return pl.pallas_call(
            _fox_shifted_kernel,
            grid=(batch, pairs),
            in_specs=[pair, pair,
                      pl.BlockSpec((None, seq // blk, 2, LANES, blk), lambda b, p: (b, 0, p, 0, 0))],
            out_specs=pl.BlockSpec((None, seq, LANES), lambda b, p: (b, 0, p)),
            out_shape=out_shape,
            scratch_shapes=[pltpu.VMEM((2, LANES, blk), F32)],
            compiler_params=_params(("parallel", "parallel")),
            name="fox_shifted",
        )(q3, k3, vt5)

    def online(q3, k3, vt5):
        return pl.pallas_call(
            _fox_online_kernel,
            grid=(batch, pairs, seq // blk),
            in_specs=[pl.BlockSpec((None, blk, 2 * LANES), lambda b, p, i: (b, i, p)),
                      pl.BlockSpec((None, seq, 2 * LANES), lambda b, p, i: (b, 0, p)),
                      pl.BlockSpec((None, seq // blk, 2, LANES, blk), lambda b, p, i: (b, 0, p, 0, 0))],
            out_specs=pl.BlockSpec((None, blk, LANES), lambda b, p, i: (b, i, p)),
            out_shape=out_shape,
            compiler_params=_params(("parallel", "parallel", "arbitrary")),
            name="fox_online",
        )(q3, k3, vt5)

    return shifted(q3, k3, vt5) if fixed_shift else online(q3, k3, vt5)


def _out_proj_kernel(h_ref, o_ref, w_ref, out_ref):
    out_ref[...] = h_ref[...] + _dot(o_ref[...], w_ref[...])


def _out_proj(h2d, o2d, w_out):
    t = h2d.shape[0]
    tm = WIDE_ROW_TILE
    tile = pl.BlockSpec((tm, D_MODEL), lambda i: (i, 0))
    return pl.pallas_call(
        _out_proj_kernel,
        grid=(t // tm,),
        in_specs=[tile, tile, pl.BlockSpec((D_MODEL, D_MODEL), lambda i: (0, 0))],
        out_specs=tile,
        out_shape=jax.ShapeDtypeStruct((t, D_MODEL), F32),
        compiler_params=_params(("parallel",)),
        name="out_proj",
    )(h2d, o2d, w_out.astype(BF16))


def _b_proj_kernel(n_rope_cols, h_ref, g_ref, w_ref, gain_ref, ones_ref, cos_ref, sin_ref, out_ref):
    xn = _rms_rows(h_ref[...], g_ref[...]).astype(BF16)
    cos, sin = cos_ref[...], sin_ref[...]
    ones_bd = ones_ref[...]
    for ch in range(w_ref.shape[1] // MXU_DIM):
        cs = slice(ch * MXU_DIM, (ch + 1) * MXU_DIM)
        y = _dot(xn, w_ref[:, cs])
        if ch * MXU_DIM < n_rope_cols:
            y = _seg_norm(y, ones_bd, B_HEAD_DIM) * gain_ref[:, cs]
            for j in range(MXU_DIM // B_HEAD_DIM):
                blk = slice(j * B_HEAD_DIM, (j + 1) * B_HEAD_DIM)
                out_ref[:, ch * MXU_DIM + j * B_HEAD_DIM:ch * MXU_DIM + (j + 1) * B_HEAD_DIM] = (
                    _rope(y[:, blk], cos, sin).astype(out_ref.dtype))
        else:
            out_ref[:, cs] = y.astype(out_ref.dtype)


def _b_proj(h2d, norm_gain, w, head_gain_row, n_rope_cols, cos, sin):
    t = h2d.shape[0]
    n = w.shape[1]
    tm = ROW_TILE
    full = lambda shape: pl.BlockSpec(shape, lambda i: (0,) * len(shape))
    return pl.pallas_call(
        functools.partial(_b_proj_kernel, n_rope_cols),
        grid=(t // tm,),
        in_specs=[pl.BlockSpec((tm, D_MODEL), lambda i: (i, 0)), full((1, D_MODEL)),
                  full((D_MODEL, n)), full((1, n)), full((MXU_DIM, MXU_DIM)),
                  pl.BlockSpec((tm, B_HEAD_DIM), lambda i: (i, 0)),
                  pl.BlockSpec((tm, B_HEAD_DIM), lambda i: (i, 0))],
        out_specs=pl.BlockSpec((tm, n), lambda i: (i, 0)),
        out_shape=jax.ShapeDtypeStruct((t, n), BF16),
        compiler_params=_params(("parallel",)),
        name="b_proj",
    )(h2d, norm_gain.reshape(1, D_MODEL), w.astype(BF16), head_gain_row,
      _block_diag_ones(B_HEAD_DIM), cos, sin)


def _dilated_kernel(q_ref, kc_ref, kp_ref, vc_ref, vp_ref, bp_ref, bc_ref, o_ref, lse_ref):
    tl = q_ref.shape[0]
    blk = BAND_BLOCK
    first_pen = jnp.where(pl.program_id(2) == 0, NEG_INF, 0.0)
    bias_prev, bias_cur = bp_ref[...], bc_ref[...]
    lane = lax.broadcasted_iota(jnp.int32, (blk, LANES), 1)
    for i in range(tl // blk):
        rows = slice(i * blk, (i + 1) * blk)
        lse_tile = jnp.zeros((blk, LANES), F32)
        for h in range(B_HEADS):
            hs = slice(h * B_HEAD_DIM, (h + 1) * B_HEAD_DIM)
            q = q_ref[rows, hs]
            if i == 0:
                k_prev, v_prev = kp_ref[:, hs], vp_ref[:, hs]
            else:
                prev_rows = slice((i - 1) * blk, i * blk)
                k_prev, v_prev = kc_ref[prev_rows, hs], vc_ref[prev_rows, hs]
            s_prev = _dot_nt(q, k_prev) + bias_prev
            if i == 0:
                s_prev = s_prev + first_pen
            s_cur = _dot_nt(q, kc_ref[rows, hs]) + bias_cur
            m = jnp.maximum(jnp.max(s_prev, axis=-1, keepdims=True),
                            jnp.max(s_cur, axis=-1, keepdims=True))
            p_prev = jnp.exp(s_prev - m)
            p_cur = jnp.exp(s_cur - m)
            l = jnp.sum(p_prev, axis=-1, keepdims=True) + jnp.sum(p_cur, axis=-1, keepdims=True)
            o = (_dot(p_prev.astype(BF16), v_prev) + _dot(p_cur.astype(BF16), vc_ref[rows, hs])) / l
            o_ref[rows, hs] = o.astype(o_ref.dtype)
            lse_tile = jnp.where(lane == h, m + jnp.log(l), lse_tile)
        lse_ref[rows, :] = lse_tile


def _band_biases():
    qi = np.arange(BAND_BLOCK)[:, None]
    kk = np.arange(BAND_BLOCK)[None, :]
    prev = np.where(kk >= qi, 0.0, -np.inf).astype(np.float32)
    cur = np.where(kk <= qi, 0.0, -np.inf).astype(np.float32)
    return jnp.asarray(prev), jnp.asarray(cur)


def _dilated_group(q_all, kv, group, batch, seq):
    d = B_DILATIONS[group]
    assert B_WINDOWS[group] // d == BAND_BLOCK and seq % (d * BAND_BLOCK) == 0
    length = seq // d
    tl = 256
    width = B_HEADS * B_HEAD_DIM
    sub = tl // BAND_BLOCK
    n_groups = len(B_DILATIONS)
    qv = q_all.reshape(batch, length, d * n_groups * width)
    kvv = kv.reshape(batch, length, d * 2 * width)

    def cur(part):
        return pl.BlockSpec((None, tl, width), lambda b, r, n: (b, n, r * 2 + part))

    def prev(part):
        return pl.BlockSpec((None, BAND_BLOCK, width),
                            lambda b, r, n: (b, jnp.maximum(n * sub - 1, 0), r * 2 + part))

    bias = pl.BlockSpec((BAND_BLOCK, BAND_BLOCK), lambda b, r, n: (0, 0))
    bias_prev, bias_cur = _band_biases()
    o, lse = pl.pallas_call(
        _dilated_kernel,
        grid=(batch, d, length // tl),
        in_specs=[pl.BlockSpec((None, tl, width), lambda b, r, n: (b, n, r * n_groups + group)),
                  cur(0), prev(0), cur(1), prev(1), bias, bias],
        out_specs=[pl.BlockSpec((None, tl, width), lambda b, r, n: (b, n, r)),
                   pl.BlockSpec((None, tl, LANES), lambda b, r, n: (b, n, r))],
        out_shape=[jax.ShapeDtypeStruct((batch, length, d * width), BF16),
                   jax.ShapeDtypeStruct((batch, length, d * LANES), F32)],
        compiler_params=_params(("parallel", "parallel", "arbitrary")),
        name=f"dilated_g{group}",
    )(qv, kvv, kvv, kvv, kvv, bias_prev, bias_cur)
    return o.reshape(batch * seq, width), lse.reshape(batch * seq, LANES)


def _b_out_kernel(h_ref, o0_ref, o1_ref, o2_ref, l0_ref, l1_ref, l2_ref, w_ref, out_ref, merged_ref):
    lses = [l0_ref[...], l1_ref[...], l2_ref[...]]
    top = jnp.maximum(jnp.maximum(lses[0], lses[1]), lses[2])
    es = [jnp.exp(x - top) for x in lses]
    den = es[0] + es[1] + es[2]
    ws = [x / den for x in es]
    o_refs = (o0_ref, o1_ref, o2_ref)
    for h in range(B_HEADS):
        hs = slice(h * B_HEAD_DIM, (h + 1) * B_HEAD_DIM)
        acc = ws[0][:, h:h + 1] * o_refs[0][:, hs].astype(F32)
        for g in (1, 2):
            acc = acc + ws[g][:, h:h + 1] * o_refs[g][:, hs].astype(F32)
        merged_ref[:, hs] = acc.astype(BF16)
    out_ref[...] = h_ref[...] + _dot(merged_ref[...], w_ref[...])


def _b_out(h2d, outs, lses, w_out):
    t = h2d.shape[0]
    tm = ROW_TILE
    tile = pl.BlockSpec((tm, D_MODEL), lambda i: (i, 0))
    ltile = pl.BlockSpec((tm, LANES), lambda i: (i, 0))
    return pl.pallas_call(
        _b_out_kernel,
        grid=(t // tm,),
        in_specs=[tile, tile, tile, tile, ltile, ltile, ltile,
                  pl.BlockSpec((D_MODEL, D_MODEL), lambda i: (0, 0))],
        out_specs=tile,
        out_shape=jax.ShapeDtypeStruct((t, D_MODEL), F32),
        scratch_shapes=[pltpu.VMEM((tm, D_MODEL), BF16)],
        compiler_params=_params(("parallel",)),
        name="b_out",
    )(h2d, *outs, *lses, w_out.astype(BF16))


def _b_proj_res_kernel(plan, n_out, n_tiles, h_ref, g_ref, w_ref, gain_ref, cos_ref, sin_ref, *rest):
    out_refs, y_ref = rest[:n_out], rest[n_out]
    tm = h_ref.shape[0]
    step = pl.program_id(0)

    def project(slot):
        xn = _rms_rows(h_ref[...], g_ref[...]).astype(BF16)
        for ch in range(len(plan)):
            y = _dot(xn, w_ref[:, ch * MXU_DIM:(ch + 1) * MXU_DIM])
            for j in range(MXU_DIM // LANES):
                y_ref[slot, ch * (MXU_DIM // LANES) + j] = y[:, j * LANES:(j + 1) * LANES]

    def finish(slot):
        cos, sin = cos_ref[...], sin_ref[...]
        for ch, (roped, dests) in enumerate(plan):
            for j in range(MXU_DIM // LANES):
                plane = ch * (MXU_DIM // LANES) + j
                if roped:
                    gain = gain_ref[:, plane * LANES:(plane + 1) * LANES]
                    y_ref[slot, plane] = _rope(_rms_rows(y_ref[slot, plane], gain), cos, sin)
                for oi, d, c0 in dests:
                    for r in range(d):
                        out_refs[oi][r, :, c0 + j * LANES:c0 + (j + 1) * LANES] = (
                            y_ref[slot, plane, pl.ds(r, tm // d, stride=d), :].astype(BF16))

    @pl.when(step == 0)
    def _():
        project(0)

    for parity in range(2):
        @pl.when((step > 0) & (step < n_tiles) & (step % 2 == parity))
        def _():
            project(parity)
            finish(1 - parity)

    @pl.when(step == n_tiles)
    def _():
        finish((n_tiles - 1) % 2)


def _b_proj_res(h2d, batch, seq, norm_gain, w, head_gain_row, plan, outs, cos, sin):
    n = w.shape[1]
    tm = ROW_TILE
    per_seq = seq // tm
    n_tiles = batch * per_seq
    full = lambda shape: pl.BlockSpec(shape, lambda s: (0,) * len(shape))
    ahead = lambda s: jnp.minimum(s, n_tiles - 1)
    behind = lambda s: jnp.maximum(s - 1, 0)
    return pl.pallas_call(
        functools.partial(_b_proj_res_kernel, plan, len(outs), n_tiles),
        grid=(n_tiles + 1,),
        in_specs=[pl.BlockSpec((tm, D_MODEL), lambda s: (ahead(s), 0)),
                  full((1, D_MODEL)), full((D_MODEL, n)), full((1, n)),
                  pl.BlockSpec((tm, B_HEAD_DIM), lambda s: (behind(s), 0)),
                  pl.BlockSpec((tm, B_HEAD_DIM), lambda s: (behind(s), 0))],
        out_specs=[pl.BlockSpec((None, d, tm // d, width),
                                lambda s: (behind(s) // per_seq, 0, behind(s) % per_seq, 0))
                   for d, width in outs],
        out_shape=[jax.ShapeDtypeStruct((batch, d, seq // d, width), BF16) for d, width in outs],
        scratch_shapes=[pltpu.VMEM((2, n // LANES, tm, LANES), F32)],
        compiler_params=_params(("arbitrary",)),
        name="b_proj_res",
    )(h2d, norm_gain.reshape(1, D_MODEL), w.astype(BF16), head_gain_row, cos, sin)


def _dilated_shifted_kernel(q_ref, kc_ref, kp_ref, vc_ref, vp_ref, bias_ref, bias0_ref,
                            o_ref, l_ref, kbuf, vbuf):
    tl = q_ref.shape[0]
    blk = BAND_BLOCK
    kbuf[0:blk, :] = kp_ref[...]
    kbuf[blk:, :] = kc_ref[...]
    vbuf[0:blk, :] = vp_ref[...]
    vbuf[blk:, :] = vc_ref[...]
    bias = bias_ref[...]
    bias_first = jnp.where(pl.program_id(2) == 0, bias0_ref[...], bias)
    lane = lax.broadcasted_iota(jnp.int32, (blk, LANES), 1)
    for i in range(tl // blk):
        rows = slice(i * blk, (i + 1) * blk)
        keys = slice(i * blk, (i + 2) * blk)
        l_tile = jnp.zeros((blk, LANES), F32)
        for h in range(B_HEADS):
            hs = slice(h * B_HEAD_DIM, (h + 1) * B_HEAD_DIM)
            s = _dot_nt(q_ref[rows, hs], kbuf[keys, hs]) + (bias_first if i == 0 else bias)
            p = jnp.exp2(s)
            o_ref[rows, hs] = _dot(p.astype(BF16), vbuf[keys, hs]).astype(o_ref.dtype)
            l_tile = jnp.where(lane == h, jnp.sum(p, axis=-1, keepdims=True), l_tile)
        l_ref[rows, :] = l_tile


def _band_shift_biases(shift_log2):
    qi = np.arange(BAND_BLOCK)[:, None]
    kk = np.arange(2 * BAND_BLOCK)[None, :]
    band = (kk >= qi) & (kk <= qi + BAND_BLOCK)
    band0 = band & (kk >= BAND_BLOCK)
    neg = jnp.full((BAND_BLOCK, 2 * BAND_BLOCK), NEG_INF, F32)
    return jnp.where(band, -shift_log2, neg), jnp.where(band0, -shift_log2, neg)


def _dilated_shifted(q_g, kv_d, d, batch, seq, biases):
    length = seq // d
    tl = min(1024, length)
    width = B_HEADS * B_HEAD_DIM
    sub = tl // BAND_BLOCK

    def cur(part):
        return pl.BlockSpec((None, None, tl, width), lambda b, r, n: (b, r, n, part))

    def prev(part):
        return pl.BlockSpec((None, None, BAND_BLOCK, width),
                            lambda b, r, n: (b, r, jnp.maximum(n * sub - 1, 0), part))

    bias = pl.BlockSpec((BAND_BLOCK, 2 * BAND_BLOCK), lambda b, r, n: (0, 0))
    return pl.pallas_call(
        _dilated_shifted_kernel,
        grid=(batch, d, length // tl),
        in_specs=[cur(0), cur(0), prev(0), cur(1), prev(1), bias, bias],
        out_specs=[cur(0), pl.BlockSpec((None, None, tl, LANES), lambda b, r, n: (b, r, n, 0))],
        out_shape=[jax.ShapeDtypeStruct((batch, d, length, width), BF16),
                   jax.ShapeDtypeStruct((batch, d, length, LANES), F32)],
        scratch_shapes=[pltpu.VMEM((tl + BAND_BLOCK, width), BF16)] * 2,
        compiler_params=_params(("parallel", "parallel", "arbitrary")),
        name=f"dilated_shifted_d{d}",
    )(q_g, kv_d, kv_d, kv_d, kv_d, *biases)


def _b_out_sum_kernel(dils, h_ref, *rest):
    n = len(dils)
    a_refs, l_refs = rest[:n], rest[n:2 * n]
    w_ref, out_ref, acc_ref, den_ref, merged_ref = rest[2 * n:]
    tm = h_ref.shape[0]
    for g, d in enumerate(dils):
        for r in range(d):
            idx = pl.ds(r, tm // d, stride=d)
            planes = [(den_ref, (idx, slice(None)), l_refs[g][r])]
            planes += [(acc_ref, (h, idx, slice(None)),
                        a_refs[g][r, :, h * B_HEAD_DIM:(h + 1) * B_HEAD_DIM].astype(F32))
                       for h in range(B_HEADS)]
            for ref, at, val in planes:
                ref[at] = val if g == 0 else ref[at] + val
    den = den_ref[...]
    for h in range(B_HEADS):
        hs = slice(h * B_HEAD_DIM, (h + 1) * B_HEAD_DIM)
        merged_ref[:, hs] = (acc_ref[h] / den[:, h:h + 1]).astype(BF16)
    out_ref[...] = h_ref[...] + _dot(merged_ref[...], w_ref[...])


def _b_out_sum(h2d, batch, seq, accs, dens, w_out):
    tm = ROW_TILE
    per_seq = seq // tm
    dils = B_DILATIONS
    tile = pl.BlockSpec((tm, D_MODEL), lambda b, i: (b * per_seq + i, 0))
    res = lambda d, width: pl.BlockSpec((None, d, tm // d, width), lambda b, i: (b, 0, i, 0))
    return pl.pallas_call(
        functools.partial(_b_out_sum_kernel, dils),
        grid=(batch, per_seq),
        in_specs=([tile] + [res(d, D_MODEL) for d in dils] + [res(d, LANES) for d in dils]
                  + [pl.BlockSpec((D_MODEL, D_MODEL), lambda b, i: (0, 0))]),
        out_specs=tile,
        out_shape=jax.ShapeDtypeStruct(h2d.shape, F32),
        scratch_shapes=[pltpu.VMEM((B_HEADS, tm, B_HEAD_DIM), F32), pltpu.VMEM((tm, LANES), F32),
                        pltpu.VMEM((tm, D_MODEL), BF16)],
        compiler_params=_params(("parallel", "parallel")),
        name="b_out_sum",
    )(h2d, *accs, *dens, w_out.astype(BF16))


def _route(logits):
    tm = logits.shape[0]
    lane = lax.broadcasted_iota(jnp.int32, (tm, LANES), 1)
    lanef = lane.astype(F32)
    far = float(LANES)
    is_g = lane < MOE_GROUPS
    g_max = jnp.max(jnp.where(is_g, logits, NEG_INF), axis=-1, keepdims=True)
    g_sum = jnp.sum(jnp.where(is_g, jnp.exp(logits - g_max), 0.0), axis=-1, keepdims=True)
    g_idx = jnp.min(jnp.where(is_g, jnp.where(logits == g_max, lanef, far), far), axis=-1, keepdims=True)
    per_group = MOE_EXPERTS // MOE_GROUPS
    e_lo = ROUTE_E0 + g_idx * per_group
    in_group = jnp.where(lanef >= e_lo, jnp.where(lanef < e_lo + per_group, 1.0, 0.0), 0.0)
    cand1 = jnp.where(in_group > 0.0, logits, NEG_INF)
    v1 = jnp.max(cand1, axis=-1, keepdims=True)
    i1 = jnp.min(jnp.where(cand1 == v1, lanef, far), axis=-1, keepdims=True)
    cand2 = jnp.where(lanef == i1, NEG_INF, cand1)
    v2 = jnp.max(cand2, axis=-1, keepdims=True)
    i2 = jnp.min(jnp.where(cand2 == v2, lanef, far), axis=-1, keepdims=True)
    e21 = jnp.exp(v2 - v1)
    w1 = 1.0 / (1.0 + e21)
    w2 = e21 / (1.0 + e21)
    comb = (jnp.where(lanef == i1, w1, 0.0) + jnp.where(lanef == i2, w2, 0.0)) / g_sum
    return comb, jnp.where(lanef == g_idx, 1.0, 0.0)


def _moe_kernel(h_ref, g_ref, wr12_ref, wr1_ref, br_ref, tri_ref, wg_ref, wu_ref, wd_ref, out_ref,
                xn_ref, c1_ref, c2_ref, posc_ref, posr_ref, hid_ref, meta_ref):
    grp = pl.program_id(1)
    per_group = MOE_EXPERTS // MOE_GROUPS
    tm = h_ref.shape[0]
    ch = MOE_CHUNK

    @pl.when(grp == 0)
    def _():
        xn = _rms_rows(h_ref[...], g_ref[...])
        x1 = xn.astype(BF16)
        x2 = (xn - x1.astype(F32)).astype(BF16)
        first = _dot(x1, wr12_ref[...])
        logits = first[:, :LANES] + (first[:, LANES:] + _dot(x2, wr1_ref[...])) + br_ref[...]
        comb, onehot = _route(logits)
        c1 = comb.astype(BF16)
        c1_ref[...] = c1
        c2_ref[...] = (comb - c1.astype(F32)).astype(BF16)
        xn_ref[...] = x1
        count = _dot(tri_ref[...], onehot.astype(BF16))
        half = ch // 2
        padded = jnp.floor((count[tm - 1:tm, :] + (half - 1)) * (1.0 / half)) * half
        lane1 = lax.broadcasted_iota(jnp.int32, (1, LANES), 1)
        prev = jnp.where(lane1 >= 1, pltpu.roll(padded, 1, 1), 0.0)
        start = (prev + jnp.where(lane1 >= 2, pltpu.roll(prev, 1, 1), 0.0)
                 + jnp.where(lane1 >= 3, pltpu.roll(prev, 2, 1), 0.0))
        slot = jnp.sum(onehot * (start + count - 1.0), axis=-1, keepdims=True)
        slot_b = jnp.broadcast_to(slot, (tm, LANES))
        posc_ref[...] = slot_b
        posr_ref[...] = slot_b.T[0:8, :]
        for g in range(MOE_GROUPS):
            pick = lane1 == g
            size = jnp.sum(jnp.where(pick, padded, 0.0)).astype(jnp.int32)
            meta_ref[g] = size // ch
            meta_ref[MOE_GROUPS + g] = jnp.sum(jnp.where(pick, start, 0.0)).astype(jnp.int32)
            meta_ref[2 * MOE_GROUPS + g] = (size % ch) // half
        out_ref[...] = h_ref[...]

    def run_chunk(first, size):
        first_slot = first.astype(F32)
        lane = lax.broadcasted_iota(jnp.int32, (size, LANES), 1)
        rows = lax.broadcasted_iota(jnp.int32, (size, tm), 0).astype(F32) + first_slot
        gather = jnp.where(posr_ref[0:1, :] == rows, 1.0, 0.0).astype(BF16)
        xs = _dot(gather, xn_ref[...]).astype(BF16)
        cs = _dot(gather, c1_ref[...]) + _dot(gather, c2_ref[...])
        for e in range(per_group):
            gate_w = jnp.sum(jnp.where(lane == grp * per_group + (ROUTE_E0 + e), cs, 0.0),
                             axis=-1, keepdims=True)
            g = _dot(xs, wg_ref[e])
            u = _dot(xs, wu_ref[e])
            hid_ref[0:size, e * MOE_FF:(e + 1) * MOE_FF] = (
                (g / (1.0 + jnp.exp(-g))) * u * gate_w).astype(BF16)
        y = _dot(hid_ref[0:size, :], wd_ref[...]).astype(BF16)
        cols = lax.broadcasted_iota(jnp.int32, (tm, size), 1).astype(F32) + first_slot
        scatter = jnp.where(posc_ref[:, 0:1] == cols, 1.0, 0.0).astype(BF16)
        out_ref[...] += _dot(scatter, y)

    base = meta_ref[MOE_GROUPS + grp]
    full_chunks = meta_ref[grp]

    @pl.loop(0, full_chunks)
    def _(c):
        run_chunk(base + c * ch, ch)

    @pl.when(meta_ref[2 * MOE_GROUPS + grp] == 1)
    def _():
        run_chunk(base + full_chunks * ch, ch // 2)


def _moe_weights(group_w, group_b, expert_w, expert_b, w_gate, w_up, w_down):
    pad = LANES - MOE_GROUPS - MOE_EXPERTS
    w_r = jnp.pad(jnp.concatenate([group_w, expert_w], axis=-1), ((0, 0), (0, 0), (0, pad)))
    r1 = w_r.astype(BF16)
    r2 = (w_r - r1.astype(F32)).astype(BF16)
    b_r = jnp.pad(jnp.concatenate([group_b, expert_b], axis=-1), ((0, 0), (0, pad)))[:, None, :]
    layers = w_gate.shape[0]
    per_group = MOE_EXPERTS // MOE_GROUPS
    by_group = (layers, MOE_GROUPS, per_group, D_MODEL, MOE_FF)
    w_g = w_gate.astype(BF16).reshape(by_group)
    w_u = w_up.astype(BF16).reshape(by_group)
    w_d = w_down.astype(BF16).reshape(layers, MOE_GROUPS, per_group * MOE_FF, D_MODEL)
    return jnp.concatenate([r1, r2], axis=-1), r1, b_r, w_g, w_u, w_d


def _moe(h2d, norm_gain, layer, weights):
    wr12, wr1, b_r, w_g, w_u, w_d = weights
    t = h2d.shape[0]
    tm = WIDE_ROW_TILE
    per_group = MOE_EXPERTS // MOE_GROUPS
    tile = pl.BlockSpec((tm, D_MODEL), lambda i, g: (i, 0))
    of_layer = lambda shape: pl.BlockSpec((None,) + shape, lambda i, g: (layer,) + (0,) * len(shape))
    tri = jnp.asarray(np.tril(np.ones((tm, tm), np.float32)), BF16)
    experts = pl.BlockSpec((None, None, per_group, D_MODEL, MOE_FF), lambda i, g: (layer, g, 0, 0, 0))
    return pl.pallas_call(
        _moe_kernel,
        grid=(t // tm, MOE_GROUPS),
        in_specs=[tile, pl.BlockSpec((1, D_MODEL), lambda i, g: (0, 0)),
                  of_layer((D_MODEL, 2 * LANES)), of_layer((D_MODEL, LANES)), of_layer((1, LANES)),
                  pl.BlockSpec((tm, tm), lambda i, g: (0, 0)), experts, experts,
                  pl.BlockSpec((None, None, per_group * MOE_FF, D_MODEL), lambda i, g: (layer, g, 0, 0))],
        out_specs=tile,
        out_shape=jax.ShapeDtypeStruct((t, D_MODEL), F32),
        scratch_shapes=[pltpu.VMEM((tm, D_MODEL), BF16), pltpu.VMEM((tm, LANES), BF16),
                        pltpu.VMEM((tm, LANES), BF16), pltpu.VMEM((tm, LANES), F32),
                        pltpu.VMEM((8, tm), F32), pltpu.VMEM((MOE_CHUNK, per_group * MOE_FF), BF16),
                        pltpu.SMEM((3 * MOE_GROUPS,), jnp.int32)],
        compiler_params=_params(("parallel", "arbitrary")),
        name="moe",
    )(h2d, norm_gain.reshape(1, D_MODEL), wr12, wr1, b_r, tri, w_g, w_u, w_d)


def kernel(x, positions, a_norm, a_w_in, a_b_f, a_q_gain, a_k_gain, a_w_out, kv_norm, kv_w, kv_k_gain, b_norm, b_w_q, b_q_gain, b_w_out, ffn_norm, moe_group_w, moe_group_b, moe_expert_w, moe_expert_b, moe_w_gate, moe_w_up, moe_w_down):
    batch, seq, _ = x.shape
    n_a = a_norm.shape[0]
    n_b = b_norm.shape[0]
    b_width = B_HEADS * B_HEAD_DIM
    n_groups = len(B_DILATIONS)
    chunks = b_width // MXU_DIM

    a_shifts = [(A_HEAD_DIM ** 0.5) * jnp.max(jnp.abs(a_q_gain[i])) * jnp.max(jnp.abs(a_k_gain[i]))
                for i in range(n_a)]
    b_shifts = [(B_HEAD_DIM ** 0.5) * jnp.max(jnp.abs(b_q_gain[j])) * jnp.max(jnp.abs(kv_k_gain))
                for j in range(n_b)]

    moe_weights = _moe_weights(moe_group_w, moe_group_b, moe_expert_w, moe_expert_b,
                               moe_w_gate, moe_w_up, moe_w_down)

    def trunk(fixed_shift, x):
        h = x.reshape(batch * seq, D_MODEL)
        kv_sh = cos = sin = None
        for layer in range(n_a + n_b):
            if layer < n_a:
                i = layer
                q, k, vt = _a_proj(h, seq, a_norm[i], a_w_in[i], a_b_f[i], a_q_gain[i], a_k_gain[i],
                                   a_shifts[i])
                o = _fox_attention(q, k, vt, batch, seq, fixed_shift)
                h = _out_proj(h, o.reshape(batch * seq, D_MODEL), a_w_out[i])
            else:
                j = layer - n_a
                k_gain_row = jnp.concatenate(
                    [jnp.tile(kv_k_gain, B_HEADS), jnp.ones((b_width,), F32)]).reshape(1, 2 * b_width)
                q_gain_row = (jnp.tile(b_q_gain[j], (1, B_HEADS)) * (B_HEAD_DIM ** -0.5)).reshape(
                    1, n_groups * b_width)
                if j == 0:
                    cos, sin = _rope_tables(positions)
                if fixed_shift:
                    if j == 0:
                        kv_plan = tuple((ch < chunks, tuple((g, d, ch * MXU_DIM) for g, d in enumerate(B_DILATIONS)))
                                        for ch in range(2 * chunks))
                        kv_sh = _b_proj_res(h, batch, seq, kv_norm, kv_w, k_gain_row, kv_plan,
                                            [(d, 2 * b_width) for d in B_DILATIONS], cos, sin)
                    q_plan = tuple((True, ((ch // chunks, B_DILATIONS[ch // chunks], (ch % chunks) * MXU_DIM),))
                                   for ch in range(n_groups * chunks))
                    q_res = _b_proj_res(h, batch, seq, b_norm[j], b_w_q[j], q_gain_row * LOG2E, q_plan,
                                        [(d, b_width) for d in B_DILATIONS], cos, sin)
                    biases = _band_shift_biases(b_shifts[j] * LOG2E)
                    accs, dens = zip(*[_dilated_shifted(q_res[g], kv_sh[g], d, batch, seq, biases)
                                       for g, d in enumerate(B_DILATIONS)])
                    h = _b_out_sum(h, batch, seq, accs, dens, b_w_out[j])
                else:
                    if j == 0:
                        kv_sh = _b_proj(h, kv_norm, kv_w, k_gain_row, b_width, cos, sin)
                    q_all = _b_proj(h, b_norm[j], b_w_q[j], q_gain_row, n_groups * b_width, cos, sin)
                    outs, lses = zip(*[_dilated_group(q_all, kv_sh, g, batch, seq) for g in range(n_groups)])
                    h = _b_out(h, outs, lses, b_w_out[j])
            h = _moe(h, ffn_norm[layer], layer, moe_weights)
        return h.reshape(batch, seq, D_MODEL)

    largest = functools.reduce(jnp.maximum, a_shifts + b_shifts)
    return lax.cond(largest <= MAX_SAFE_SHIFT, functools.partial(trunk, True),
                    functools.partial(trunk, False), x)
```
